```python
import math
import jax, jax.numpy as jnp
from jax import lax
import numpy as np

D_MODEL = 1024
BATCH = 2
SEQ = 8192
DEPTH = 1

CHUNK = 64
Q_BLOCK = 128
EPS = 1e-6
ROPE_THETA = 10000.0
DIFF_HEADS = 4
DIFF_HEAD_DIM = 64
DIFF_V_DIM = 2 * DIFF_HEAD_DIM
DIFF_WIDTH = DIFF_HEADS * 2 * DIFF_HEAD_DIM
CONV_WIDTH = 512
CONV_K = 3
N_MEM = 256
MEM_HEADS = 4
MEM_HEAD_DIM = 128
MEM_WIDTH = MEM_HEADS * MEM_HEAD_DIM
N_BRANCH = 3
BRANCH_WIDTH = 512
IN_SIZES = (DIFF_WIDTH, DIFF_WIDTH, DIFF_WIDTH,
            CONV_WIDTH, CONV_WIDTH, CONV_WIDTH,
            MEM_WIDTH, N_BRANCH * D_MODEL)
IN_TOTAL = sum(IN_SIZES)
N_EXPERTS = 32
TOP_K = 4
D_FF = D_MODEL
SWIGLU_LIMIT = 7.0
SWIGLU_ALPHA = 1.702
MOE_BLOCK = 256

kernel_name = "hybrid_diffattn_shortconv_memxattn_moe"


def rmsnorm(x, g):
    xf = x.astype(jnp.float32)
    y = xf * lax.rsqrt(jnp.mean(xf * xf, axis=-1, keepdims=True) + EPS)
    return (y * g.astype(jnp.float32)).astype(x.dtype)


def rope(x, cos, sin):
    x1, x2 = jnp.split(x, 2, axis=-1)
    return jnp.concatenate([x1 * cos - x2 * sin, x2 * cos + x1 * sin], axis=-1)


def diff_attention(q, k, v, lam):
    B, S = q.shape[0], q.shape[1]
    n_qb = S // Q_BLOCK
    qb = q.reshape(B, n_qb, Q_BLOCK, DIFF_HEADS, 2, DIFF_HEAD_DIM).swapaxes(0, 1)
    key_chunk = jnp.arange(S) // CHUNK
    scale = DIFF_HEAD_DIM ** -0.5

    def block(args):
        q_blk, bi = args
        q_chunk = (bi * Q_BLOCK + jnp.arange(Q_BLOCK)) // CHUNK
        allowed = key_chunk[None, :] <= q_chunk[:, None]
        s = jnp.einsum('bqhcd,bkhcd->bhcqk', q_blk, k).astype(jnp.float32) * scale
        p = jax.nn.softmax(jnp.where(allowed, s, -jnp.inf), axis=-1)
        a = p[:, :, 0] - lam * p[:, :, 1]
        return jnp.einsum('bhqk,bkhe->bqhe', a.astype(v.dtype), v)

    out = lax.map(block, (qb, jnp.arange(n_qb)))
    return out.swapaxes(0, 1).reshape(B, S, DIFF_HEADS, DIFF_V_DIM)


def causal_depthwise_conv(u, w):
    return lax.conv_general_dilated(
        u, w[:, None, :].astype(u.dtype), window_strides=(1,),
        padding=[(CONV_K - 1, 0)], dimension_numbers=('NWC', 'WIO', 'NWC'),
        feature_group_count=u.shape[-1])


def memory_cross_attention(q, mem, g_mem, w_mem_kv, g_q, g_k):
    B, S = q.shape[0], q.shape[1]
    kv = rmsnorm(mem, g_mem) @ w_mem_kv
    k, v = jnp.split(kv, 2, axis=-1)
    M = mem.shape[1]
    k = rmsnorm(k.reshape(B, M, MEM_HEADS, MEM_HEAD_DIM), g_k)
    v = v.reshape(B, M, MEM_HEADS, MEM_HEAD_DIM)
    q = rmsnorm(q.reshape(B, S, MEM_HEADS, MEM_HEAD_DIM), g_q)
    s = jnp.einsum('bshd,bmhd->bhsm', q, k).astype(jnp.float32) * (MEM_HEAD_DIM ** -0.5)
    p = jax.nn.softmax(s, axis=-1)
    o = jnp.einsum('bhsm,bmhd->bshd', p.astype(v.dtype), v)
    return o.reshape(B, S, MEM_WIDTH)


def moe(h, w_router, b_router, w_gate_up, b_gate_up, w_down, b_down):
    B, S, D = h.shape
    T = B * S
    xt = h.reshape(T, D)
    logits = (xt @ w_router).astype(jnp.float32) + b_router.astype(jnp.float32)
    top_vals, top_idx = lax.top_k(logits, TOP_K)
    gates = jax.nn.softmax(top_vals, axis=-1)
    n_assign = T * TOP_K
    expert_flat = top_idx.reshape(-1)
    token_flat = jnp.arange(n_assign, dtype=jnp.int32) // TOP_K
    gate_flat = gates.reshape(-1)
    order = jnp.argsort(expert_flat)
    sorted_expert = expert_flat[order]
    counts = jnp.bincount(expert_flat, length=N_EXPERTS)
    padded = (counts + MOE_BLOCK - 1) // MOE_BLOCK * MOE_BLOCK
    start = jnp.cumsum(counts) - counts
    pend = jnp.cumsum(padded)
    pstart = pend - padded
    dest = pstart[sorted_expert] + (jnp.arange(n_assign) - start[sorted_expert])
    n_rows = -(-(n_assign + N_EXPERTS * (MOE_BLOCK - 1)) // MOE_BLOCK) * MOE_BLOCK
    n_blocks = n_rows // MOE_BLOCK
    row_token = jnp.zeros((n_rows,), jnp.int32).at[dest].set(token_flat[order])
    row_gate = jnp.zeros((n_rows,), jnp.float32).at[dest].set(gate_flat[order])
    block_expert = jnp.minimum(
        jnp.searchsorted(pend, jnp.arange(n_blocks) * MOE_BLOCK, side='right'),
        N_EXPERTS - 1)
    xs = xt[row_token].reshape(n_blocks, MOE_BLOCK, D)

    def expert_block(args):
        xb, e = args
        gu = xb @ w_gate_up[e] + b_gate_up[e]
        g, u = jnp.split(gu, 2, axis=-1)
        g = jnp.minimum(g, SWIGLU_LIMIT)
        u = jnp.clip(u, -SWIGLU_LIMIT, SWIGLU_LIMIT)
        act = (u + 1.0) * (g * jax.nn.sigmoid(g * SWIGLU_ALPHA))
        return act @ w_down[e] + b_down[e]

    ys = lax.map(expert_block, (xs, block_expert)).reshape(n_rows, D)
    ys = ys * row_gate[:, None].astype(ys.dtype)
    out = jax.ops.segment_sum(ys, row_token, num_segments=T)
    return out.reshape(B, S, D)


def setup_inputs(seed: int = 0) -> dict:
    key = jax.random.key(seed)
    ks = jax.random.split(key, 26)
    f32 = jnp.float32
    L, D, E = DEPTH, D_MODEL, N_EXPERTS

    def nrm(k, shape, scale):
        return jax.random.normal(k, shape, f32) * scale

    def gain(k, shape):
        return 1.0 + 0.02 * jax.random.normal(k, shape, f32)

    return {
        "x": nrm(ks[0], (BATCH, SEQ, D), 1.0),
        "mem": nrm(ks[1], (BATCH, N_MEM, D), 1.0),
        "g_mix": gain(ks[2], (L, D)),
        "w_in": nrm(ks[3], (L, D, IN_TOTAL), D ** -0.5),
        "g_q_diff": gain(ks[4], (L, DIFF_HEAD_DIM)),
        "g_k_diff": gain(ks[5], (L, DIFF_HEAD_DIM)),
        "lambda_q1": nrm(ks[6], (L, DIFF_HEAD_DIM), 0.1),
        "lambda_k1": nrm(ks[7], (L, DIFF_HEAD_DIM), 0.1),
        "lambda_q2": nrm(ks[8], (L, DIFF_HEAD_DIM), 0.1),
        "lambda_k2": nrm(ks[9], (L, DIFF_HEAD_DIM), 0.1),
        "g_subln": gain(ks[10], (L, DIFF_V_DIM)),
        "conv_w": nrm(ks[11], (L, CONV_K, CONV_WIDTH), CONV_K ** -0.5),
        "g_mem": gain(ks[12], (L, D)),
        "w_mem_kv": nrm(ks[13], (L, D, 2 * MEM_WIDTH), D ** -0.5),
        "g_q_mem": gain(ks[14], (L, MEM_HEAD_DIM)),
        "g_k_mem": gain(ks[15], (L, MEM_HEAD_DIM)),
        "w_branch": nrm(ks[16], (L, N_BRANCH, BRANCH_WIDTH, D), BRANCH_WIDTH ** -0.5),
        "w_out": nrm(ks[17], (L, D, D), D ** -0.5),
        "g_ffn": gain(ks[18], (L, D)),
        "w_router": nrm(ks[19], (L, D, E), D ** -0.5),
        "b_router": nrm(ks[20], (L, E), 0.01),
        "w_gate_up": nrm(ks[21], (L, E, D, 2 * D_FF), D ** -0.5),
        "b_gate_up": nrm(ks[22], (L, E, 2 * D_FF), 0.02),
        "w_down": nrm(ks[23], (L, E, D_FF, D), D_FF ** -0.5),
        "b_down": nrm(ks[24], (L, E, D), 0.02),
    }


def reference(x, mem, g_mix, w_in, g_q_diff, g_k_diff, lambda_q1, lambda_k1,
              lambda_q2, lambda_k2, g_subln, conv_w, g_mem, w_mem_kv, g_q_mem,
              g_k_mem, w_branch, w_out, g_ffn, w_router, b_router, w_gate_up,
              b_gate_up, w_down, b_down):
    B, S, D = x.shape
    f32 = jnp.float32
    pos = jnp.arange(S, dtype=f32)
    inv_freq = 1.0 / (ROPE_THETA ** (jnp.arange(0, DIFF_HEAD_DIM, 2, dtype=f32) / DIFF_HEAD_DIM))
    ang = pos[:, None] * inv_freq[None, :]
    cos = jnp.cos(ang)[:, None, None, :].astype(x.dtype)
    sin = jnp.sin(ang)[:, None, None, :].astype(x.dtype)
    split_points = np.cumsum(IN_SIZES)[:-1].tolist()

    for l in range(DEPTH):
        lam_init = 0.8 - 0.6 * math.exp(-0.3 * l)
        h = rmsnorm(x, g_mix[l])
        proj = h @ w_in[l]
        dq, dk, dv, cb, cc, ch, mq, gate_logits = jnp.split(proj, split_points, axis=-1)

        dq = rope(rmsnorm(dq.reshape(B, S, DIFF_HEADS, 2, DIFF_HEAD_DIM), g_q_diff[l]), cos, sin)
        dk = rope(rmsnorm(dk.reshape(B, S, DIFF_HEADS, 2, DIFF_HEAD_DIM), g_k_diff[l]), cos, sin)
        dv = dv.reshape(B, S, DIFF_HEADS, DIFF_V_DIM)
        lam = (jnp.exp(jnp.sum(lambda_q1[l].astype(f32) * lambda_k1[l].astype(f32)))
               - jnp.exp(jnp.sum(lambda_q2[l].astype(f32) * lambda_k2[l].astype(f32)))
               + lam_init)
        a = diff_attention(dq, dk, dv, lam)
        a = rmsnorm(a, g_subln[l]) * (1.0 - lam_init)
        y_diff = a.reshape(B, S, DIFF_WIDTH)

        y_conv = cb * causal_depthwise_conv(cc * ch, conv_w[l])

        y_mem = memory_cross_attention(mq, mem, g_mem[l], w_mem_kv[l], g_q_mem[l], g_k_mem[l])

        branches = jnp.stack([y_diff, y_conv, y_mem], axis=2)
        branch_out = jnp.einsum('bsnc,ncd->bsnd', branches, w_branch[l])
        gates = jax.nn.sigmoid(gate_logits.reshape(B, S, N_BRANCH, D))
        merged = jnp.sum(gates * branch_out, axis=2)
        x = x + merged @ w_out[l]

        x = x + moe(rmsnorm(x, g_ffn[l]), w_router[l], b_router[l], w_gate_up[l],
                    b_gate_up[l], w_down[l], b_down[l])
    return x
```

```python
import functools
import math

import jax
import jax.numpy as jnp
from jax import lax
from jax.experimental import pallas as pl
from jax.experimental.pallas import tpu as pltpu

F32 = jnp.float32
BF16 = jnp.bfloat16

D_MODEL = 1024
CHUNK = 64
EPS = 1e-6
ROPE_THETA = 10000.0
DIFF_HEADS = 4
DIFF_HEAD_DIM = 64
DIFF_V_DIM = 2 * DIFF_HEAD_DIM
N_MEM = 256
MEM_HEADS = 4
MEM_HEAD_DIM = 128
SEC = 512
N_SEC = 13
N_EXPERTS = 32
TOP_K = 4
SWIGLU_LIMIT = 7.0
SWIGLU_ALPHA = 1.702
MOE_BLOCK = 256
LANES = 128
ROW_CHUNKS = D_MODEL // LANES
LOG2E = 1.4426950408889634
NEG_BIG = -1e30

TM_PROJ = 256
TQ = 256
TM_POST = 256
TC_COMB = 128
VMEM_LIMIT = 48 * 1024 * 1024


def _rms(x, eps=EPS):
    return x * lax.rsqrt(jnp.mean(x * x, axis=-1, keepdims=True) + eps)


def _mem_kv_kernel(mem_ref, g_ref, w_ref, gk_ref, k_out, v_out):
    h = (_rms(mem_ref[...]) * g_ref[...]).astype(BF16)
    kv = jnp.dot(h, w_ref[...], preferred_element_type=F32)
    for hd in range(MEM_HEADS):
        sl = slice(hd * MEM_HEAD_DIM, (hd + 1) * MEM_HEAD_DIM)
        k_out[:, sl] = (_rms(kv[:, sl]) * gk_ref[...]).astype(BF16)
    v_out[...] = kv[:, SEC:].astype(BF16)


def _mem_kv(mem2d, g_mem, w_kv, g_k):
    rows = mem2d.shape[0]
    return pl.pallas_call(
        _mem_kv_kernel,
        grid=(rows // N_MEM,),
        in_specs=[
            pl.BlockSpec((N_MEM, D_MODEL), lambda i: (i, 0)),
            pl.BlockSpec((1, D_MODEL), lambda i: (0, 0)),
            pl.BlockSpec((D_MODEL, 2 * SEC), lambda i: (0, 0)),
            pl.BlockSpec((1, MEM_HEAD_DIM), lambda i: (0, 0)),
        ],
        out_specs=[pl.BlockSpec((N_MEM, SEC), lambda i: (i, 0))] * 2,
        out_shape=[jax.ShapeDtypeStruct((rows, SEC), BF16)] * 2,
        compiler_params=pltpu.CompilerParams(vmem_limit_bytes=VMEM_LIMIT),
        name="mem_kv",
    )(mem2d, g_mem, w_kv, g_k)


def _in_proj_kernel(x_ref, g_ref, w_ref, cos_ref, sin_ref, bd_ref, gq_ref, gk_ref, gqm_ref,
                    q_out, k_out, v_out, cb_out, u_out, mq_out, sig_out):
    tm = x_ref.shape[0]
    h = (_rms(x_ref[...]) * g_ref[...]).astype(BF16)

    def proj(sec):
        return jnp.dot(h, w_ref[:, sec * SEC:(sec + 1) * SEC], preferred_element_type=F32)

    cos = jnp.concatenate([cos_ref[...]] * (SEC // LANES), axis=1)
    sin = jnp.concatenate([sin_ref[...]] * (SEC // LANES), axis=1)
    lane = lax.broadcasted_iota(jnp.int32, (tm, SEC), 1)
    first_half = (lane & (DIFF_HEAD_DIM // 2)) == 0

    def norm_rope(a, g, scale):
        ms = jnp.dot((a * a).astype(BF16), bd_ref[...], preferred_element_type=F32)
        y = a * lax.rsqrt(ms + EPS) * g
        partner = jnp.where(first_half,
                            pltpu.roll(y, SEC - DIFF_HEAD_DIM // 2, 1),
                            pltpu.roll(y, DIFF_HEAD_DIM // 2, 1))
        return (y * cos + partner * sin) * scale

    q_out[...] = norm_rope(proj(0), gq_ref[...], DIFF_HEAD_DIM ** -0.5 * LOG2E).astype(BF16)
    k_out[...] = norm_rope(proj(1), gk_ref[...], 1.0).astype(BF16)
    v_out[...] = proj(2).astype(BF16)
    cb_out[...] = proj(3).astype(BF16)
    u_out[...] = proj(4) * proj(5)
    mq = proj(6)
    for hd in range(MEM_HEADS):
        sl = slice(hd * MEM_HEAD_DIM, (hd + 1) * MEM_HEAD_DIM)
        mq_out[:, sl] = (_rms(mq[:, sl]) * gqm_ref[...]
                         * (MEM_HEAD_DIM ** -0.5 * LOG2E)).astype(BF16)
    for s in range(7, N_SEC):
        a = proj(s)
        sig_out[:, (s - 7) * SEC:(s - 6) * SEC] = (1.0 / (1.0 + jnp.exp(-a))).astype(BF16)


def _in_proj(x2d, g_mix, w_in, cos_t, sin_t, bd, gq, gk, gqm, seq):
    t = x2d.shape[0]
    tm = TM_PROJ
    tiles_per_seq = seq // tm
    row = lambda i: (i, 0)
    const = lambda i: (0, 0)
    outs = [
        jax.ShapeDtypeStruct((t, SEC), BF16),
        jax.ShapeDtypeStruct((t, SEC), BF16),
        jax.ShapeDtypeStruct((t, SEC), BF16),
        jax.ShapeDtypeStruct((t, SEC), BF16),
        jax.ShapeDtypeStruct((t, SEC), F32),
        jax.ShapeDtypeStruct((t, SEC), BF16),
        jax.ShapeDtypeStruct((t, 3 * D_MODEL), BF16),
    ]
    return pl.pallas_call(
        _in_proj_kernel,
        grid=(t // tm,),
        in_specs=[
            pl.BlockSpec((tm, D_MODEL), row),
            pl.BlockSpec((1, D_MODEL), const),
            pl.BlockSpec((D_MODEL, N_SEC * SEC), const),
            pl.BlockSpec((tm, LANES), lambda i: (i % tiles_per_seq, 0)),
            pl.BlockSpec((tm, LANES), lambda i: (i % tiles_per_seq, 0)),
            pl.BlockSpec((SEC, SEC), const),
            pl.BlockSpec((1, SEC), const),
            pl.BlockSpec((1, SEC), const),
            pl.BlockSpec((1, MEM_HEAD_DIM), const),
        ],
        out_specs=[pl.BlockSpec((tm, o.shape[1]), row) for o in outs],
        out_shape=outs,
        compiler_params=pltpu.CompilerParams(vmem_limit_bytes=VMEM_LIMIT),
        name="in_proj",
    )(x2d, g_mix, w_in, cos_t, sin_t, bd, gq, gk, gqm)


def _diff_attn_kernel(q_ref, k_ref, v_ref, lam_ref, gs_ref, o_ref, m_sc, l_sc, acc_sc, *, lam_init):
    i = pl.program_id(2)
    tq = q_ref.shape[0]
    q = q_ref[...]
    lane = lax.broadcasted_iota(jnp.int32, q.shape, 1)
    zero = jnp.zeros_like(q)
    qq = jnp.concatenate([jnp.where(lane < DIFF_HEAD_DIM, q, zero),
                          jnp.where(lane >= DIFF_HEAD_DIM, q, zero)], axis=0)
    m_sc[...] = jnp.full(m_sc.shape, NEG_BIG, F32)
    l_sc[...] = jnp.zeros(l_sc.shape, F32)
    acc_sc[...] = jnp.zeros(acc_sc.shape, F32)

    def step(j, masked):
        off = pl.multiple_of(j * tq, tq)
        k = k_ref[pl.ds(off, tq), :]
        v = v_ref[pl.ds(off, tq), :]
        s = lax.dot_general(qq, k, (((1,), (1,)), ((), ())), preferred_element_type=F32)
        if masked:
            r = lax.broadcasted_iota(jnp.int32, s.shape, 0)
            c = lax.broadcasted_iota(jnp.int32, s.shape, 1)
            s = jnp.where((c // CHUNK) <= ((r & (tq - 1)) // CHUNK), s, NEG_BIG)
        m_prev = m_sc[...]
        m_new = jnp.maximum(m_prev, jnp.max(s, axis=-1, keepdims=True))
        p = jnp.exp2(s - m_new)
        alpha = jnp.exp2(m_prev - m_new)
        l_sc[...] = alpha * l_sc[...] + jnp.sum(p, axis=-1, keepdims=True)
        acc_sc[...] = alpha * acc_sc[...] + jnp.dot(p.astype(BF16), v, preferred_element_type=F32)
        m_sc[...] = m_new

    def body(j, carry):
        step(j, False)
        return carry

    lax.fori_loop(0, i, body, 0)
    step(i, True)

    o = acc_sc[...] / l_sc[...]
    lam = (jnp.exp(jnp.sum(lam_ref[0:1, :] * lam_ref[1:2, :], axis=-1, keepdims=True))
           - jnp.exp(jnp.sum(lam_ref[2:3, :] * lam_ref[3:4, :], axis=-1, keepdims=True))
           + lam_init)
    d = o[:tq] - lam * o[tq:]
    o_ref[...] = (_rms(d) * gs_ref[...] * (1.0 - lam_init)).astype(BF16)


def _diff_attn(qn, kn, vv, lam_rows, g_subln, batch, seq, lam_init):
    t = qn.shape[0]
    nq = seq // TQ
    return pl.pallas_call(
        functools.partial(_diff_attn_kernel, lam_init=lam_init),
        grid=(batch, DIFF_HEADS, nq),
        in_specs=[
            pl.BlockSpec((TQ, DIFF_V_DIM), lambda b, h, i: (b * nq + i, h)),
            pl.BlockSpec((seq, DIFF_V_DIM), lambda b, h, i: (b, h)),
            pl.BlockSpec((seq, DIFF_V_DIM), lambda b, h, i: (b, h)),
            pl.BlockSpec((4, DIFF_HEAD_DIM), lambda b, h, i: (0, 0)),
            pl.BlockSpec((1, DIFF_V_DIM), lambda b, h, i: (0, 0)),
        ],
        out_specs=pl.BlockSpec((TQ, DIFF_V_DIM), lambda b, h, i: (b * nq + i, h)),
        out_shape=jax.ShapeDtypeStruct((t, SEC), BF16),
        scratch_shapes=[
            pltpu.VMEM((2 * TQ, 1), F32),
            pltpu.VMEM((2 * TQ, 1), F32),
            pltpu.VMEM((2 * TQ, DIFF_V_DIM), F32),
        ],
        compiler_params=pltpu.CompilerParams(vmem_limit_bytes=VMEM_LIMIT),
        name="diff_attn",
    )(qn, kn, vv, lam_rows, g_subln)


def _post_kernel(x_ref, yd_ref, cb_ref, u_ref, up_ref, mq_ref, sig_ref, km_ref, vm_ref,
                 cw_ref, wb_ref, wo_ref, gf_ref, wr_ref, br_ref, tri_ref,
                 x1_out, h2_out, eidx_out, gate_out, rank_out, cnt_out, carry_sc, *, seq):
    i = pl.program_id(0)
    tm = x_ref.shape[0]

    @pl.when(i == 0)
    def _():
        carry_sc[...] = jnp.zeros(carry_sc.shape, F32)

    u = u_ref[...]
    seq_start = (i * tm) % seq == 0
    up = jnp.where(seq_start, 0.0, up_ref[...])
    r = lax.broadcasted_iota(jnp.int32, u.shape, 0)
    u1 = jnp.where(r == 0, up[7:8, :], pltpu.roll(u, 1, 0))
    u2 = jnp.where(r == 0, up[6:7, :], jnp.where(r == 1, up[7:8, :], pltpu.roll(u, 2, 0)))
    y_conv = cb_ref[...].astype(F32) * (cw_ref[0:1, :] * u2 + cw_ref[1:2, :] * u1 + cw_ref[2:3, :] * u)

    y_mem = []
    for hd in range(MEM_HEADS):
        sl = slice(hd * MEM_HEAD_DIM, (hd + 1) * MEM_HEAD_DIM)
        s = lax.dot_general(mq_ref[:, sl], km_ref[:, sl], (((1,), (1,)), ((), ())),
                            preferred_element_type=F32)
        p = jnp.exp2(s - jnp.max(s, axis=-1, keepdims=True))
        o = jnp.dot(p.astype(BF16), vm_ref[:, sl], preferred_element_type=F32)
        y_mem.append(o / jnp.sum(p, axis=-1, keepdims=True))

    merged = (sig_ref[:, 0:D_MODEL].astype(F32)
              * jnp.dot(yd_ref[...], wb_ref[0], preferred_element_type=F32))
    merged += (sig_ref[:, D_MODEL:2 * D_MODEL].astype(F32)
               * jnp.dot(y_conv.astype(BF16), wb_ref[1], preferred_element_type=F32))
    merged += (sig_ref[:, 2 * D_MODEL:3 * D_MODEL].astype(F32)
               * jnp.dot(jnp.concatenate(y_mem, axis=1).astype(BF16), wb_ref[2],
                         preferred_element_type=F32))
    x1 = x_ref[...] + jnp.dot(merged.astype(BF16), wo_ref[...], preferred_element_type=F32)
    x1_out[...] = x1

    h2 = _rms(x1) * gf_ref[...]
    for c in range(ROW_CHUNKS):
        h2_out[:, c, :] = h2[:, c * LANES:(c + 1) * LANES]

    logits = lax.dot_general(wr_ref[...], h2, (((1,), (1,)), ((), ())),
                             precision=lax.Precision.HIGHEST,
                             preferred_element_type=F32) + br_ref[...]
    eio = lax.broadcasted_iota(jnp.int32, logits.shape, 0)
    work = logits
    vals, hots = [], []
    for k in range(TOP_K):
        mk = jnp.max(work, axis=0, keepdims=True)
        ik = jnp.min(jnp.where(work == mk, eio, N_EXPERTS), axis=0, keepdims=True)
        hot = eio == ik
        work = jnp.where(hot, -jnp.inf, work)
        vals.append(mk)
        hots.append(hot)
        eidx_out[k:k + 1, :] = ik
    ex = [jnp.exp(v - vals[0]) for v in vals]
    den = ex[0] + ex[1] + ex[2] + ex[3]
    for k in range(TOP_K):
        gate_out[k:k + 1, :] = ex[k] / den

    assign = jnp.zeros(logits.shape, F32)
    for hot in hots:
        assign = jnp.where(hot, 1.0, assign)
    before = jnp.dot(assign.astype(BF16), tri_ref[...], preferred_element_type=F32) + carry_sc[...]
    for k in range(TOP_K):
        rank_out[k:k + 1, :] = jnp.sum(jnp.where(hots[k], before, 0.0), axis=0,
                                       keepdims=True).astype(jnp.int32)
    carry_sc[...] = carry_sc[...] + jnp.sum(assign, axis=1, keepdims=True)
    cnt_out[...] = jnp.broadcast_to(carry_sc[...], cnt_out.shape)


def _post(x2d, y_diff, cb, u, mqn, sig, km, vm, conv_w, w_branch, w_out, g_ffn, w_rt, b_r, tri, seq):
    t = x2d.shape[0]
    tm = TM_POST
    row = lambda i: (i, 0)
    const = lambda i: (0, 0)
    outs = [
        jax.ShapeDtypeStruct((t, D_MODEL), F32),
        jax.ShapeDtypeStruct((t, ROW_CHUNKS, LANES), F32),
        jax.ShapeDtypeStruct((TOP_K, t), jnp.int32),
        jax.ShapeDtypeStruct((TOP_K, t), F32),
        jax.ShapeDtypeStruct((TOP_K, t), jnp.int32),
        jax.ShapeDtypeStruct((N_EXPERTS, LANES), F32),
    ]
    return pl.pallas_call(
        functools.partial(_post_kernel, seq=seq),
        grid=(t // tm,),
        in_specs=[
            pl.BlockSpec((tm, D_MODEL), row),
            pl.BlockSpec((tm, SEC), row),
            pl.BlockSpec((tm, SEC), row),
            pl.BlockSpec((tm, SEC), row),
            pl.BlockSpec((8, SEC), lambda i: (jnp.maximum(i * (tm // 8) - 1, 0), 0)),
            pl.BlockSpec((tm, SEC), row),
            pl.BlockSpec((tm, 3 * D_MODEL), row),
            pl.BlockSpec((N_MEM, SEC), lambda i: ((i * tm) // seq, 0)),
            pl.BlockSpec((N_MEM, SEC), lambda i: ((i * tm) // seq, 0)),
            pl.BlockSpec((3, SEC), const),
            pl.BlockSpec((3, SEC, D_MODEL), lambda i: (0, 0, 0)),
            pl.BlockSpec((D_MODEL, D_MODEL), const),
            pl.BlockSpec((1, D_MODEL), const),
            pl.BlockSpec((N_EXPERTS, D_MODEL), const),
            pl.BlockSpec((N_EXPERTS, 1), const),
            pl.BlockSpec((tm, tm), const),
        ],
        out_specs=[
            pl.BlockSpec((tm, D_MODEL), row),
            pl.BlockSpec((tm, ROW_CHUNKS, LANES), lambda i: (i, 0, 0)),
            pl.BlockSpec((TOP_K, tm), lambda i: (0, i)),
            pl.BlockSpec((TOP_K, tm), lambda i: (0, i)),
            pl.BlockSpec((TOP_K, tm), lambda i: (0, i)),
            pl.BlockSpec((N_EXPERTS, LANES), const),
        ],
        out_shape=outs,
        scratch_shapes=[pltpu.VMEM((N_EXPERTS, 1), F32)],
        compiler_params=pltpu.CompilerParams(dimension_semantics=("arbitrary",),
                                             vmem_limit_bytes=VMEM_LIMIT),
        name="post",
    )(x2d, y_diff, cb, u, u, mqn, sig, km, vm, conv_w, w_branch, w_out, g_ffn, w_rt, b_r, tri)


def _row_gather_copy(src_hbm, buf, sem, idx, dst_row):
    return pltpu.make_async_copy(src_hbm.at[idx], buf.at[dst_row], sem)


def _issue_rows(idx_ref, n, src_hbm, buf, sem, base):
    def body(r, carry):
        _row_gather_copy(src_hbm, buf, sem, idx_ref[0, 0, r], base + r).start()
        return carry
    lax.fori_loop(0, n, body, 0)


def _wait_rows(n, src_hbm, buf, sem, base):
    pltpu.make_async_copy(src_hbm.at[pl.ds(0, n)], buf.at[pl.ds(base, n)], sem).wait()


def _experts_kernel(be_ref, na_ref, rt_cur, rt_nxt, h2_hbm, wgu_ref, bgu_ref, wd_ref, bd_ref, rg_ref,
                    ys_out, xbuf, sem):
    b = pl.program_id(0)
    n_active = na_ref[0]
    slot = b % 2

    @pl.when(b == 0)
    def _():
        _issue_rows(rt_cur, MOE_BLOCK, h2_hbm, xbuf, sem.at[0], 0)

    @pl.when(b + 1 < n_active)
    def _():
        _issue_rows(rt_nxt, MOE_BLOCK, h2_hbm, xbuf, sem.at[1 - slot], (1 - slot) * MOE_BLOCK)

    @pl.when(b < n_active)
    def _():
        base = pl.multiple_of(slot * MOE_BLOCK, MOE_BLOCK)
        _wait_rows(MOE_BLOCK, h2_hbm, xbuf, sem.at[slot], base)
        x = jnp.concatenate([xbuf[pl.ds(base, MOE_BLOCK), c, :] for c in range(ROW_CHUNKS)],
                            axis=1).astype(BF16)
        gu = jnp.dot(x, wgu_ref[0], preferred_element_type=F32) + bgu_ref[0]
        g = jnp.minimum(gu[:, :D_MODEL], SWIGLU_LIMIT)
        u = jnp.clip(gu[:, D_MODEL:], -SWIGLU_LIMIT, SWIGLU_LIMIT)
        act = (u + 1.0) * (g * (1.0 / (1.0 + jnp.exp(-SWIGLU_ALPHA * g))))
        y = (jnp.dot(act.astype(BF16), wd_ref[0], preferred_element_type=F32) + bd_ref[0]) * rg_ref[...]
        for c in range(ROW_CHUNKS):
            ys_out[:, c, :] = y[:, c * LANES:(c + 1) * LANES]

    @pl.when(b >= n_active)
    def _():
        ys_out[...] = jnp.zeros(ys_out.shape, F32)


def _experts(block_expert, n_active, row_token, h2_rows, wgu, bgu, wd, bd, row_gate):
    n_blocks = block_expert.shape[0]
    ff2 = wgu.shape[2]
    grid_spec = pltpu.PrefetchScalarGridSpec(
        num_scalar_prefetch=2,
        grid=(n_blocks,),
        in_specs=[
            pl.BlockSpec((1, 1, MOE_BLOCK), lambda b, be, na: (b, 0, 0), memory_space=pltpu.SMEM),
            pl.BlockSpec((1, 1, MOE_BLOCK), lambda b, be, na: (jnp.minimum(b + 1, n_blocks - 1), 0, 0),
                         memory_space=pltpu.SMEM),
            pl.BlockSpec(memory_space=pl.ANY),
            pl.BlockSpec((1, D_MODEL, ff2), lambda b, be, na: (be[b], 0, 0)),
            pl.BlockSpec((1, 1, ff2), lambda b, be, na: (be[b], 0, 0)),
            pl.BlockSpec((1, ff2 // 2, D_MODEL), lambda b, be, na: (be[b], 0, 0)),
            pl.BlockSpec((1, 1, D_MODEL), lambda b, be, na: (be[b], 0, 0)),
            pl.BlockSpec((MOE_BLOCK, 1), lambda b, be, na: (b, 0)),
        ],
        out_specs=pl.BlockSpec((MOE_BLOCK, ROW_CHUNKS, LANES), lambda b, be, na: (b, 0, 0)),
        scratch_shapes=[
            pltpu.VMEM((2 * MOE_BLOCK, ROW_CHUNKS, LANES), F32),
            pltpu.SemaphoreType.DMA((2,)),
        ],
    )
    return pl.pallas_call(
        _experts_kernel,
        grid_spec=grid_spec,
        out_shape=jax.ShapeDtypeStruct((n_blocks * MOE_BLOCK, ROW_CHUNKS, LANES), F32),
        compiler_params=pltpu.CompilerParams(dimension_semantics=("arbitrary",),
                                             vmem_limit_bytes=VMEM_LIMIT),
        name="experts",
    )(block_expert, n_active, row_token, row_token, h2_rows, wgu, bgu, wd, bd, row_gate)


def _combine_kernel(d_cur, d_nxt, x1_ref, ys_hbm, o_ref, buf, sem):
    i = pl.program_id(0)
    n = pl.num_programs(0)
    tc = x1_ref.shape[0]
    rows = TOP_K * tc
    slot = i % 2

    @pl.when(i == 0)
    def _():
        _issue_rows(d_cur, rows, ys_hbm, buf, sem.at[0], 0)

    @pl.when(i + 1 < n)
    def _():
        _issue_rows(d_nxt, rows, ys_hbm, buf, sem.at[1 - slot], (1 - slot) * rows)

    base = pl.multiple_of(slot * rows, rows)
    _wait_rows(rows, ys_hbm, buf, sem.at[slot], base)
    acc = x1_ref[...]
    for k in range(TOP_K):
        acc += jnp.concatenate([buf[pl.ds(base + k * tc, tc), c, :] for c in range(ROW_CHUNKS)], axis=1)
    o_ref[...] = acc


def _combine(dest_tiles, x1, ys_rows):
    t = x1.shape[0]
    tc = TC_COMB
    n = t // tc
    rows = TOP_K * tc
    return pl.pallas_call(
        _combine_kernel,
        grid=(n,),
        in_specs=[
            pl.BlockSpec((1, 1, rows), lambda i: (i, 0, 0), memory_space=pltpu.SMEM),
            pl.BlockSpec((1, 1, rows), lambda i: (jnp.minimum(i + 1, n - 1), 0, 0),
                         memory_space=pltpu.SMEM),
            pl.BlockSpec((tc, D_MODEL), lambda i: (i, 0)),
            pl.BlockSpec(memory_space=pl.ANY),
        ],
        out_specs=pl.BlockSpec((tc, D_MODEL), lambda i: (i, 0)),
        out_shape=jax.ShapeDtypeStruct((t, D_MODEL), F32),
        scratch_shapes=[
            pltpu.VMEM((2 * rows, ROW_CHUNKS, LANES), F32),
            pltpu.SemaphoreType.DMA((2,)),
        ],
        compiler_params=pltpu.CompilerParams(dimension_semantics=("arbitrary",),
                                             vmem_limit_bytes=VMEM_LIMIT),
        name="combine",
    )(dest_tiles, dest_tiles, x1, ys_rows)


def kernel(x, mem, g_mix, w_in, g_q_diff, g_k_diff, lambda_q1, lambda_k1, lambda_q2, lambda_k2,
           g_subln, conv_w, g_mem, w_mem_kv, g_q_mem, g_k_mem, w_branch, w_out, g_ffn, w_router,
           b_router, w_gate_up, b_gate_up, w_down, b_down):
    batch, seq, d = x.shape
    t = batch * seq
    depth = g_mix.shape[0]
    assert depth == 1 and d == D_MODEL and seq % TQ == 0 and t % TM_PROJ == 0

    pos = jnp.arange(seq, dtype=F32)
    inv_freq = 1.0 / (ROPE_THETA ** (jnp.arange(0, DIFF_HEAD_DIM, 2, dtype=F32) / DIFF_HEAD_DIM))
    ang = pos[:, None] * inv_freq[None, :]
    cos_t = jnp.tile(jnp.cos(ang), (1, 4))
    sin_t = jnp.tile(jnp.concatenate([-jnp.sin(ang), jnp.sin(ang)], axis=1), (1, 2))
    grp = jnp.arange(SEC) // DIFF_HEAD_DIM
    bd = jnp.where(grp[:, None] == grp[None, :], 1.0 / DIFF_HEAD_DIM, 0.0).astype(BF16)
    tok = jnp.arange(TM_POST)
    tri = (tok[:, None] < tok[None, :]).astype(BF16)

    x2d = x.reshape(t, d)
    l = 0
    lam_init = 0.8 - 0.6 * math.exp(-0.3 * l)
    km, vm = _mem_kv(mem.reshape(batch * N_MEM, d), g_mem[l][None, :], w_mem_kv[l].astype(BF16),
                     g_k_mem[l][None, :])
    qn, kn, vv, cb, u, mqn, sig = _in_proj(
        x2d, g_mix[l][None, :], w_in[l].astype(BF16), cos_t, sin_t, bd,
        jnp.tile(g_q_diff[l], SEC // DIFF_HEAD_DIM)[None, :],
        jnp.tile(g_k_diff[l], SEC // DIFF_HEAD_DIM)[None, :],
        g_q_mem[l][None, :], seq)
    lam_rows = jnp.stack([lambda_q1[l], lambda_k1[l], lambda_q2[l], lambda_k2[l]]).astype(F32)
    y_diff = _diff_attn(qn, kn, vv, lam_rows, g_subln[l][None, :], batch, seq, lam_init)
    x1, h2_rows, eidx, gate, rank, counts = _post(
        x2d, y_diff, cb, u, mqn, sig, km, vm, conv_w[l], w_branch[l].astype(BF16),
        w_out[l].astype(BF16), g_ffn[l][None, :], w_router[l].T, b_router[l][:, None], tri, seq)

    n_assign = t * TOP_K
    n_rows = -(-(n_assign + N_EXPERTS * (MOE_BLOCK - 1)) // MOE_BLOCK) * MOE_BLOCK
    n_blocks = n_rows // MOE_BLOCK
    cnt = counts[:, 0].astype(jnp.int32)
    padded = (cnt + MOE_BLOCK - 1) // MOE_BLOCK * MOE_BLOCK
    pend = jnp.cumsum(padded)
    pstart = pend - padded
    dest = pstart[eidx] + rank
    token_ids = jnp.broadcast_to(jnp.arange(t, dtype=jnp.int32)[None, :], dest.shape)
    row_token = jnp.zeros((n_rows,), jnp.int32).at[dest.reshape(-1)].set(token_ids.reshape(-1))
    row_gate = jnp.zeros((n_rows,), F32).at[dest.reshape(-1)].set(gate.reshape(-1))
    block_row0 = jnp.arange(n_blocks, dtype=jnp.int32) * MOE_BLOCK
    block_expert = jnp.minimum(
        jnp.sum((pend[None, :] <= block_row0[:, None]).astype(jnp.int32), axis=1), N_EXPERTS - 1)
    n_active = (pend[-1:] // MOE_BLOCK).astype(jnp.int32)

    ys_rows = _experts(block_expert, n_active, row_token.reshape(n_blocks, 1, MOE_BLOCK), h2_rows,
                       w_gate_up[l].astype(BF16), b_gate_up[l][:, None, :],
                       w_down[l].astype(BF16), b_down[l][:, None, :], row_gate[:, None])
    dest_tiles = dest.reshape(TOP_K, t // TC_COMB, TC_COMB).transpose(1, 0, 2).reshape(
        t // TC_COMB, 1, TOP_K * TC_COMB)
    out = _combine(dest_tiles, x1, ys_rows)
    return out.reshape(batch, seq, d)
```

```python
import functools
import math

import jax
import jax.numpy as jnp
from jax import lax
from jax.experimental import pallas as pl
from jax.experimental.pallas import tpu as pltpu

F32 = jnp.float32
BF16 = jnp.bfloat16

D_MODEL = 1024
CHUNK = 64
EPS = 1e-6
ROPE_THETA = 10000.0
DIFF_HEADS = 4
DIFF_HEAD_DIM = 64
DIFF_V_DIM = 2 * DIFF_HEAD_DIM
N_MEM = 256
MEM_HEADS = 4
MEM_HEAD_DIM = 128
SEC = 512
N_SEC = 13
N_EXPERTS = 32
TOP_K = 4
SWIGLU_LIMIT = 7.0
SWIGLU_ALPHA = 1.702
MOE_BLOCK = 256
LANES = 128
ROW_CHUNKS = D_MODEL // LANES
LOG2E = 1.4426950408889634
NEG_BIG = -1e30

TM_PROJ = 256
TQ = 512
ATTN_STRIP = 256
TM_POST = 256
TC_COMB = 128
VMEM_LIMIT = 48 * 1024 * 1024


def _rms(x, eps=EPS):
    return x * lax.rsqrt(jnp.mean(x * x, axis=-1, keepdims=True) + eps)


def _mem_kv_kernel(mem_ref, g_ref, w_ref, gk_ref, k_out, v_out):
    h = (_rms(mem_ref[...]) * g_ref[...]).astype(BF16)
    kv = jnp.dot(h, w_ref[...], preferred_element_type=F32)
    for hd in range(MEM_HEADS):
        sl = slice(hd * MEM_HEAD_DIM, (hd + 1) * MEM_HEAD_DIM)
        k_out[:, sl] = (_rms(kv[:, sl]) * gk_ref[...]).astype(BF16)
    v_out[...] = kv[:, SEC:].astype(BF16)


def _mem_kv(mem2d, g_mem, w_kv, g_k):
    rows = mem2d.shape[0]
    return pl.pallas_call(
        _mem_kv_kernel,
        grid=(rows // N_MEM,),
        in_specs=[
            pl.BlockSpec((N_MEM, D_MODEL), lambda i: (i, 0)),
            pl.BlockSpec((1, D_MODEL), lambda i: (0, 0)),
            pl.BlockSpec((D_MODEL, 2 * SEC), lambda i: (0, 0)),
            pl.BlockSpec((1, MEM_HEAD_DIM), lambda i: (0, 0)),
        ],
        out_specs=[pl.BlockSpec((N_MEM, SEC), lambda i: (i, 0))] * 2,
        out_shape=[jax.ShapeDtypeStruct((rows, SEC), BF16)] * 2,
        compiler_params=pltpu.CompilerParams(vmem_limit_bytes=VMEM_LIMIT),
        name="mem_kv",
    )(mem2d, g_mem, w_kv, g_k)


def _in_proj_kernel(x_ref, g_ref, w_ref, cos_ref, sin_ref, bd_ref, gq_ref, gk_ref, gqm_ref,
                    q_out, k_out, v_out, cb_out, u_out, mq_out, sig_out):
    tm = x_ref.shape[0]
    h = (_rms(x_ref[...]) * g_ref[...]).astype(BF16)

    def proj(sec):
        return jnp.dot(h, w_ref[:, sec * SEC:(sec + 1) * SEC], preferred_element_type=F32)

    cos = jnp.concatenate([cos_ref[...]] * (SEC // LANES), axis=1)
    sin = jnp.concatenate([sin_ref[...]] * (SEC // LANES), axis=1)
    lane = lax.broadcasted_iota(jnp.int32, (tm, SEC), 1)
    first_half = (lane & (DIFF_HEAD_DIM // 2)) == 0

    def norm_rope(a, g, scale):
        ms = jnp.dot((a * a).astype(BF16), bd_ref[...], preferred_element_type=F32)
        y = a * lax.rsqrt(ms + EPS) * g
        partner = jnp.where(first_half,
                            pltpu.roll(y, SEC - DIFF_HEAD_DIM // 2, 1),
                            pltpu.roll(y, DIFF_HEAD_DIM // 2, 1))
        return (y * cos + partner * sin) * scale

    q_out[...] = norm_rope(proj(0), gq_ref[...], DIFF_HEAD_DIM ** -0.5 * LOG2E).astype(BF16)
    k_out[...] = norm_rope(proj(1), gk_ref[...], 1.0).astype(BF16)
    v_out[...] = proj(2).astype(BF16)
    cb_out[...] = proj(3).astype(BF16)
    u_out[...] = proj(4) * proj(5)
    mq = proj(6)
    for hd in range(MEM_HEADS):
        sl = slice(hd * MEM_HEAD_DIM, (hd + 1) * MEM_HEAD_DIM)
        mq_out[:, sl] = (_rms(mq[:, sl]) * gqm_ref[...]
                         * (MEM_HEAD_DIM ** -0.5 * LOG2E)).astype(BF16)
    for s in range(7, N_SEC):
        a = proj(s)
        sig_out[:, (s - 7) * SEC:(s - 6) * SEC] = (1.0 / (1.0 + jnp.exp(-a))).astype(BF16)


def _in_proj(x2d, g_mix, w_in, cos_t, sin_t, bd, gq, gk, gqm, seq):
    t = x2d.shape[0]
    tm = TM_PROJ
    tiles_per_seq = seq // tm
    row = lambda i: (i, 0)
    const = lambda i: (0, 0)
    outs = [
        jax.ShapeDtypeStruct((t, SEC), BF16),
        jax.ShapeDtypeStruct((t, SEC), BF16),
        jax.ShapeDtypeStruct((t, SEC), BF16),
        jax.ShapeDtypeStruct((t, SEC), BF16),
        jax.ShapeDtypeStruct((t, SEC), F32),
        jax.ShapeDtypeStruct((t, SEC), BF16),
        jax.ShapeDtypeStruct((t, 3 * D_MODEL), BF16),
    ]
    return pl.pallas_call(
        _in_proj_kernel,
        grid=(t // tm,),
        in_specs=[
            pl.BlockSpec((tm, D_MODEL), row),
            pl.BlockSpec((1, D_MODEL), const),
            pl.BlockSpec((D_MODEL, N_SEC * SEC), const),
            pl.BlockSpec((tm, LANES), lambda i: (i % tiles_per_seq, 0)),
            pl.BlockSpec((tm, LANES), lambda i: (i % tiles_per_seq, 0)),
            pl.BlockSpec((SEC, SEC), const),
            pl.BlockSpec((1, SEC), const),
            pl.BlockSpec((1, SEC), const),
            pl.BlockSpec((1, MEM_HEAD_DIM), const),
        ],
        out_specs=[pl.BlockSpec((tm, o.shape[1]), row) for o in outs],
        out_shape=outs,
        compiler_params=pltpu.CompilerParams(vmem_limit_bytes=VMEM_LIMIT),
        name="in_proj",
    )(x2d, g_mix, w_in, cos_t, sin_t, bd, gq, gk, gqm)


def _diff_attn_kernel(q_ref, k_ref, vt_ref, lam_ref, gs_ref, o_ref, acc_sc, s0_sc, *, lam_init):
    i = pl.program_id(2)
    tq = q_ref.shape[0]
    q = q_ref[...]
    lane = lax.broadcasted_iota(jnp.int32, q.shape, 1)
    zero = jnp.zeros_like(q)
    qq = jnp.concatenate([jnp.where(lane < DIFF_HEAD_DIM, q, zero),
                          jnp.where(lane >= DIFF_HEAD_DIM, q, zero)], axis=0)
    acc_sc[...] = jnp.zeros(acc_sc.shape, F32)
    n_strips = 2 * tq // ATTN_STRIP

    def scores(j, c, nk):
        off = pl.multiple_of(j * tq, tq)
        return lax.dot_general(k_ref[pl.ds(off, nk), :], qq[c * ATTN_STRIP:(c + 1) * ATTN_STRIP, :],
                               (((1,), (1,)), ((), ())), preferred_element_type=F32)

    def step(j, ms, ls, masked):
        q_offs = [(c * ATTN_STRIP) % tq for c in range(n_strips)]
        nks = [min(tq, qo + ATTN_STRIP) if masked else tq for qo in q_offs]
        ms_new, ls_new = [], []
        s = s0_sc[0:nks[0], :]
        for c in range(n_strips):
            cols = slice(c * ATTN_STRIP, (c + 1) * ATTN_STRIP)
            if c + 1 < n_strips:
                s_next = scores(j, c + 1, nks[c + 1])
            elif not masked:
                s0_sc[...] = scores(j + 1, 0, tq)
            if masked:
                r = lax.broadcasted_iota(jnp.int32, s.shape, 0)
                col = lax.broadcasted_iota(jnp.int32, s.shape, 1)
                s = jnp.where((r // CHUNK) <= ((col + q_offs[c]) // CHUNK), s, NEG_BIG)
            m_new = jnp.maximum(ms[c], jnp.max(s, axis=0, keepdims=True))
            p = jnp.exp2(s - m_new)
            alpha = jnp.exp2(ms[c] - m_new)
            ls_new.append(alpha * ls[c] + jnp.sum(p, axis=0, keepdims=True))
            ms_new.append(m_new)
            acc_sc[:, cols] = alpha * acc_sc[:, cols] + jnp.dot(
                vt_ref[0, 0, j][:, :nks[c]], p.astype(BF16), preferred_element_type=F32)
            if c + 1 < n_strips:
                s = s_next
        return tuple(ms_new), tuple(ls_new)

    s0_sc[...] = scores(0, 0, tq)
    m0 = tuple(jnp.full((1, ATTN_STRIP), NEG_BIG, F32) for _ in range(n_strips))
    l0 = tuple(jnp.zeros((1, ATTN_STRIP), F32) for _ in range(n_strips))
    m, l = lax.fori_loop(0, i, lambda j, c: step(j, c[0], c[1], False), (m0, l0))
    m, l = step(i, m, l, True)

    ot = acc_sc[...] / jnp.concatenate(l, axis=1)
    lam = (jnp.exp(jnp.sum(lam_ref[0:1, :] * lam_ref[1:2, :], axis=-1, keepdims=True))
           - jnp.exp(jnp.sum(lam_ref[2:3, :] * lam_ref[3:4, :], axis=-1, keepdims=True))
           + lam_init)
    d = (ot[:, :tq] - lam * ot[:, tq:]).T
    o_ref[...] = (_rms(d) * gs_ref[...] * (1.0 - lam_init)).astype(BF16)


def _diff_attn(qn, kn, vt, lam_rows, g_subln, batch, seq, lam_init):
    t = qn.shape[0]
    nq = seq // TQ
    return pl.pallas_call(
        functools.partial(_diff_attn_kernel, lam_init=lam_init),
        grid=(batch, DIFF_HEADS, nq),
        in_specs=[
            pl.BlockSpec((TQ, DIFF_V_DIM), lambda b, h, i: (b * nq + i, h)),
            pl.BlockSpec((seq, DIFF_V_DIM), lambda b, h, i: (b, h)),
            pl.BlockSpec((1, 1, nq, DIFF_V_DIM, TQ), lambda b, h, i: (b, h, 0, 0, 0)),
            pl.BlockSpec((4, DIFF_HEAD_DIM), lambda b, h, i: (0, 0)),
            pl.BlockSpec((1, DIFF_V_DIM), lambda b, h, i: (0, 0)),
        ],
        out_specs=pl.BlockSpec((TQ, DIFF_V_DIM), lambda b, h, i: (b * nq + i, h)),
        out_shape=jax.ShapeDtypeStruct((t, SEC), BF16),
        scratch_shapes=[pltpu.VMEM((DIFF_V_DIM, 2 * TQ), F32), pltpu.VMEM((TQ, ATTN_STRIP), F32)],
        compiler_params=pltpu.CompilerParams(vmem_limit_bytes=VMEM_LIMIT),
        name="diff_attn",
    )(qn, kn, vt, lam_rows, g_subln)


def _post_kernel(x_ref, yd_ref, cb_ref, u_ref, up_ref, mq_ref, sig_ref, km_ref, vm_ref,
                 cw_ref, wb_ref, wo_ref, gf_ref, wr_ref, br_ref, tri_ref,
                 x1_out, h2_out, eidx_out, gate_out, rank_out, cnt_out, carry_sc, *, seq):
    i = pl.program_id(0)
    tm = x_ref.shape[0]

    @pl.when(i == 0)
    def _():
        carry_sc[...] = jnp.zeros(carry_sc.shape, F32)

    u = u_ref[...]
    seq_start = (i * tm) % seq == 0
    up = jnp.where(seq_start, 0.0, up_ref[...])
    r = lax.broadcasted_iota(jnp.int32, u.shape, 0)
    u1 = jnp.where(r == 0, up[7:8, :], pltpu.roll(u, 1, 0))
    u2 = jnp.where(r == 0, up[6:7, :], jnp.where(r == 1, up[7:8, :], pltpu.roll(u, 2, 0)))
    y_conv = cb_ref[...].astype(F32) * (cw_ref[0:1, :] * u2 + cw_ref[1:2, :] * u1 + cw_ref[2:3, :] * u)

    y_mem = []
    for hd in range(MEM_HEADS):
        sl = slice(hd * MEM_HEAD_DIM, (hd + 1) * MEM_HEAD_DIM)
        s = lax.dot_general(mq_ref[:, sl], km_ref[:, sl], (((1,), (1,)), ((), ())),
                            preferred_element_type=F32)
        p = jnp.exp2(s - jnp.max(s, axis=-1, keepdims=True))
        o = jnp.dot(p.astype(BF16), vm_ref[:, sl], preferred_element_type=F32)
        y_mem.append(o / jnp.sum(p, axis=-1, keepdims=True))

    merged = (sig_ref[:, 0:D_MODEL].astype(F32)
              * jnp.dot(yd_ref[...], wb_ref[0], preferred_element_type=F32))
    merged += (sig_ref[:, D_MODEL:2 * D_MODEL].astype(F32)
               * jnp.dot(y_conv.astype(BF16), wb_ref[1], preferred_element_type=F32))
    merged += (sig_ref[:, 2 * D_MODEL:3 * D_MODEL].astype(F32)
               * jnp.dot(jnp.concatenate(y_mem, axis=1).astype(BF16), wb_ref[2],
                         preferred_element_type=F32))
    x1 = x_ref[...] + jnp.dot(merged.astype(BF16), wo_ref[...], preferred_element_type=F32)
    x1_out[...] = x1

    h2 = _rms(x1) * gf_ref[...]
    for c in range(ROW_CHUNKS):
        h2_out[:, c, :] = h2[:, c * LANES:(c + 1) * LANES]

    logits = lax.dot_general(wr_ref[...], h2, (((1,), (1,)), ((), ())),
                             precision=lax.Precision.HIGHEST,
                             preferred_element_type=F32) + br_ref[...]
    eio = lax.broadcasted_iota(jnp.int32, logits.shape, 0)
    work = logits
    vals, hots = [], []
    for k in range(TOP_K):
        mk = jnp.max(work, axis=0, keepdims=True)
        ik = jnp.min(jnp.where(work == mk, eio, N_EXPERTS), axis=0, keepdims=True)
        hot = eio == ik
        work = jnp.where(hot, -jnp.inf, work)
        vals.append(mk)
        hots.append(hot)
        eidx_out[k:k + 1, :] = ik
    ex = [jnp.exp(v - vals[0]) for v in vals]
    den = ex[0] + ex[1] + ex[2] + ex[3]
    for k in range(TOP_K):
        gate_out[k:k + 1, :] = ex[k] / den

    assign = jnp.zeros(logits.shape, F32)
    for hot in hots:
        assign = jnp.where(hot, 1.0, assign)
    before = jnp.dot(assign.astype(BF16), tri_ref[...], preferred_element_type=F32) + carry_sc[...]
    for k in range(TOP_K):
        rank_out[k:k + 1, :] = jnp.sum(jnp.where(hots[k], before, 0.0), axis=0,
                                       keepdims=True).astype(jnp.int32)
    carry_sc[...] = carry_sc[...] + jnp.sum(assign, axis=1, keepdims=True)
    cnt_out[...] = jnp.broadcast_to(carry_sc[...], cnt_out.shape)


def _post(x2d, y_diff, cb, u, mqn, sig, km, vm, conv_w, w_branch, w_out, g_ffn, w_rt, b_r, tri, seq):
    t = x2d.shape[0]
    tm = TM_POST
    row = lambda i: (i, 0)
    const = lambda i: (0, 0)
    outs = [
        jax.ShapeDtypeStruct((t, D_MODEL), F32),
        jax.ShapeDtypeStruct((t, ROW_CHUNKS, LANES), F32),
        jax.ShapeDtypeStruct((TOP_K, t), jnp.int32),
        jax.ShapeDtypeStruct((TOP_K, t), F32),
        jax.ShapeDtypeStruct((TOP_K, t), jnp.int32),
        jax.ShapeDtypeStruct((N_EXPERTS, LANES), F32),
    ]
    return pl.pallas_call(
        functools.partial(_post_kernel, seq=seq),
        grid=(t // tm,),
        in_specs=[
            pl.BlockSpec((tm, D_MODEL), row),
            pl.BlockSpec((tm, SEC), row),
            pl.BlockSpec((tm, SEC), row),
            pl.BlockSpec((tm, SEC), row),
            pl.BlockSpec((8, SEC), lambda i: (jnp.maximum(i * (tm // 8) - 1, 0), 0)),
            pl.BlockSpec((tm, SEC), row),
            pl.BlockSpec((tm, 3 * D_MODEL), row),
            pl.BlockSpec((N_MEM, SEC), lambda i: ((i * tm) // seq, 0)),
            pl.BlockSpec((N_MEM, SEC), lambda i: ((i * tm) // seq, 0)),
            pl.BlockSpec((3, SEC), const),
            pl.BlockSpec((3, SEC, D_MODEL), lambda i: (0, 0, 0)),
            pl.BlockSpec((D_MODEL, D_MODEL), const),
            pl.BlockSpec((1, D_MODEL), const),
            pl.BlockSpec((N_EXPERTS, D_MODEL), const),
            pl.BlockSpec((N_EXPERTS, 1), const),
            pl.BlockSpec((tm, tm), const),
        ],
        out_specs=[
            pl.BlockSpec((tm, D_MODEL), row),
            pl.BlockSpec((tm, ROW_CHUNKS, LANES), lambda i: (i, 0, 0)),
            pl.BlockSpec((TOP_K, tm), lambda i: (0, i)),
            pl.BlockSpec((TOP_K, tm), lambda i: (0, i)),
            pl.BlockSpec((TOP_K, tm), lambda i: (0, i)),
            pl.BlockSpec((N_EXPERTS, LANES), const),
        ],
        out_shape=outs,
        scratch_shapes=[pltpu.VMEM((N_EXPERTS, 1), F32)],
        compiler_params=pltpu.CompilerParams(dimension_semantics=("arbitrary",),
                                             vmem_limit_bytes=VMEM_LIMIT),
        name="post",
    )(x2d, y_diff, cb, u, u, mqn, sig, km, vm, conv_w, w_branch, w_out, g_ffn, w_rt, b_r, tri)


def _row_gather_copy(src_hbm, buf, sem, idx, dst_row):
    return pltpu.make_async_copy(src_hbm.at[idx], buf.at[dst_row], sem)


def _issue_rows(idx_ref, n, src_hbm, buf, sem, base):
    def body(r, carry):
        _row_gather_copy(src_hbm, buf, sem, idx_ref[0, 0, r], base + r).start()
        return carry
    lax.fori_loop(0, n, body, 0)


def _wait_rows(n, src_hbm, buf, sem, base):
    pltpu.make_async_copy(src_hbm.at[pl.ds(0, n)], buf.at[pl.ds(base, n)], sem).wait()


def _experts_kernel(be_ref, na_ref, rt_cur, rt_nxt, h2_hbm, wgu_ref, bgu_ref, wd_ref, bd_ref, rg_ref,
                    ys_out, xbuf, sem):
    b = pl.program_id(0)
    n_active = na_ref[0]
    slot = b % 2

    @pl.when(b == 0)
    def _():
        _issue_rows(rt_cur, MOE_BLOCK, h2_hbm, xbuf, sem.at[0], 0)

    @pl.when(b + 1 < n_active)
    def _():
        _issue_rows(rt_nxt, MOE_BLOCK, h2_hbm, xbuf, sem.at[1 - slot], (1 - slot) * MOE_BLOCK)

    @pl.when(b < n_active)
    def _():
        base = pl.multiple_of(slot * MOE_BLOCK, MOE_BLOCK)
        _wait_rows(MOE_BLOCK, h2_hbm, xbuf, sem.at[slot], base)
        x = jnp.concatenate([xbuf[pl.ds(base, MOE_BLOCK), c, :] for c in range(ROW_CHUNKS)],
                            axis=1).astype(BF16)
        gu = jnp.dot(x, wgu_ref[0], preferred_element_type=F32) + bgu_ref[0]
        g = jnp.minimum(gu[:, :D_MODEL], SWIGLU_LIMIT)
        u = jnp.clip(gu[:, D_MODEL:], -SWIGLU_LIMIT, SWIGLU_LIMIT)
        act = (u + 1.0) * (g * (1.0 / (1.0 + jnp.exp(-SWIGLU_ALPHA * g))))
        y = (jnp.dot(act.astype(BF16), wd_ref[0], preferred_element_type=F32) + bd_ref[0]) * rg_ref[...]
        for c in range(ROW_CHUNKS):
            ys_out[:, c, :] = y[:, c * LANES:(c + 1) * LANES]

    @pl.when(b >= n_active)
    def _():
        ys_out[...] = jnp.zeros(ys_out.shape, F32)


def _experts(block_expert, n_active, row_token, h2_rows, wgu, bgu, wd, bd, row_gate):
    n_blocks = block_expert.shape[0]
    ff2 = wgu.shape[2]
    grid_spec = pltpu.PrefetchScalarGridSpec(
        num_scalar_prefetch=2,
        grid=(n_blocks,),
        in_specs=[
            pl.BlockSpec((1, 1, MOE_BLOCK), lambda b, be, na: (b, 0, 0), memory_space=pltpu.SMEM),
            pl.BlockSpec((1, 1, MOE_BLOCK), lambda b, be, na: (jnp.minimum(b + 1, n_blocks - 1), 0, 0),
                         memory_space=pltpu.SMEM),
            pl.BlockSpec(memory_space=pl.ANY),
            pl.BlockSpec((1, D_MODEL, ff2), lambda b, be, na: (be[b], 0, 0)),
            pl.BlockSpec((1, 1, ff2), lambda b, be, na: (be[b], 0, 0)),
            pl.BlockSpec((1, ff2 // 2, D_MODEL), lambda b, be, na: (be[b], 0, 0)),
            pl.BlockSpec((1, 1, D_MODEL), lambda b, be, na: (be[b], 0, 0)),
            pl.BlockSpec((MOE_BLOCK, 1), lambda b, be, na: (b, 0)),
        ],
        out_specs=pl.BlockSpec((MOE_BLOCK, ROW_CHUNKS, LANES), lambda b, be, na: (b, 0, 0)),
        scratch_shapes=[
            pltpu.VMEM((2 * MOE_BLOCK, ROW_CHUNKS, LANES), F32),
            pltpu.SemaphoreType.DMA((2,)),
        ],
    )
    return pl.pallas_call(
        _experts_kernel,
        grid_spec=grid_spec,
        out_shape=jax.ShapeDtypeStruct((n_blocks * MOE_BLOCK, ROW_CHUNKS, LANES), F32),
        compiler_params=pltpu.CompilerParams(dimension_semantics=("arbitrary",),
                                             vmem_limit_bytes=VMEM_LIMIT),
        name="experts",
    )(block_expert, n_active, row_token, row_token, h2_rows, wgu, bgu, wd, bd, row_gate)


def _combine_kernel(d_cur, d_nxt, x1_ref, ys_hbm, o_ref, buf, sem):
    i = pl.program_id(0)
    n = pl.num_programs(0)
    tc = x1_ref.shape[0]
    rows = TOP_K * tc
    slot = i % 2

    @pl.when(i == 0)
    def _():
        _issue_rows(d_cur, rows, ys_hbm, buf, sem.at[0], 0)

    @pl.when(i + 1 < n)
    def _():
        _issue_rows(d_nxt, rows, ys_hbm, buf, sem.at[1 - slot], (1 - slot) * rows)

    base = pl.multiple_of(slot * rows, rows)
    _wait_rows(rows, ys_hbm, buf, sem.at[slot], base)
    acc = x1_ref[...]
    for k in range(TOP_K):
        acc += jnp.concatenate([buf[pl.ds(base + k * tc, tc), c, :] for c in range(ROW_CHUNKS)], axis=1)
    o_ref[...] = acc


def _combine(dest_tiles, x1, ys_rows):
    t = x1.shape[0]
    tc = TC_COMB
    n = t // tc
    rows = TOP_K * tc
    return pl.pallas_call(
        _combine_kernel,
        grid=(n,),
        in_specs=[
            pl.BlockSpec((1, 1, rows), lambda i: (i, 0, 0), memory_space=pltpu.SMEM),
            pl.BlockSpec((1, 1, rows), lambda i: (jnp.minimum(i + 1, n - 1), 0, 0),
                         memory_space=pltpu.SMEM),
            pl.BlockSpec((tc, D_MODEL), lambda i: (i, 0)),
            pl.BlockSpec(memory_space=pl.ANY),
        ],
        out_specs=pl.BlockSpec((tc, D_MODEL), lambda i: (i, 0)),
        out_shape=jax.ShapeDtypeStruct((t, D_MODEL), F32),
        scratch_shapes=[
            pltpu.VMEM((2 * rows, ROW_CHUNKS, LANES), F32),
            pltpu.SemaphoreType.DMA((2,)),
        ],
        compiler_params=pltpu.CompilerParams(dimension_semantics=("arbitrary",),
                                             vmem_limit_bytes=VMEM_LIMIT),
        name="combine",
    )(dest_tiles, dest_tiles, x1, ys_rows)


def kernel(x, mem, g_mix, w_in, g_q_diff, g_k_diff, lambda_q1, lambda_k1, lambda_q2, lambda_k2,
           g_subln, conv_w, g_mem, w_mem_kv, g_q_mem, g_k_mem, w_branch, w_out, g_ffn, w_router,
           b_router, w_gate_up, b_gate_up, w_down, b_down):
    batch, seq, d = x.shape
    t = batch * seq
    depth = g_mix.shape[0]
    assert depth == 1 and d == D_MODEL and seq % TQ == 0 and t % TM_PROJ == 0

    pos = jnp.arange(seq, dtype=F32)
    inv_freq = 1.0 / (ROPE_THETA ** (jnp.arange(0, DIFF_HEAD_DIM, 2, dtype=F32) / DIFF_HEAD_DIM))
    ang = pos[:, None] * inv_freq[None, :]
    cos_t = jnp.tile(jnp.cos(ang), (1, 4))
    sin_t = jnp.tile(jnp.concatenate([-jnp.sin(ang), jnp.sin(ang)], axis=1), (1, 2))
    grp = jnp.arange(SEC) // DIFF_HEAD_DIM
    bd = jnp.where(grp[:, None] == grp[None, :], 1.0 / DIFF_HEAD_DIM, 0.0).astype(BF16)
    tok = jnp.arange(TM_POST)
    tri = (tok[:, None] < tok[None, :]).astype(BF16)

    x2d = x.reshape(t, d)
    l = 0
    lam_init = 0.8 - 0.6 * math.exp(-0.3 * l)
    km, vm = _mem_kv(mem.reshape(batch * N_MEM, d), g_mem[l][None, :], w_mem_kv[l].astype(BF16),
                     g_k_mem[l][None, :])
    qn, kn, vv, cb, u, mqn, sig = _in_proj(
        x2d, g_mix[l][None, :], w_in[l].astype(BF16), cos_t, sin_t, bd,
        jnp.tile(g_q_diff[l], SEC // DIFF_HEAD_DIM)[None, :],
        jnp.tile(g_k_diff[l], SEC // DIFF_HEAD_DIM)[None, :],
        g_q_mem[l][None, :], seq)
    lam_rows = jnp.stack([lambda_q1[l], lambda_k1[l], lambda_q2[l], lambda_k2[l]]).astype(F32)
    vt = vv.reshape(batch, seq // TQ, TQ, DIFF_HEADS, DIFF_V_DIM).transpose(0, 3, 1, 4, 2)
    y_diff = _diff_attn(qn, kn, vt, lam_rows, g_subln[l][None, :], batch, seq, lam_init)
    x1, h2_rows, eidx, gate, rank, counts = _post(
        x2d, y_diff, cb, u, mqn, sig, km, vm, conv_w[l], w_branch[l].astype(BF16),
        w_out[l].astype(BF16), g_ffn[l][None, :], w_router[l].T, b_router[l][:, None], tri, seq)

    n_assign = t * TOP_K
    n_rows = -(-(n_assign + N_EXPERTS * (MOE_BLOCK - 1)) // MOE_BLOCK) * MOE_BLOCK
    n_blocks = n_rows // MOE_BLOCK
    cnt = counts[:, 0].astype(jnp.int32)
    padded = (cnt + MOE_BLOCK - 1) // MOE_BLOCK * MOE_BLOCK
    pend = jnp.cumsum(padded)
    pstart = pend - padded
    dest = pstart[eidx] + rank
    token_ids = jnp.broadcast_to(jnp.arange(t, dtype=jnp.int32)[None, :], dest.shape)
    row_token = jnp.zeros((n_rows,), jnp.int32).at[dest.reshape(-1)].set(token_ids.reshape(-1))
    row_gate = jnp.zeros((n_rows,), F32).at[dest.reshape(-1)].set(gate.reshape(-1))
    block_row0 = jnp.arange(n_blocks, dtype=jnp.int32) * MOE_BLOCK
    block_expert = jnp.minimum(
        jnp.sum((pend[None, :] <= block_row0[:, None]).astype(jnp.int32), axis=1), N_EXPERTS - 1)
    n_active = (pend[-1:] // MOE_BLOCK).astype(jnp.int32)

    ys_rows = _experts(block_expert, n_active, row_token.reshape(n_blocks, 1, MOE_BLOCK), h2_rows,
                       w_gate_up[l].astype(BF16), b_gate_up[l][:, None, :],
                       w_down[l].astype(BF16), b_down[l][:, None, :], row_gate[:, None])
    dest_tiles = dest.reshape(TOP_K, t // TC_COMB, TC_COMB).transpose(1, 0, 2).reshape(
        t // TC_COMB, 1, TOP_K * TC_COMB)
    out = _combine(dest_tiles, x1, ys_rows)
    return out.reshape(batch, seq, d)
```

```python
import functools
import math

import jax
import jax.numpy as jnp
from jax import lax
from jax.experimental import pallas as pl
from jax.experimental.pallas import tpu as pltpu

F32 = jnp.float32
BF16 = jnp.bfloat16

D_MODEL = 1024
CHUNK = 64
EPS = 1e-6
ROPE_THETA = 10000.0
DIFF_HEADS = 4
DIFF_HEAD_DIM = 64
DIFF_V_DIM = 2 * DIFF_HEAD_DIM
N_MEM = 256
MEM_HEADS = 4
MEM_HEAD_DIM = 128
SEC = 512
N_SEC = 13
N_EXPERTS = 32
TOP_K = 4
SWIGLU_LIMIT = 7.0
SWIGLU_ALPHA = 1.702
MOE_BLOCK = 256
LANES = 128
ROW_CHUNKS = D_MODEL // LANES
LOG2E = 1.4426950408889634
NEG_BIG = -1e30

TM_PROJ = 256
TQ = 512
ATTN_STRIP = 256
TM_POST = 256
MAX_RUN = max(MOE_BLOCK, TM_POST)
VMEM_LIMIT = 48 * 1024 * 1024


def _rms(x, eps=EPS):
    return x * lax.rsqrt(jnp.mean(x * x, axis=-1, keepdims=True) + eps)


def _mem_kv_kernel(mem_ref, g_ref, w_ref, gk_ref, k_out, v_out):
    h = (_rms(mem_ref[...]) * g_ref[...]).astype(BF16)
    kv = jnp.dot(h, w_ref[...], preferred_element_type=F32)
    for hd in range(MEM_HEADS):
        sl = slice(hd * MEM_HEAD_DIM, (hd + 1) * MEM_HEAD_DIM)
        k_out[:, sl] = (_rms(kv[:, sl]) * gk_ref[...]).astype(BF16)
    v_out[...] = kv[:, SEC:].astype(BF16)


def _mem_kv(mem2d, g_mem, w_kv, g_k):
    rows = mem2d.shape[0]
    return pl.pallas_call(
        _mem_kv_kernel,
        grid=(rows // N_MEM,),
        in_specs=[
            pl.BlockSpec((N_MEM, D_MODEL), lambda i: (i, 0)),
            pl.BlockSpec((1, D_MODEL), lambda i: (0, 0)),
            pl.BlockSpec((D_MODEL, 2 * SEC), lambda i: (0, 0)),
            pl.BlockSpec((1, MEM_HEAD_DIM), lambda i: (0, 0)),
        ],
        out_specs=[pl.BlockSpec((N_MEM, SEC), lambda i: (i, 0))] * 2,
        out_shape=[jax.ShapeDtypeStruct((rows, SEC), BF16)] * 2,
        compiler_params=pltpu.CompilerParams(vmem_limit_bytes=VMEM_LIMIT),
        name="mem_kv",
    )(mem2d, g_mem, w_kv, g_k)


def _in_proj_kernel(x_ref, g_ref, w_ref, cos_ref, sin_ref, bd_ref, gq_ref, gk_ref, gqm_ref,
                    q_out, k_out, v_out, cb_out, u_out, mq_out, sig_out):
    tm = x_ref.shape[0]
    h = (_rms(x_ref[...]) * g_ref[...]).astype(BF16)

    def proj(sec):
        return jnp.dot(h, w_ref[:, sec * SEC:(sec + 1) * SEC], preferred_element_type=F32)

    cos = jnp.concatenate([cos_ref[...]] * (SEC // LANES), axis=1)
    sin = jnp.concatenate([sin_ref[...]] * (SEC // LANES), axis=1)
    lane = lax.broadcasted_iota(jnp.int32, (tm, SEC), 1)
    first_half = (lane & (DIFF_HEAD_DIM // 2)) == 0

    def norm_rope(a, g, scale):
        ms = jnp.dot((a * a).astype(BF16), bd_ref[...], preferred_element_type=F32)
        y = a * lax.rsqrt(ms + EPS) * g
        partner = jnp.where(first_half,
                            pltpu.roll(y, SEC - DIFF_HEAD_DIM // 2, 1),
                            pltpu.roll(y, DIFF_HEAD_DIM // 2, 1))
        return (y * cos + partner * sin) * scale

    q_out[...] = norm_rope(proj(0), gq_ref[...], DIFF_HEAD_DIM ** -0.5 * LOG2E).astype(BF16)
    k_out[...] = norm_rope(proj(1), gk_ref[...], 1.0).astype(BF16)
    v_out[...] = proj(2).astype(BF16)
    cb_out[...] = proj(3).astype(BF16)
    u_out[...] = proj(4) * proj(5)
    mq = proj(6)
    for hd in range(MEM_HEADS):
        sl = slice(hd * MEM_HEAD_DIM, (hd + 1) * MEM_HEAD_DIM)
        mq_out[:, sl] = (_rms(mq[:, sl]) * gqm_ref[...]
                         * (MEM_HEAD_DIM ** -0.5 * LOG2E)).astype(BF16)
    for s in range(7, N_SEC):
        a = proj(s)
        sig_out[:, (s - 7) * SEC:(s - 6) * SEC] = (1.0 / (1.0 + jnp.exp(-a))).astype(BF16)


def _in_proj(x2d, g_mix, w_in, cos_t, sin_t, bd, gq, gk, gqm, seq):
    t = x2d.shape[0]
    tm = TM_PROJ
    tiles_per_seq = seq // tm
    row = lambda i: (i, 0)
    const = lambda i: (0, 0)
    outs = [
        jax.ShapeDtypeStruct((t, SEC), BF16),
        jax.ShapeDtypeStruct((t, SEC), BF16),
        jax.ShapeDtypeStruct((t, SEC), BF16),
        jax.ShapeDtypeStruct((t, SEC), BF16),
        jax.ShapeDtypeStruct((t, SEC), F32),
        jax.ShapeDtypeStruct((t, SEC), BF16),
        jax.ShapeDtypeStruct((t, 3 * D_MODEL), BF16),
    ]
    return pl.pallas_call(
        _in_proj_kernel,
        grid=(t // tm,),
        in_specs=[
            pl.BlockSpec((tm, D_MODEL), row),
            pl.BlockSpec((1, D_MODEL), const),
            pl.BlockSpec((D_MODEL, N_SEC * SEC), const),
            pl.BlockSpec((tm, LANES), lambda i: (i % tiles_per_seq, 0)),
            pl.BlockSpec((tm, LANES), lambda i: (i % tiles_per_seq, 0)),
            pl.BlockSpec((SEC, SEC), const),
            pl.BlockSpec((1, SEC), const),
            pl.BlockSpec((1, SEC), const),
            pl.BlockSpec((1, MEM_HEAD_DIM), const),
        ],
        out_specs=[pl.BlockSpec((tm, o.shape[1]), row) for o in outs],
        out_shape=outs,
        compiler_params=pltpu.CompilerParams(vmem_limit_bytes=VMEM_LIMIT),
        name="in_proj",
    )(x2d, g_mix, w_in, cos_t, sin_t, bd, gq, gk, gqm)


def _diff_attn_kernel(q_ref, k_ref, vt_ref, lam_ref, gs_ref, o_ref, acc_sc, s0_sc, *, lam_init):
    i = pl.program_id(2)
    tq = q_ref.shape[0]
    q = q_ref[...]
    lane = lax.broadcasted_iota(jnp.int32, q.shape, 1)
    zero = jnp.zeros_like(q)
    qq = jnp.concatenate([jnp.where(lane < DIFF_HEAD_DIM, q, zero),
                          jnp.where(lane >= DIFF_HEAD_DIM, q, zero)], axis=0)
    acc_sc[...] = jnp.zeros(acc_sc.shape, F32)
    n_strips = 2 * tq // ATTN_STRIP

    def scores(j, c, nk):
        off = pl.multiple_of(j * tq, tq)
        return lax.dot_general(k_ref[pl.ds(off, nk), :], qq[c * ATTN_STRIP:(c + 1) * ATTN_STRIP, :],
                               (((1,), (1,)), ((), ())), preferred_element_type=F32)

    def step(j, ms, ls, masked):
        q_offs = [(c * ATTN_STRIP) % tq for c in range(n_strips)]
        nks = [min(tq, qo + ATTN_STRIP) if masked else tq for qo in q_offs]
        ms_new, ls_new = [], []
        s = s0_sc[0:nks[0], :]
        for c in range(n_strips):
            cols = slice(c * ATTN_STRIP, (c + 1) * ATTN_STRIP)
            if c + 1 < n_strips:
                s_next = scores(j, c + 1, nks[c + 1])
            elif not masked:
                s0_sc[...] = scores(j + 1, 0, tq)
            if masked:
                r = lax.broadcasted_iota(jnp.int32, s.shape, 0)
                col = lax.broadcasted_iota(jnp.int32, s.shape, 1)
                s = jnp.where((r // CHUNK) <= ((col + q_offs[c]) // CHUNK), s, NEG_BIG)
            m_new = jnp.maximum(ms[c], jnp.max(s, axis=0, keepdims=True))
            p = jnp.exp2(s - m_new)
            alpha = jnp.exp2(ms[c] - m_new)
            ls_new.append(alpha * ls[c] + jnp.sum(p, axis=0, keepdims=True))
            ms_new.append(m_new)
            acc_sc[:, cols] = alpha * acc_sc[:, cols] + jnp.dot(
                vt_ref[0, 0, j][:, :nks[c]], p.astype(BF16), preferred_element_type=F32)
            if c + 1 < n_strips:
                s = s_next
        return tuple(ms_new), tuple(ls_new)

    s0_sc[...] = scores(0, 0, tq)
    m0 = tuple(jnp.full((1, ATTN_STRIP), NEG_BIG, F32) for _ in range(n_strips))
    l0 = tuple(jnp.zeros((1, ATTN_STRIP), F32) for _ in range(n_strips))
    m, l = lax.fori_loop(0, i, lambda j, c: step(j, c[0], c[1], False), (m0, l0))
    m, l = step(i, m, l, True)

    ot = acc_sc[...] / jnp.concatenate(l, axis=1)
    lam = (jnp.exp(jnp.sum(lam_ref[0:1, :] * lam_ref[1:2, :], axis=-1, keepdims=True))
           - jnp.exp(jnp.sum(lam_ref[2:3, :] * lam_ref[3:4, :], axis=-1, keepdims=True))
           + lam_init)
    d = (ot[:, :tq] - lam * ot[:, tq:]).T
    o_ref[...] = (_rms(d) * gs_ref[...] * (1.0 - lam_init)).astype(BF16)


def _diff_attn(qn, kn, vt, lam_rows, g_subln, batch, seq, lam_init):
    t = qn.shape[0]
    nq = seq // TQ
    return pl.pallas_call(
        functools.partial(_diff_attn_kernel, lam_init=lam_init),
        grid=(batch, DIFF_HEADS, nq),
        in_specs=[
            pl.BlockSpec((TQ, DIFF_V_DIM), lambda b, h, i: (b * nq + i, h)),
            pl.BlockSpec((seq, DIFF_V_DIM), lambda b, h, i: (b, h)),
            pl.BlockSpec((1, 1, nq, DIFF_V_DIM, TQ), lambda b, h, i: (b, h, 0, 0, 0)),
            pl.BlockSpec((4, DIFF_HEAD_DIM), lambda b, h, i: (0, 0)),
            pl.BlockSpec((1, DIFF_V_DIM), lambda b, h, i: (0, 0)),
        ],
        out_specs=pl.BlockSpec((TQ, DIFF_V_DIM), lambda b, h, i: (b * nq + i, h)),
        out_shape=jax.ShapeDtypeStruct((t, SEC), BF16),
        scratch_shapes=[pltpu.VMEM((DIFF_V_DIM, 2 * TQ), F32), pltpu.VMEM((TQ, ATTN_STRIP), F32)],
        compiler_params=pltpu.CompilerParams(vmem_limit_bytes=VMEM_LIMIT),
        name="diff_attn",
    )(qn, kn, vt, lam_rows, g_subln)


def _post_kernel(x_ref, yd_ref, cb_ref, u_ref, up_ref, mq_ref, sig_ref, km_ref, vm_ref,
                 cw_ref, wb_ref, wo_ref, gf_ref, wr_ref, br_ref, tri_ref, ltri_ref,
                 x1_out, rows_out, slot_out, gate_out, cbase_out, ntile_out, off_out, cnt_out, carry_sc,
                 *, seq):
    i = pl.program_id(0)
    tm = x_ref.shape[0]

    @pl.when(i == 0)
    def _():
        carry_sc[...] = jnp.zeros(carry_sc.shape, F32)

    u = u_ref[...]
    seq_start = (i * tm) % seq == 0
    up = jnp.where(seq_start, 0.0, up_ref[...])
    r = lax.broadcasted_iota(jnp.int32, u.shape, 0)
    u1 = jnp.where(r == 0, up[7:8, :], pltpu.roll(u, 1, 0))
    u2 = jnp.where(r == 0, up[6:7, :], jnp.where(r == 1, up[7:8, :], pltpu.roll(u, 2, 0)))
    y_conv = cb_ref[...].astype(F32) * (cw_ref[0:1, :] * u2 + cw_ref[1:2, :] * u1 + cw_ref[2:3, :] * u)

    y_mem = []
    for hd in range(MEM_HEADS):
        sl = slice(hd * MEM_HEAD_DIM, (hd + 1) * MEM_HEAD_DIM)
        s = lax.dot_general(mq_ref[:, sl], km_ref[:, sl], (((1,), (1,)), ((), ())),
                            preferred_element_type=F32)
        p = jnp.exp2(s - jnp.max(s, axis=-1, keepdims=True))
        o = jnp.dot(p.astype(BF16), vm_ref[:, sl], preferred_element_type=F32)
        y_mem.append(o / jnp.sum(p, axis=-1, keepdims=True))

    merged = (sig_ref[:, 0:D_MODEL].astype(F32)
              * jnp.dot(yd_ref[...], wb_ref[0], preferred_element_type=F32))
    merged += (sig_ref[:, D_MODEL:2 * D_MODEL].astype(F32)
               * jnp.dot(y_conv.astype(BF16), wb_ref[1], preferred_element_type=F32))
    merged += (sig_ref[:, 2 * D_MODEL:3 * D_MODEL].astype(F32)
               * jnp.dot(jnp.concatenate(y_mem, axis=1).astype(BF16), wb_ref[2],
                         preferred_element_type=F32))
    x1 = x_ref[...] + jnp.dot(merged.astype(BF16), wo_ref[...], preferred_element_type=F32)
    x1_out[...] = x1

    h2 = _rms(x1) * gf_ref[...]

    logits = lax.dot_general(wr_ref[...], h2, (((1,), (1,)), ((), ())),
                             precision=lax.Precision.HIGHEST,
                             preferred_element_type=F32) + br_ref[...]
    eio = lax.broadcasted_iota(jnp.int32, logits.shape, 0)
    work = logits
    vals, hots = [], []
    for k in range(TOP_K):
        mk = jnp.max(work, axis=0, keepdims=True)
        ik = jnp.min(jnp.where(work == mk, eio, N_EXPERTS), axis=0, keepdims=True)
        hot = eio == ik
        work = jnp.where(hot, -jnp.inf, work)
        vals.append(mk)
        hots.append(hot)
    ex = [jnp.exp(v - vals[0]) for v in vals]
    den = ex[0] + ex[1] + ex[2] + ex[3]
    gates = [e / den for e in ex]

    assign = jnp.zeros(logits.shape, F32)
    for hot in hots:
        assign = jnp.where(hot, 1.0, assign)
    earlier = jnp.dot(assign.astype(BF16), tri_ref[...], preferred_element_type=F32)
    n_col = jnp.sum(assign, axis=1, keepdims=True)
    off_col = jnp.dot(ltri_ref[...], jnp.broadcast_to(n_col, (N_EXPERTS, LANES)),
                      precision=lax.Precision.HIGHEST, preferred_element_type=F32)[:, 0:1]
    slots = [jnp.sum(jnp.where(hot, earlier + off_col, 0.0), axis=0, keepdims=True).astype(jnp.int32)
             for hot in hots]
    for k in range(TOP_K):
        slot_out[k:k + 1, :] = slots[k]
        gate_out[k:k + 1, :] = gates[k]

    n_slots = TOP_K * tm
    jdx = lax.broadcasted_iota(jnp.int32, (n_slots, tm), 0)
    pick = jnp.where(jdx == slots[0], 1.0, jnp.where(jdx == slots[1], 1.0, jnp.where(
        jdx == slots[2], 1.0, jnp.where(jdx == slots[3], 1.0, 0.0)))).astype(BF16)
    rows = jnp.dot(pick, h2.astype(BF16), preferred_element_type=F32)
    for c in range(ROW_CHUNKS):
        rows_out[:, c, :] = rows[:, c * LANES:(c + 1) * LANES]

    cbase_out[...] = jnp.broadcast_to(carry_sc[...], cbase_out.shape).astype(jnp.int32)
    ntile_out[...] = jnp.broadcast_to(n_col, ntile_out.shape).astype(jnp.int32)
    off_out[...] = jnp.broadcast_to(off_col, off_out.shape).astype(jnp.int32)
    carry_sc[...] = carry_sc[...] + n_col
    cnt_out[...] = jnp.broadcast_to(carry_sc[...], cnt_out.shape).astype(jnp.int32)


def _post(x2d, y_diff, cb, u, mqn, sig, km, vm, conv_w, w_branch, w_out, g_ffn, w_rt, b_r, tri, ltri,
          seq):
    t = x2d.shape[0]
    tm = TM_POST
    n_tiles = t // tm
    row = lambda i: (i, 0)
    const = lambda i: (0, 0)
    table = jax.ShapeDtypeStruct((n_tiles * N_EXPERTS, LANES), jnp.int32)
    outs = [
        jax.ShapeDtypeStruct((t, D_MODEL), F32),
        jax.ShapeDtypeStruct((TOP_K * t, ROW_CHUNKS, LANES), F32),
        jax.ShapeDtypeStruct((TOP_K, t), jnp.int32),
        jax.ShapeDtypeStruct((TOP_K, t), F32),
        table, table, table,
        jax.ShapeDtypeStruct((N_EXPERTS, LANES), jnp.int32),
    ]
    return pl.pallas_call(
        functools.partial(_post_kernel, seq=seq),
        grid=(t // tm,),
        in_specs=[
            pl.BlockSpec((tm, D_MODEL), row),
            pl.BlockSpec((tm, SEC), row),
            pl.BlockSpec((tm, SEC), row),
            pl.BlockSpec((tm, SEC), row),
            pl.BlockSpec((8, SEC), lambda i: (jnp.maximum(i * (tm // 8) - 1, 0), 0)),
            pl.BlockSpec((tm, SEC), row),
            pl.BlockSpec((tm, 3 * D_MODEL), row),
            pl.BlockSpec((N_MEM, SEC), lambda i: ((i * tm) // seq, 0)),
            pl.BlockSpec((N_MEM, SEC), lambda i: ((i * tm) // seq, 0)),
            pl.BlockSpec((3, SEC), const),
            pl.BlockSpec((3, SEC, D_MODEL), lambda i: (0, 0, 0)),
            pl.BlockSpec((D_MODEL, D_MODEL), const),
            pl.BlockSpec((1, D_MODEL), const),
            pl.BlockSpec((N_EXPERTS, D_MODEL), const),
            pl.BlockSpec((N_EXPERTS, 1), const),
            pl.BlockSpec((tm, tm), const),
            pl.BlockSpec((N_EXPERTS, N_EXPERTS), const),
        ],
        out_specs=[
            pl.BlockSpec((tm, D_MODEL), row),
            pl.BlockSpec((TOP_K * tm, ROW_CHUNKS, LANES), lambda i: (i, 0, 0)),
            pl.BlockSpec((TOP_K, tm), lambda i: (0, i)),
            pl.BlockSpec((TOP_K, tm), lambda i: (0, i)),
            pl.BlockSpec((N_EXPERTS, LANES), row),
            pl.BlockSpec((N_EXPERTS, LANES), row),
            pl.BlockSpec((N_EXPERTS, LANES), row),
            pl.BlockSpec((N_EXPERTS, LANES), const),
        ],
        out_shape=outs,
        scratch_shapes=[pltpu.VMEM((N_EXPERTS, 1), F32)],
        compiler_params=pltpu.CompilerParams(dimension_semantics=("arbitrary",),
                                             vmem_limit_bytes=VMEM_LIMIT),
        name="post",
    )(x2d, y_diff, cb, u, u, mqn, sig, km, vm, conv_w, w_branch, w_out, g_ffn, w_rt, b_r, tri, ltri)


def _copy_run(src_hbm, buf, sem, src_row, dst_row, n):
    piece = MAX_RUN
    while piece >= 1:
        take = (n & piece) != 0

        @pl.when(take)
        def _(piece=piece, src_row=src_row, dst_row=dst_row):
            pltpu.make_async_copy(src_hbm.at[pl.ds(src_row, piece)], buf.at[pl.ds(dst_row, piece)],
                                  sem).start()

        step = jnp.where(take, piece, 0)
        src_row = src_row + step
        dst_row = dst_row + step
        piece //= 2


def _wait_rows(n, src_hbm, buf, sem, base):
    pltpu.make_async_copy(src_hbm.at[pl.ds(0, n)], buf.at[pl.ds(base, n)], sem).wait()


def _experts_kernel(be_ref, t0_ref, na_ref, ps_ref, cnt_ref, cb_ref, nt_ref, of_ref,
                    rows_hbm, zeros_hbm, wgu_ref, bgu_ref, wd_ref, bd_ref,
                    ys_out, xbuf, sem, *, n_tiles, slots_per_tile):
    b = pl.program_id(0)
    n_active = na_ref[0]
    slot = b % 2

    def issue(blk, buf_slot):
        e = be_ref[blk]
        r0 = blk * MOE_BLOCK - ps_ref[e]
        r1 = r0 + MOE_BLOCK
        base = buf_slot * MOE_BLOCK
        sm = sem.at[buf_slot]

        def cond(i):
            return jnp.logical_and(i < n_tiles, cb_ref[e * n_tiles + jnp.minimum(i, n_tiles - 1)] < r1)

        def body(i):
            c = cb_ref[e * n_tiles + i]
            lo = jnp.maximum(c, r0)
            hi = jnp.minimum(c + nt_ref[e * n_tiles + i], r1)
            src = i * slots_per_tile + of_ref[e * n_tiles + i] + (lo - c)
            _copy_run(rows_hbm, xbuf, sm, src, base + (lo - r0), jnp.maximum(hi - lo, 0))
            return i + 1

        lax.while_loop(cond, body, t0_ref[blk])
        valid = jnp.clip(cnt_ref[e] - r0, 0, MOE_BLOCK)
        _copy_run(zeros_hbm, xbuf, sm, 0, base + valid, MOE_BLOCK - valid)

    @pl.when(b == 0)
    def _():
        issue(b, 0)

    @pl.when(b + 1 < n_active)
    def _():
        issue(b + 1, 1 - slot)

    @pl.when(b < n_active)
    def _():
        base = pl.multiple_of(slot * MOE_BLOCK, MOE_BLOCK)
        _wait_rows(MOE_BLOCK, rows_hbm, xbuf, sem.at[slot], base)
        x = jnp.concatenate([xbuf[pl.ds(base, MOE_BLOCK), c, :] for c in range(ROW_CHUNKS)],
                            axis=1).astype(BF16)
        gu = jnp.dot(x, wgu_ref[0], preferred_element_type=F32) + bgu_ref[0]
        g = jnp.minimum(gu[:, :D_MODEL], SWIGLU_LIMIT)
        u = jnp.clip(gu[:, D_MODEL:], -SWIGLU_LIMIT, SWIGLU_LIMIT)
        act = (u + 1.0) * (g * (1.0 / (1.0 + jnp.exp(-SWIGLU_ALPHA * g))))
        y = jnp.dot(act.astype(BF16), wd_ref[0], preferred_element_type=F32) + bd_ref[0]
        for c in range(ROW_CHUNKS):
            ys_out[:, c, :] = y[:, c * LANES:(c + 1) * LANES]

    @pl.when(b >= n_active)
    def _():
        ys_out[...] = jnp.zeros(ys_out.shape, F32)


def _experts(block_expert, first_tile, n_active, pstart, counts, cbase_e, ntile_e, off_e,
             rows, zero_rows, wgu, bgu, wd, bd, n_tiles):
    n_blocks = block_expert.shape[0]
    ff2 = wgu.shape[2]
    by_expert = lambda b, be, *_: (be[b], 0, 0)
    grid_spec = pltpu.PrefetchScalarGridSpec(
        num_scalar_prefetch=8,
        grid=(n_blocks,),
        in_specs=[
            pl.BlockSpec(memory_space=pl.ANY),
            pl.BlockSpec(memory_space=pl.ANY),
            pl.BlockSpec((1, D_MODEL, ff2), by_expert),
            pl.BlockSpec((1, 1, ff2), by_expert),
            pl.BlockSpec((1, ff2 // 2, D_MODEL), by_expert),
            pl.BlockSpec((1, 1, D_MODEL), by_expert),
        ],
        out_specs=pl.BlockSpec((MOE_BLOCK, ROW_CHUNKS, LANES), lambda b, *_: (b, 0, 0)),
        scratch_shapes=[
            pltpu.VMEM((2 * MOE_BLOCK, ROW_CHUNKS, LANES), F32),
            pltpu.SemaphoreType.DMA((2,)),
        ],
    )
    return pl.pallas_call(
        functools.partial(_experts_kernel, n_tiles=n_tiles, slots_per_tile=TOP_K * TM_POST),
        grid_spec=grid_spec,
        out_shape=jax.ShapeDtypeStruct((n_blocks * MOE_BLOCK, ROW_CHUNKS, LANES), F32),
        compiler_params=pltpu.CompilerParams(dimension_semantics=("arbitrary",),
                                             vmem_limit_bytes=VMEM_LIMIT),
        name="experts",
    )(block_expert, first_tile, n_active, pstart, counts, cbase_e, ntile_e, off_e,
      rows, zero_rows, wgu, bgu, wd, bd)


def _combine_kernel(ps_ref, cb_ref, nt_ref, of_ref, slot_ref, gate_ref, x1_ref, ys_hbm, o_ref, buf,
                    sem):
    i = pl.program_id(0)
    n = pl.num_programs(0)
    tm = x1_ref.shape[0]
    n_slots = TOP_K * tm
    cur = i % 2

    def issue(tile, buf_slot):
        def body(e, carry):
            k = tile * N_EXPERTS + e
            _copy_run(ys_hbm, buf, sem.at[buf_slot], ps_ref[e] + cb_ref[k],
                      buf_slot * n_slots + of_ref[k], nt_ref[k])
            return carry
        lax.fori_loop(0, N_EXPERTS, body, 0)

    @pl.when(i == 0)
    def _():
        issue(i, 0)

    @pl.when(i + 1 < n)
    def _():
        issue(i + 1, 1 - cur)

    base = pl.multiple_of(cur * n_slots, n_slots)
    _wait_rows(n_slots, ys_hbm, buf, sem.at[cur], base)
    y = jnp.concatenate([buf[pl.ds(base, n_slots), c, :] for c in range(ROW_CHUNKS)],
                        axis=1).astype(BF16)
    jdx = lax.broadcasted_iota(jnp.int32, (n_slots, tm), 0)
    pick = jnp.where(jdx == slot_ref[0:1, :], gate_ref[0:1, :], jnp.where(
        jdx == slot_ref[1:2, :], gate_ref[1:2, :], jnp.where(
            jdx == slot_ref[2:3, :], gate_ref[2:3, :], jnp.where(
                jdx == slot_ref[3:4, :], gate_ref[3:4, :], 0.0)))).astype(BF16)
    o_ref[...] = x1_ref[...] + lax.dot_general(pick, y, (((0,), (0,)), ((), ())),
                                               preferred_element_type=F32)


def _combine(pstart, cbase_t, ntile_t, off_t, slots, gates, x1, ys_rows):
    t = x1.shape[0]
    tm = TM_POST
    n_slots = TOP_K * tm
    grid_spec = pltpu.PrefetchScalarGridSpec(
        num_scalar_prefetch=4,
        grid=(t // tm,),
        in_specs=[
            pl.BlockSpec((TOP_K, tm), lambda i, *_: (0, i)),
            pl.BlockSpec((TOP_K, tm), lambda i, *_: (0, i)),
            pl.BlockSpec((tm, D_MODEL), lambda i, *_: (i, 0)),
            pl.BlockSpec(memory_space=pl.ANY),
        ],
        out_specs=pl.BlockSpec((tm, D_MODEL), lambda i, *_: (i, 0)),
        scratch_shapes=[
            pltpu.VMEM((2 * n_slots, ROW_CHUNKS, LANES), F32),
            pltpu.SemaphoreType.DMA((2,)),
        ],
    )
    return pl.pallas_call(
        _combine_kernel,
        grid_spec=grid_spec,
        out_shape=jax.ShapeDtypeStruct((t, D_MODEL), F32),
        compiler_params=pltpu.CompilerParams(dimension_semantics=("arbitrary",),
                                             vmem_limit_bytes=VMEM_LIMIT),
        name="combine",
    )(pstart, cbase_t, ntile_t, off_t, slots, gates, x1, ys_rows)


def kernel(x, mem, g_mix, w_in, g_q_diff, g_k_diff, lambda_q1, lambda_k1, lambda_q2, lambda_k2,
           g_subln, conv_w, g_mem, w_mem_kv, g_q_mem, g_k_mem, w_branch, w_out, g_ffn, w_router,
           b_router, w_gate_up, b_gate_up, w_down, b_down):
    batch, seq, d = x.shape
    t = batch * seq
    depth = g_mix.shape[0]
    assert depth == 1 and d == D_MODEL and seq % TQ == 0 and t % TM_PROJ == 0

    pos = jnp.arange(seq, dtype=F32)
    inv_freq = 1.0 / (ROPE_THETA ** (jnp.arange(0, DIFF_HEAD_DIM, 2, dtype=F32) / DIFF_HEAD_DIM))
    ang = pos[:, None] * inv_freq[None, :]
    cos_t = jnp.tile(jnp.cos(ang), (1, 4))
    sin_t = jnp.tile(jnp.concatenate([-jnp.sin(ang), jnp.sin(ang)], axis=1), (1, 2))
    grp = jnp.arange(SEC) // DIFF_HEAD_DIM
    bd = jnp.where(grp[:, None] == grp[None, :], 1.0 / DIFF_HEAD_DIM, 0.0).astype(BF16)
    tok = jnp.arange(TM_POST)
    tri = (tok[:, None] < tok[None, :]).astype(BF16)

    x2d = x.reshape(t, d)
    l = 0
    lam_init = 0.8 - 0.6 * math.exp(-0.3 * l)
    km, vm = _mem_kv(mem.reshape(batch * N_MEM, d), g_mem[l][None, :], w_mem_kv[l].astype(BF16),
                     g_k_mem[l][None, :])
    qn, kn, vv, cb, u, mqn, sig = _in_proj(
        x2d, g_mix[l][None, :], w_in[l].astype(BF16), cos_t, sin_t, bd,
        jnp.tile(g_q_diff[l], SEC // DIFF_HEAD_DIM)[None, :],
        jnp.tile(g_k_diff[l], SEC // DIFF_HEAD_DIM)[None, :],
        g_q_mem[l][None, :], seq)
    lam_rows = jnp.stack([lambda_q1[l], lambda_k1[l], lambda_q2[l], lambda_k2[l]]).astype(F32)
    vt = vv.reshape(batch, seq // TQ, TQ, DIFF_HEADS, DIFF_V_DIM).transpose(0, 3, 1, 4, 2)
    y_diff = _diff_attn(qn, kn, vt, lam_rows, g_subln[l][None, :], batch, seq, lam_init)
    ex = jnp.arange(N_EXPERTS)
    ltri = (ex[None, :] < ex[:, None]).astype(F32)
    x1, rows, slots, gates, cbase, ntile, off, counts = _post(
        x2d, y_diff, cb, u, mqn, sig, km, vm, conv_w[l], w_branch[l].astype(BF16),
        w_out[l].astype(BF16), g_ffn[l][None, :], w_router[l].T, b_router[l][:, None], tri, ltri, seq)

    n_tiles = t // TM_POST
    n_assign = t * TOP_K
    n_rows = -(-(n_assign + N_EXPERTS * (MOE_BLOCK - 1)) // MOE_BLOCK) * MOE_BLOCK
    n_blocks = n_rows // MOE_BLOCK
    cnt = counts[:, 0]
    padded = (cnt + MOE_BLOCK - 1) // MOE_BLOCK * MOE_BLOCK
    pend = jnp.cumsum(padded)
    pstart = pend - padded
    block_row0 = jnp.arange(n_blocks, dtype=jnp.int32) * MOE_BLOCK
    block_expert = jnp.minimum(
        jnp.sum((pend[None, :] <= block_row0[:, None]).astype(jnp.int32), axis=1), N_EXPERTS - 1)
    n_active = (pend[-1:] // MOE_BLOCK).astype(jnp.int32)
    cbase_t = cbase[:, 0].reshape(n_tiles, N_EXPERTS)
    ntile_t = ntile[:, 0].reshape(n_tiles, N_EXPERTS)
    off_t = off[:, 0].reshape(n_tiles, N_EXPERTS)
    cend_b = (cbase_t + ntile_t).T[block_expert]
    r0 = block_row0 - pstart[block_expert]
    first_tile = jnp.sum((cend_b <= r0[:, None]).astype(jnp.int32), axis=1)
    first_tile = jnp.minimum(first_tile, n_tiles - 1).astype(jnp.int32)

    zero_rows = jnp.zeros((MOE_BLOCK, ROW_CHUNKS, LANES), F32)
    ys_rows = _experts(block_expert, first_tile, n_active, pstart, cnt,
                       cbase_t.T.reshape(-1), ntile_t.T.reshape(-1), off_t.T.reshape(-1),
                       rows, zero_rows, w_gate_up[l].astype(BF16), b_gate_up[l][:, None, :],
                       w_down[l].astype(BF16), b_down[l][:, None, :], n_tiles)
    out = _combine(pstart, cbase_t.reshape(-1), ntile_t.reshape(-1), off_t.reshape(-1), slots, gates,
                   x1, ys_rows)
    return out.reshape(batch, seq, d)
```

```python
import functools
import math

import jax
import jax.numpy as jnp
from jax import lax
from jax.experimental import pallas as pl
from jax.experimental.pallas import tpu as pltpu

F32 = jnp.float32
BF16 = jnp.bfloat16

D_MODEL = 1024
CHUNK = 64
EPS = 1e-6
ROPE_THETA = 10000.0
DIFF_HEADS = 4
DIFF_HEAD_DIM = 64
DIFF_V_DIM = 2 * DIFF_HEAD_DIM
N_MEM = 256
MEM_HEADS = 4
MEM_HEAD_DIM = 128
SEC = 512
N_SEC = 13
N_EXPERTS = 32
TOP_K = 4
SWIGLU_LIMIT = 7.0
SWIGLU_ALPHA = 1.702
MOE_BLOCK = 512
LANES = 128
ROW_CHUNKS = D_MODEL // LANES
LOG2E = 1.4426950408889634
NEG_BIG = -1e30

TM_PROJ = 256
TQ = 512
ATTN_STRIP = 256
TM_POST = 256
MAX_RUN = max(MOE_BLOCK, TM_POST)
FF_CHUNK = 256
VMEM_LIMIT = 48 * 1024 * 1024


def _rms(x, eps=EPS):
    return x * lax.rsqrt(jnp.mean(x * x, axis=-1, keepdims=True) + eps)


def _mem_kv_kernel(mem_ref, g_ref, w_ref, gk_ref, k_out, v_out):
    h = (_rms(mem_ref[...]) * g_ref[...]).astype(BF16)
    kv = jnp.dot(h, w_ref[...], preferred_element_type=F32)
    for hd in range(MEM_HEADS):
        sl = slice(hd * MEM_HEAD_DIM, (hd + 1) * MEM_HEAD_DIM)
        k_out[:, sl] = (_rms(kv[:, sl]) * gk_ref[...]).astype(BF16)
    v_out[...] = kv[:, SEC:].astype(BF16)


def _mem_kv(mem2d, g_mem, w_kv, g_k):
    rows = mem2d.shape[0]
    return pl.pallas_call(
        _mem_kv_kernel,
        grid=(rows // N_MEM,),
        in_specs=[
            pl.BlockSpec((N_MEM, D_MODEL), lambda i: (i, 0)),
            pl.BlockSpec((1, D_MODEL), lambda i: (0, 0)),
            pl.BlockSpec((D_MODEL, 2 * SEC), lambda i: (0, 0)),
            pl.BlockSpec((1, MEM_HEAD_DIM), lambda i: (0, 0)),
        ],
        out_specs=[pl.BlockSpec((N_MEM, SEC), lambda i: (i, 0))] * 2,
        out_shape=[jax.ShapeDtypeStruct((rows, SEC), BF16)] * 2,
        compiler_params=pltpu.CompilerParams(vmem_limit_bytes=VMEM_LIMIT),
        name="mem_kv",
    )(mem2d, g_mem, w_kv, g_k)


def _in_proj_kernel(x_ref, g_ref, w_ref, cos_ref, sin_ref, bd_ref, gq_ref, gk_ref, gqm_ref,
                    q_out, k_out, v_out, cb_out, u_out, mq_out, sig_out):
    tm = x_ref.shape[0]
    h = (_rms(x_ref[...]) * g_ref[...]).astype(BF16)

    def proj(sec):
        return jnp.dot(h, w_ref[:, sec * SEC:(sec + 1) * SEC], preferred_element_type=F32)

    cos = jnp.concatenate([cos_ref[...]] * (SEC // LANES), axis=1)
    sin = jnp.concatenate([sin_ref[...]] * (SEC // LANES), axis=1)
    lane = lax.broadcasted_iota(jnp.int32, (tm, SEC), 1)
    first_half = (lane & (DIFF_HEAD_DIM // 2)) == 0

    def norm_rope(a, g, scale):
        ms = jnp.dot((a * a).astype(BF16), bd_ref[...], preferred_element_type=F32)
        y = a * lax.rsqrt(ms + EPS) * g
        partner = jnp.where(first_half,
                            pltpu.roll(y, SEC - DIFF_HEAD_DIM // 2, 1),
                            pltpu.roll(y, DIFF_HEAD_DIM // 2, 1))
        return (y * cos + partner * sin) * scale

    q_out[...] = norm_rope(proj(0), gq_ref[...], DIFF_HEAD_DIM ** -0.5 * LOG2E).astype(BF16)
    k_out[...] = norm_rope(proj(1), gk_ref[...], 1.0).astype(BF16)
    v_out[...] = proj(2).astype(BF16)
    cb_out[...] = proj(3).astype(BF16)
    u_out[...] = proj(4) * proj(5)
    mq = proj(6)
    for hd in range(MEM_HEADS):
        sl = slice(hd * MEM_HEAD_DIM, (hd + 1) * MEM_HEAD_DIM)
        mq_out[:, sl] = (_rms(mq[:, sl]) * gqm_ref[...]
                         * (MEM_HEAD_DIM ** -0.5 * LOG2E)).astype(BF16)
    for s in range(7, N_SEC):
        a = proj(s)
        sig_out[:, (s - 7) * SEC:(s - 6) * SEC] = (1.0 / (1.0 + jnp.exp(-a))).astype(BF16)


def _in_proj(x2d, g_mix, w_in, cos_t, sin_t, bd, gq, gk, gqm, seq):
    t = x2d.shape[0]
    tm = TM_PROJ
    tiles_per_seq = seq // tm
    row = lambda i: (i, 0)
    const = lambda i: (0, 0)
    outs = [
        jax.ShapeDtypeStruct((t, SEC), BF16),
        jax.ShapeDtypeStruct((t, SEC), BF16),
        jax.ShapeDtypeStruct((t, SEC), BF16),
        jax.ShapeDtypeStruct((t, SEC), BF16),
        jax.ShapeDtypeStruct((t, SEC), F32),
        jax.ShapeDtypeStruct((t, SEC), BF16),
        jax.ShapeDtypeStruct((t, 3 * D_MODEL), BF16),
    ]
    return pl.pallas_call(
        _in_proj_kernel,
        grid=(t // tm,),
        in_specs=[
            pl.BlockSpec((tm, D_MODEL), row),
            pl.BlockSpec((1, D_MODEL), const),
            pl.BlockSpec((D_MODEL, N_SEC * SEC), const),
            pl.BlockSpec((tm, LANES), lambda i: (i % tiles_per_seq, 0)),
            pl.BlockSpec((tm, LANES), lambda i: (i % tiles_per_seq, 0)),
            pl.BlockSpec((SEC, SEC), const),
            pl.BlockSpec((1, SEC), const),
            pl.BlockSpec((1, SEC), const),
            pl.BlockSpec((1, MEM_HEAD_DIM), const),
        ],
        out_specs=[pl.BlockSpec((tm, o.shape[1]), row) for o in outs],
        out_shape=outs,
        compiler_params=pltpu.CompilerParams(vmem_limit_bytes=VMEM_LIMIT),
        name="in_proj",
    )(x2d, g_mix, w_in, cos_t, sin_t, bd, gq, gk, gqm)


def _diff_attn_kernel(q_ref, k_ref, vt_ref, lam_ref, gs_ref, o_ref, acc_sc, s0_sc, *, lam_init):
    i = pl.program_id(2)
    tq = q_ref.shape[0]
    q = q_ref[...]
    lane = lax.broadcasted_iota(jnp.int32, q.shape, 1)
    zero = jnp.zeros_like(q)
    qq = jnp.concatenate([jnp.where(lane < DIFF_HEAD_DIM, q, zero),
                          jnp.where(lane >= DIFF_HEAD_DIM, q, zero)], axis=0)
    acc_sc[...] = jnp.zeros(acc_sc.shape, F32)
    n_strips = 2 * tq // ATTN_STRIP

    def scores(j, c, nk):
        off = pl.multiple_of(j * tq, tq)
        return lax.dot_general(k_ref[pl.ds(off, nk), :], qq[c * ATTN_STRIP:(c + 1) * ATTN_STRIP, :],
                               (((1,), (1,)), ((), ())), preferred_element_type=F32)

    def step(j, ms, ls, masked):
        q_offs = [(c * ATTN_STRIP) % tq for c in range(n_strips)]
        nks = [min(tq, qo + ATTN_STRIP) if masked else tq for qo in q_offs]
        ms_new, ls_new = [], []
        s = s0_sc[0:nks[0], :]
        for c in range(n_strips):
            cols = slice(c * ATTN_STRIP, (c + 1) * ATTN_STRIP)
            if c + 1 < n_strips:
                s_next = scores(j, c + 1, nks[c + 1])
            elif not masked:
                s0_sc[...] = scores(j + 1, 0, tq)
            if masked:
                r = lax.broadcasted_iota(jnp.int32, s.shape, 0)
                col = lax.broadcasted_iota(jnp.int32, s.shape, 1)
                s = jnp.where((r // CHUNK) <= ((col + q_offs[c]) // CHUNK), s, NEG_BIG)
            m_new = jnp.maximum(ms[c], jnp.max(s, axis=0, keepdims=True))
            p = jnp.exp2(s - m_new)
            alpha = jnp.exp2(ms[c] - m_new)
            ls_new.append(alpha * ls[c] + jnp.sum(p, axis=0, keepdims=True))
            ms_new.append(m_new)
            acc_sc[:, cols] = alpha * acc_sc[:, cols] + jnp.dot(
                vt_ref[0, 0, j][:, :nks[c]], p.astype(BF16), preferred_element_type=F32)
            if c + 1 < n_strips:
                s = s_next
        return tuple(ms_new), tuple(ls_new)

    s0_sc[...] = scores(0, 0, tq)
    m0 = tuple(jnp.full((1, ATTN_STRIP), NEG_BIG, F32) for _ in range(n_strips))
    l0 = tuple(jnp.zeros((1, ATTN_STRIP), F32) for _ in range(n_strips))
    m, l = lax.fori_loop(0, i, lambda j, c: step(j, c[0], c[1], False), (m0, l0))
    m, l = step(i, m, l, True)

    ot = acc_sc[...] / jnp.concatenate(l, axis=1)
    lam = (jnp.exp(jnp.sum(lam_ref[0:1, :] * lam_ref[1:2, :], axis=-1, keepdims=True))
           - jnp.exp(jnp.sum(lam_ref[2:3, :] * lam_ref[3:4, :], axis=-1, keepdims=True))
           + lam_init)
    d = (ot[:, :tq] - lam * ot[:, tq:]).T
    o_ref[...] = (_rms(d) * gs_ref[...] * (1.0 - lam_init)).astype(BF16)


def _diff_attn(qn, kn, vt, lam_rows, g_subln, batch, seq, lam_init):
    t = qn.shape[0]
    nq = seq // TQ
    return pl.pallas_call(
        functools.partial(_diff_attn_kernel, lam_init=lam_init),
        grid=(batch, DIFF_HEADS, nq),
        in_specs=[
            pl.BlockSpec((TQ, DIFF_V_DIM), lambda b, h, i: (b * nq + i, h)),
            pl.BlockSpec((seq, DIFF_V_DIM), lambda b, h, i: (b, h)),
            pl.BlockSpec((1, 1, nq, DIFF_V_DIM, TQ), lambda b, h, i: (b, h, 0, 0, 0)),
            pl.BlockSpec((4, DIFF_HEAD_DIM), lambda b, h, i: (0, 0)),
            pl.BlockSpec((1, DIFF_V_DIM), lambda b, h, i: (0, 0)),
        ],
        out_specs=pl.BlockSpec((TQ, DIFF_V_DIM), lambda b, h, i: (b * nq + i, h)),
        out_shape=jax.ShapeDtypeStruct((t, SEC), BF16),
        scratch_shapes=[pltpu.VMEM((DIFF_V_DIM, 2 * TQ), F32), pltpu.VMEM((TQ, ATTN_STRIP), F32)],
        compiler_params=pltpu.CompilerParams(vmem_limit_bytes=VMEM_LIMIT),
        name="diff_attn",
    )(qn, kn, vt, lam_rows, g_subln)


def _post_kernel(x_ref, yd_ref, cb_ref, u_ref, up_ref, mq_ref, sig_ref, km_ref, vm_ref,
                 cw_ref, wb_ref, wo_ref, gf_ref, wr_ref, br_ref, tri_ref, ltri_ref,
                 x1_out, rows_out, slot_out, gate_out, cbase_out, ntile_out, off_out, cnt_out, carry_sc,
                 *, seq):
    i = pl.program_id(0)
    tm = x_ref.shape[0]

    @pl.when(i == 0)
    def _():
        carry_sc[...] = jnp.zeros(carry_sc.shape, F32)

    u = u_ref[...]
    seq_start = (i * tm) % seq == 0
    up = jnp.where(seq_start, 0.0, up_ref[...])
    r = lax.broadcasted_iota(jnp.int32, u.shape, 0)
    u1 = jnp.where(r == 0, up[7:8, :], pltpu.roll(u, 1, 0))
    u2 = jnp.where(r == 0, up[6:7, :], jnp.where(r == 1, up[7:8, :], pltpu.roll(u, 2, 0)))
    y_conv = cb_ref[...].astype(F32) * (cw_ref[0:1, :] * u2 + cw_ref[1:2, :] * u1 + cw_ref[2:3, :] * u)

    y_mem = []
    for hd in range(MEM_HEADS):
        sl = slice(hd * MEM_HEAD_DIM, (hd + 1) * MEM_HEAD_DIM)
        s = lax.dot_general(mq_ref[:, sl], km_ref[:, sl], (((1,), (1,)), ((), ())),
                            preferred_element_type=F32)
        p = jnp.exp2(s - jnp.max(s, axis=-1, keepdims=True))
        o = jnp.dot(p.astype(BF16), vm_ref[:, sl], preferred_element_type=F32)
        y_mem.append(o / jnp.sum(p, axis=-1, keepdims=True))

    merged = (sig_ref[:, 0:D_MODEL].astype(F32)
              * jnp.dot(yd_ref[...], wb_ref[0], preferred_element_type=F32))
    merged += (sig_ref[:, D_MODEL:2 * D_MODEL].astype(F32)
               * jnp.dot(y_conv.astype(BF16), wb_ref[1], preferred_element_type=F32))
    merged += (sig_ref[:, 2 * D_MODEL:3 * D_MODEL].astype(F32)
               * jnp.dot(jnp.concatenate(y_mem, axis=1).astype(BF16), wb_ref[2],
                         preferred_element_type=F32))
    x1 = x_ref[...] + jnp.dot(merged.astype(BF16), wo_ref[...], preferred_element_type=F32)
    x1_out[...] = x1

    h2 = _rms(x1) * gf_ref[...]

    nt = (((1,), (1,)), ((), ()))
    w = wr_ref[...]
    w_hi = w.astype(BF16)
    w_lo = (w - w_hi.astype(F32)).astype(BF16)
    h_hi = h2.astype(BF16)
    h_lo = (h2 - h_hi.astype(F32)).astype(BF16)
    part = lax.dot_general(jnp.concatenate([w_hi, w_lo], axis=0), h_hi, nt,
                           preferred_element_type=F32)
    logits = (part[:N_EXPERTS] + part[N_EXPERTS:]
              + lax.dot_general(w_hi, h_lo, nt, preferred_element_type=F32) + br_ref[...])
    eio = lax.broadcasted_iota(jnp.int32, logits.shape, 0)
    work = logits
    vals, hots = [], []
    for k in range(TOP_K):
        mk = jnp.max(work, axis=0, keepdims=True)
        ik = jnp.min(jnp.where(work == mk, eio, N_EXPERTS), axis=0, keepdims=True)
        hot = eio == ik
        work = jnp.where(hot, -jnp.inf, work)
        vals.append(mk)
        hots.append(hot)
    ex = [jnp.exp(v - vals[0]) for v in vals]
    den = ex[0] + ex[1] + ex[2] + ex[3]
    gates = [e / den for e in ex]

    assign = jnp.zeros(logits.shape, F32)
    for hot in hots:
        assign = jnp.where(hot, 1.0, assign)
    earlier = jnp.dot(assign.astype(BF16), tri_ref[...], preferred_element_type=F32)
    n_col = jnp.sum(assign, axis=1, keepdims=True)
    off_col = jnp.dot(ltri_ref[...], jnp.broadcast_to(n_col, (N_EXPERTS, LANES)),
                      precision=lax.Precision.HIGHEST, preferred_element_type=F32)[:, 0:1]
    slots = [jnp.sum(jnp.where(hot, earlier + off_col, 0.0), axis=0, keepdims=True).astype(jnp.int32)
             for hot in hots]
    for k in range(TOP_K):
        slot_out[k:k + 1, :] = slots[k]
        gate_out[k:k + 1, :] = gates[k]

    n_slots = TOP_K * tm
    jdx = lax.broadcasted_iota(jnp.int32, (n_slots, tm), 0)
    pick = jnp.where(jdx == slots[0], 1.0, jnp.where(jdx == slots[1], 1.0, jnp.where(
        jdx == slots[2], 1.0, jnp.where(jdx == slots[3], 1.0, 0.0)))).astype(BF16)
    rows = jnp.dot(pick, h_hi, preferred_element_type=F32)
    _store_rows(rows_out, n_slots, rows)

    cbase_out[...] = jnp.broadcast_to(carry_sc[...], cbase_out.shape).astype(jnp.int32)
    ntile_out[...] = jnp.broadcast_to(n_col, ntile_out.shape).astype(jnp.int32)
    off_out[...] = jnp.broadcast_to(off_col, off_out.shape).astype(jnp.int32)
    carry_sc[...] = carry_sc[...] + n_col
    cnt_out[...] = jnp.broadcast_to(carry_sc[...], cnt_out.shape).astype(jnp.int32)


def _post(x2d, y_diff, cb, u, mqn, sig, km, vm, conv_w, w_branch, w_out, g_ffn, w_rt, b_r, tri, ltri,
          seq):
    t = x2d.shape[0]
    tm = TM_POST
    n_tiles = t // tm
    row = lambda i: (i, 0)
    const = lambda i: (0, 0)
    table = jax.ShapeDtypeStruct((n_tiles * N_EXPERTS, LANES), jnp.int32)
    outs = [
        jax.ShapeDtypeStruct((t, D_MODEL), F32),
        jax.ShapeDtypeStruct((TOP_K * t * ROW_CHUNKS, LANES), F32),
        jax.ShapeDtypeStruct((TOP_K, t), jnp.int32),
        jax.ShapeDtypeStruct((TOP_K, t), F32),
        table, table, table,
        jax.ShapeDtypeStruct((N_EXPERTS, LANES), jnp.int32),
    ]
    return pl.pallas_call(
        functools.partial(_post_kernel, seq=seq),
        grid=(t // tm,),
        in_specs=[
            pl.BlockSpec((tm, D_MODEL), row),
            pl.BlockSpec((tm, SEC), row),
            pl.BlockSpec((tm, SEC), row),
            pl.BlockSpec((tm, SEC), row),
            pl.BlockSpec((8, SEC), lambda i: (jnp.maximum(i * (tm // 8) - 1, 0), 0)),
            pl.BlockSpec((tm, SEC), row),
            pl.BlockSpec((tm, 3 * D_MODEL), row),
            pl.BlockSpec((N_MEM, SEC), lambda i: ((i * tm) // seq, 0)),
            pl.BlockSpec((N_MEM, SEC), lambda i: ((i * tm) // seq, 0)),
            pl.BlockSpec((3, SEC), const),
            pl.BlockSpec((3, SEC, D_MODEL), lambda i: (0, 0, 0)),
            pl.BlockSpec((D_MODEL, D_MODEL), const),
            pl.BlockSpec((1, D_MODEL), const),
            pl.BlockSpec((N_EXPERTS, D_MODEL), const),
            pl.BlockSpec((N_EXPERTS, 1), const),
            pl.BlockSpec((tm, tm), const),
            pl.BlockSpec((N_EXPERTS, N_EXPERTS), const),
        ],
        out_specs=[
            pl.BlockSpec((tm, D_MODEL), row),
            pl.BlockSpec((TOP_K * tm * ROW_CHUNKS, LANES), row),
            pl.BlockSpec((TOP_K, tm), lambda i: (0, i)),
            pl.BlockSpec((TOP_K, tm), lambda i: (0, i)),
            pl.BlockSpec((N_EXPERTS, LANES), row),
            pl.BlockSpec((N_EXPERTS, LANES), row),
            pl.BlockSpec((N_EXPERTS, LANES), row),
            pl.BlockSpec((N_EXPERTS, LANES), const),
        ],
        out_shape=outs,
        scratch_shapes=[pltpu.VMEM((N_EXPERTS, 1), F32)],
        compiler_params=pltpu.CompilerParams(dimension_semantics=("arbitrary",),
                                             vmem_limit_bytes=VMEM_LIMIT),
        name="post",
    )(x2d, y_diff, cb, u, u, mqn, sig, km, vm, conv_w, w_branch, w_out, g_ffn, w_rt, b_r, tri, ltri)


def _row_span(ref, row, n):
    return ref.at[pl.ds(pl.multiple_of(row * ROW_CHUNKS, ROW_CHUNKS), n * ROW_CHUNKS), :]


def _chunk_rows(row0, n, c):
    return pl.ds(row0 * ROW_CHUNKS + c, n, stride=ROW_CHUNKS)


def _load_rows(ref, row0, n):
    return jnp.concatenate([ref[_chunk_rows(row0, n, c), :] for c in range(ROW_CHUNKS)], axis=1)


def _store_rows(ref, n, val):
    for c in range(ROW_CHUNKS):
        ref[_chunk_rows(0, n, c), :] = val[:, c * LANES:(c + 1) * LANES]


def _copy_run(src_hbm, buf, sem, src_row, dst_row, n):
    piece = MAX_RUN
    while piece >= 1:
        take = (n & piece) != 0

        @pl.when(take)
        def _(piece=piece, src_row=src_row, dst_row=dst_row):
            pltpu.make_async_copy(_row_span(src_hbm, src_row, piece), _row_span(buf, dst_row, piece),
                                  sem).start()

        step = jnp.where(take, piece, 0)
        src_row = src_row + step
        dst_row = dst_row + step
        piece //= 2


def _wait_rows(n, src_hbm, buf, sem, base):
    pltpu.make_async_copy(_row_span(src_hbm, 0, n), _row_span(buf, base, n), sem).wait()


def _experts_kernel(be_ref, t0_ref, na_ref, ps_ref, cnt_ref, cb_ref, nt_ref, of_ref,
                    rows_hbm, zeros_hbm, wgu_ref, bgu_ref, wd_ref, bd_ref,
                    ys_out, xbuf, sem, *, n_tiles, slots_per_tile):
    b = pl.program_id(0)
    n_active = na_ref[0]
    slot = b % 2

    def issue(blk, buf_slot):
        e = be_ref[blk]
        r0 = blk * MOE_BLOCK - ps_ref[e]
        r1 = r0 + MOE_BLOCK
        base = buf_slot * MOE_BLOCK
        sm = sem.at[buf_slot]

        def cond(i):
            return jnp.logical_and(i < n_tiles, cb_ref[e * n_tiles + jnp.minimum(i, n_tiles - 1)] < r1)

        def body(i):
            c = cb_ref[e * n_tiles + i]
            lo = jnp.maximum(c, r0)
            hi = jnp.minimum(c + nt_ref[e * n_tiles + i], r1)
            src = i * slots_per_tile + of_ref[e * n_tiles + i] + (lo - c)
            _copy_run(rows_hbm, xbuf, sm, src, base + (lo - r0), jnp.maximum(hi - lo, 0))
            return i + 1

        lax.while_loop(cond, body, t0_ref[blk])
        valid = jnp.clip(cnt_ref[e] - r0, 0, MOE_BLOCK)
        _copy_run(zeros_hbm, xbuf, sm, 0, base + valid, MOE_BLOCK - valid)

    @pl.when(b == 0)
    def _():
        issue(b, 0)

    @pl.when(b + 1 < n_active)
    def _():
        issue(b + 1, 1 - slot)

    @pl.when(b < n_active)
    def _():
        base = pl.multiple_of(slot * MOE_BLOCK, MOE_BLOCK)
        _wait_rows(MOE_BLOCK, rows_hbm, xbuf, sem.at[slot], base)
        x = _load_rows(xbuf, base, MOE_BLOCK).astype(BF16)
        d_ff = wd_ref.shape[1]

        def gate_up(j):
            gs = slice(j * FF_CHUNK, (j + 1) * FF_CHUNK)
            us = slice(d_ff + j * FF_CHUNK, d_ff + (j + 1) * FF_CHUNK)
            return (jnp.dot(x, wgu_ref[0, :, gs], preferred_element_type=F32) + bgu_ref[0, :, gs],
                    jnp.dot(x, wgu_ref[0, :, us], preferred_element_type=F32) + bgu_ref[0, :, us])

        acts = []
        nxt = gate_up(0)
        for j in range(d_ff // FF_CHUNK):
            g, u = nxt
            if (j + 1) * FF_CHUNK < d_ff:
                nxt = gate_up(j + 1)
            g = jnp.minimum(g, SWIGLU_LIMIT)
            u = jnp.clip(u, -SWIGLU_LIMIT, SWIGLU_LIMIT)
            acts.append(((u + 1.0) * (g * (1.0 / (1.0 + jnp.exp(-SWIGLU_ALPHA * g))))).astype(BF16))
        act = jnp.concatenate(acts, axis=1)
        for n in range(D_MODEL // FF_CHUNK):
            cols = slice(n * FF_CHUNK, (n + 1) * FF_CHUNK)
            yn = jnp.dot(act, wd_ref[0, :, cols], preferred_element_type=F32) + bd_ref[0, :, cols]
            for c in range(FF_CHUNK // LANES):
                ys_out[_chunk_rows(0, MOE_BLOCK, n * (FF_CHUNK // LANES) + c), :] = (
                    yn[:, c * LANES:(c + 1) * LANES])

    @pl.when(b >= n_active)
    def _():
        ys_out[...] = jnp.zeros(ys_out.shape, F32)


def _experts(block_expert, first_tile, n_active, pstart, counts, cbase_e, ntile_e, off_e,
             rows, zero_rows, wgu, bgu, wd, bd, n_tiles):
    n_blocks = block_expert.shape[0]
    ff2 = wgu.shape[2]
    by_expert = lambda b, be, *_: (be[b], 0, 0)
    grid_spec = pltpu.PrefetchScalarGridSpec(
        num_scalar_prefetch=8,
        grid=(n_blocks,),
        in_specs=[
            pl.BlockSpec(memory_space=pl.ANY),
            pl.BlockSpec(memory_space=pl.ANY),
            pl.BlockSpec((1, D_MODEL, ff2), by_expert),
            pl.BlockSpec((1, 1, ff2), by_expert),
            pl.BlockSpec((1, ff2 // 2, D_MODEL), by_expert),
            pl.BlockSpec((1, 1, D_MODEL), by_expert),
        ],
        out_specs=pl.BlockSpec((MOE_BLOCK * ROW_CHUNKS, LANES), lambda b, *_: (b, 0)),
        scratch_shapes=[
            pltpu.VMEM((2 * MOE_BLOCK * ROW_CHUNKS, LANES), F32),
            pltpu.SemaphoreType.DMA((2,)),
        ],
    )
    return pl.pallas_call(
        functools.partial(_experts_kernel, n_tiles=n_tiles, slots_per_tile=TOP_K * TM_POST),
        grid_spec=grid_spec,
        out_shape=jax.ShapeDtypeStruct((n_blocks * MOE_BLOCK * ROW_CHUNKS, LANES), F32),
        compiler_params=pltpu.CompilerParams(dimension_semantics=("arbitrary",),
                                             vmem_limit_bytes=VMEM_LIMIT),
        name="experts",
    )(block_expert, first_tile, n_active, pstart, counts, cbase_e, ntile_e, off_e,
      rows, zero_rows, wgu, bgu, wd, bd)


def _combine_kernel(ps_ref, cb_ref, nt_ref, of_ref, slot_ref, gate_ref, x1_ref, ys_hbm, o_ref, buf,
                    sem):
    i = pl.program_id(0)
    n = pl.num_programs(0)
    tm = x1_ref.shape[0]
    n_slots = TOP_K * tm
    cur = i % 2

    def issue(tile, buf_slot):
        def body(e, carry):
            k = tile * N_EXPERTS + e
            _copy_run(ys_hbm, buf, sem.at[buf_slot], ps_ref[e] + cb_ref[k],
                      buf_slot * n_slots + of_ref[k], nt_ref[k])
            return carry
        lax.fori_loop(0, N_EXPERTS, body, 0)

    @pl.when(i == 0)
    def _():
        issue(i, 0)

    @pl.when(i + 1 < n)
    def _():
        issue(i + 1, 1 - cur)

    base = pl.multiple_of(cur * n_slots, n_slots)
    _wait_rows(n_slots, ys_hbm, buf, sem.at[cur], base)
    y = _load_rows(buf, base, n_slots).astype(BF16)
    jdx = lax.broadcasted_iota(jnp.int32, (n_slots, tm), 0)
    pick = jnp.where(jdx == slot_ref[0:1, :], gate_ref[0:1, :], jnp.where(
        jdx == slot_ref[1:2, :], gate_ref[1:2, :], jnp.where(
            jdx == slot_ref[2:3, :], gate_ref[2:3, :], jnp.where(
                jdx == slot_ref[3:4, :], gate_ref[3:4, :], 0.0)))).astype(BF16)
    o_ref[...] = x1_ref[...] + lax.dot_general(pick, y, (((0,), (0,)), ((), ())),
                                               preferred_element_type=F32)


def _combine(pstart, cbase_t, ntile_t, off_t, slots, gates, x1, ys_rows):
    t = x1.shape[0]
    tm = TM_POST
    n_slots = TOP_K * tm
    grid_spec = pltpu.PrefetchScalarGridSpec(
        num_scalar_prefetch=4,
        grid=(t // tm,),
        in_specs=[
            pl.BlockSpec((TOP_K, tm), lambda i, *_: (0, i)),
            pl.BlockSpec((TOP_K, tm), lambda i, *_: (0, i)),
            pl.BlockSpec((tm, D_MODEL), lambda i, *_: (i, 0)),
            pl.BlockSpec(memory_space=pl.ANY),
        ],
        out_specs=pl.BlockSpec((tm, D_MODEL), lambda i, *_: (i, 0)),
        scratch_shapes=[
            pltpu.VMEM((2 * n_slots * ROW_CHUNKS, LANES), F32),
            pltpu.SemaphoreType.DMA((2,)),
        ],
    )
    return pl.pallas_call(
        _combine_kernel,
        grid_spec=grid_spec,
        out_shape=jax.ShapeDtypeStruct((t, D_MODEL), F32),
        compiler_params=pltpu.CompilerParams(dimension_semantics=("arbitrary",),
                                             vmem_limit_bytes=VMEM_LIMIT),
        name="combine",
    )(pstart, cbase_t, ntile_t, off_t, slots, gates, x1, ys_rows)


def kernel(x, mem, g_mix, w_in, g_q_diff, g_k_diff, lambda_q1, lambda_k1, lambda_q2, lambda_k2,
           g_subln, conv_w, g_mem, w_mem_kv, g_q_mem, g_k_mem, w_branch, w_out, g_ffn, w_router,
           b_router, w_gate_up, b_gate_up, w_down, b_down):
    batch, seq, d = x.shape
    t = batch * seq
    depth = g_mix.shape[0]
    assert depth == 1 and d == D_MODEL and seq % TQ == 0 and t % TM_PROJ == 0

    pos = jnp.arange(seq, dtype=F32)
    inv_freq = 1.0 / (ROPE_THETA ** (jnp.arange(0, DIFF_HEAD_DIM, 2, dtype=F32) / DIFF_HEAD_DIM))
    ang = pos[:, None] * inv_freq[None, :]
    cos_t = jnp.tile(jnp.cos(ang), (1, 4))
    sin_t = jnp.tile(jnp.concatenate([-jnp.sin(ang), jnp.sin(ang)], axis=1), (1, 2))
    grp = jnp.arange(SEC) // DIFF_HEAD_DIM
    bd = jnp.where(grp[:, None] == grp[None, :], 1.0 / DIFF_HEAD_DIM, 0.0).astype(BF16)
    tok = jnp.arange(TM_POST)
    tri = (tok[:, None] < tok[None, :]).astype(BF16)

    x2d = x.reshape(t, d)
    l = 0
    lam_init = 0.8 - 0.6 * math.exp(-0.3 * l)
    km, vm = _mem_kv(mem.reshape(batch * N_MEM, d), g_mem[l][None, :], w_mem_kv[l].astype(BF16),
                     g_k_mem[l][None, :])
    qn, kn, vv, cb, u, mqn, sig = _in_proj(
        x2d, g_mix[l][None, :], w_in[l].astype(BF16), cos_t, sin_t, bd,
        jnp.tile(g_q_diff[l], SEC // DIFF_HEAD_DIM)[None, :],
        jnp.tile(g_k_diff[l], SEC // DIFF_HEAD_DIM)[None, :],
        g_q_mem[l][None, :], seq)
    lam_rows = jnp.stack([lambda_q1[l], lambda_k1[l], lambda_q2[l], lambda_k2[l]]).astype(F32)
    vt = vv.reshape(batch, seq // TQ, TQ, DIFF_HEADS, DIFF_V_DIM).transpose(0, 3, 1, 4, 2)
    y_diff = _diff_attn(qn, kn, vt, lam_rows, g_subln[l][None, :], batch, seq, lam_init)
    ex = jnp.arange(N_EXPERTS)
    ltri = (ex[None, :] < ex[:, None]).astype(F32)
    x1, rows, slots, gates, cbase, ntile, off, counts = _post(
        x2d, y_diff, cb, u, mqn, sig, km, vm, conv_w[l], w_branch[l].astype(BF16),
        w_out[l].astype(BF16), g_ffn[l][None, :], w_router[l].T, b_router[l][:, None], tri, ltri, seq)

    n_tiles = t // TM_POST
    n_assign = t * TOP_K
    n_rows = -(-(n_assign + N_EXPERTS * (MOE_BLOCK - 1)) // MOE_BLOCK) * MOE_BLOCK
    n_blocks = n_rows // MOE_BLOCK
    cnt = counts[:, 0]
    padded = (cnt + MOE_BLOCK - 1) // MOE_BLOCK * MOE_BLOCK
    pend = jnp.cumsum(padded)
    pstart = pend - padded
    block_row0 = jnp.arange(n_blocks, dtype=jnp.int32) * MOE_BLOCK
    block_expert = jnp.minimum(
        jnp.sum((pend[None, :] <= block_row0[:, None]).astype(jnp.int32), axis=1), N_EXPERTS - 1)
    n_active = (pend[-1:] // MOE_BLOCK).astype(jnp.int32)
    cbase_t = cbase[:, 0].reshape(n_tiles, N_EXPERTS)
    ntile_t = ntile[:, 0].reshape(n_tiles, N_EXPERTS)
    off_t = off[:, 0].reshape(n_tiles, N_EXPERTS)
    cend_b = (cbase_t + ntile_t).T[block_expert]
    r0 = block_row0 - pstart[block_expert]
    first_tile = jnp.sum((cend_b <= r0[:, None]).astype(jnp.int32), axis=1)
    first_tile = jnp.minimum(first_tile, n_tiles - 1).astype(jnp.int32)

    zero_rows = jnp.zeros((MOE_BLOCK * ROW_CHUNKS, LANES), F32)
    ys_rows = _experts(block_expert, first_tile, n_active, pstart, cnt,
                       cbase_t.T.reshape(-1), ntile_t.T.reshape(-1), off_t.T.reshape(-1),
                       rows, zero_rows, w_gate_up[l].astype(BF16), b_gate_up[l][:, None, :],
                       w_down[l].astype(BF16), b_down[l][:, None, :], n_tiles)
    out = _combine(pstart, cbase_t.reshape(-1), ntile_t.reshape(-1), off_t.reshape(-1), slots, gates,
                   x1, ys_rows)
    return out.reshape(batch, seq, d)
```

```python
import functools
import math

import jax
import jax.numpy as jnp
from jax import lax
from jax.experimental import pallas as pl
from jax.experimental.pallas import tpu as pltpu

F32 = jnp.float32
BF16 = jnp.bfloat16

D_MODEL = 1024
CHUNK = 64
EPS = 1e-6
ROPE_THETA = 10000.0
DIFF_HEADS = 4
DIFF_HEAD_DIM = 64
DIFF_V_DIM = 2 * DIFF_HEAD_DIM
N_MEM = 256
MEM_HEADS = 4
MEM_HEAD_DIM = 128
SEC = 512
N_SEC = 13
N_EXPERTS = 32
TOP_K = 4
SWIGLU_LIMIT = 7.0
SWIGLU_ALPHA = 1.702
MOE_BLOCK = 512
LANES = 128
ROW_CHUNKS = D_MODEL // LANES
LOG2E = 1.4426950408889634
NEG_BIG = -1e30

TM_PROJ = 256
TQ = 1024
ATTN_STRIP = 512
TM_POST = 256
MAX_RUN = max(MOE_BLOCK, TM_POST)
FF_CHUNK = 256
VMEM_LIMIT = 48 * 1024 * 1024
VMEM_LIMIT_EXPERTS = 56 * 1024 * 1024


def _rms(x, eps=EPS):
    return x * lax.rsqrt(jnp.mean(x * x, axis=-1, keepdims=True) + eps)


def _mem_kv_kernel(mem_ref, g_ref, w_ref, gk_ref, k_out, v_out):
    h = (_rms(mem_ref[...]) * g_ref[...]).astype(BF16)
    kv = jnp.dot(h, w_ref[...], preferred_element_type=F32)
    for hd in range(MEM_HEADS):
        sl = slice(hd * MEM_HEAD_DIM, (hd + 1) * MEM_HEAD_DIM)
        k_out[:, sl] = (_rms(kv[:, sl]) * gk_ref[...]).astype(BF16)
    v_out[...] = kv[:, SEC:].astype(BF16)


def _mem_kv(mem2d, g_mem, w_kv, g_k):
    rows = mem2d.shape[0]
    return pl.pallas_call(
        _mem_kv_kernel,
        grid=(rows // N_MEM,),
        in_specs=[
            pl.BlockSpec((N_MEM, D_MODEL), lambda i: (i, 0)),
            pl.BlockSpec((1, D_MODEL), lambda i: (0, 0)),
            pl.BlockSpec((D_MODEL, 2 * SEC), lambda i: (0, 0)),
            pl.BlockSpec((1, MEM_HEAD_DIM), lambda i: (0, 0)),
        ],
        out_specs=[pl.BlockSpec((N_MEM, SEC), lambda i: (i, 0))] * 2,
        out_shape=[jax.ShapeDtypeStruct((rows, SEC), BF16)] * 2,
        compiler_params=pltpu.CompilerParams(vmem_limit_bytes=VMEM_LIMIT),
        name="mem_kv",
    )(mem2d, g_mem, w_kv, g_k)


def _in_proj_kernel(x_ref, g_ref, w_ref, cos_ref, sin_ref, bd_ref, gq_ref, gk_ref, gqm_ref,
                    q_out, k_out, v_out, cb_out, u_out, mq_out, sig_out):
    tm = x_ref.shape[0]
    h = (_rms(x_ref[...]) * g_ref[...]).astype(BF16)

    def proj(sec):
        return jnp.dot(h, w_ref[:, sec * SEC:(sec + 1) * SEC], preferred_element_type=F32)

    cos = jnp.concatenate([cos_ref[...]] * (SEC // LANES), axis=1)
    sin = jnp.concatenate([sin_ref[...]] * (SEC // LANES), axis=1)
    lane = lax.broadcasted_iota(jnp.int32, (tm, SEC), 1)
    first_half = (lane & (DIFF_HEAD_DIM // 2)) == 0

    def norm_rope(a, g, scale):
        ms = jnp.dot((a * a).astype(BF16), bd_ref[...], preferred_element_type=F32)
        y = a * lax.rsqrt(ms + EPS) * g
        partner = jnp.where(first_half,
                            pltpu.roll(y, SEC - DIFF_HEAD_DIM // 2, 1),
                            pltpu.roll(y, DIFF_HEAD_DIM // 2, 1))
        return (y * cos + partner * sin) * scale

    q_out[...] = norm_rope(proj(0), gq_ref[...], DIFF_HEAD_DIM ** -0.5 * LOG2E).astype(BF16)
    k_out[...] = norm_rope(proj(1), gk_ref[...], 1.0).astype(BF16)
    v = proj(2)
    for hd in range(DIFF_HEADS):
        v_out[0, hd, 0] = v[:, hd * DIFF_V_DIM:(hd + 1) * DIFF_V_DIM].T.astype(BF16)
    cb_out[...] = proj(3).astype(BF16)
    u_out[...] = proj(4) * proj(5)
    mq = proj(6)
    for hd in range(MEM_HEADS):
        sl = slice(hd * MEM_HEAD_DIM, (hd + 1) * MEM_HEAD_DIM)
        mq_out[:, sl] = (_rms(mq[:, sl]) * gqm_ref[...]
                         * (MEM_HEAD_DIM ** -0.5 * LOG2E)).astype(BF16)
    for s in range(7, N_SEC):
        a = proj(s)
        sig_out[:, (s - 7) * SEC:(s - 6) * SEC] = (1.0 / (1.0 + jnp.exp(-a))).astype(BF16)


def _in_proj(x2d, g_mix, w_in, cos_t, sin_t, bd, gq, gk, gqm, seq):
    t = x2d.shape[0]
    tm = TM_PROJ
    tiles_per_seq = seq // tm
    tiles_per_key = TQ // tm
    row = lambda i: (i, 0)
    const = lambda i: (0, 0)
    outs = [
        jax.ShapeDtypeStruct((t, SEC), BF16),
        jax.ShapeDtypeStruct((t, SEC), BF16),
        jax.ShapeDtypeStruct((t // seq, DIFF_HEADS, seq // TQ, DIFF_V_DIM, TQ), BF16),
        jax.ShapeDtypeStruct((t, SEC), BF16),
        jax.ShapeDtypeStruct((t, SEC), F32),
        jax.ShapeDtypeStruct((t, SEC), BF16),
        jax.ShapeDtypeStruct((t, 3 * D_MODEL), BF16),
    ]
    return pl.pallas_call(
        _in_proj_kernel,
        grid=(t // tm,),
        in_specs=[
            pl.BlockSpec((tm, D_MODEL), row),
            pl.BlockSpec((1, D_MODEL), const),
            pl.BlockSpec((D_MODEL, N_SEC * SEC), const),
            pl.BlockSpec((tm, LANES), lambda i: (i % tiles_per_seq, 0)),
            pl.BlockSpec((tm, LANES), lambda i: (i % tiles_per_seq, 0)),
            pl.BlockSpec((SEC, SEC), const),
            pl.BlockSpec((1, SEC), const),
            pl.BlockSpec((1, SEC), const),
            pl.BlockSpec((1, MEM_HEAD_DIM), const),
        ],
        out_specs=[
            pl.BlockSpec((1, DIFF_HEADS, 1, DIFF_V_DIM, tm),
                         lambda i: (i // tiles_per_seq, 0, (i % tiles_per_seq) // tiles_per_key,
                                    0, i % tiles_per_key))
            if o.ndim == 5 else pl.BlockSpec((tm, o.shape[1]), row) for o in outs],
        out_shape=outs,
        compiler_params=pltpu.CompilerParams(vmem_limit_bytes=VMEM_LIMIT),
        name="in_proj",
    )(x2d, g_mix, w_in, cos_t, sin_t, bd, gq, gk, gqm)


def _diff_attn_kernel(q_ref, k_ref, vt_ref, lam_ref, gs_ref, o_ref, acc_sc, s0_sc, *, lam_init):
    i = pl.program_id(2)
    tq = q_ref.shape[0]
    q = q_ref[...]
    lane = lax.broadcasted_iota(jnp.int32, q.shape, 1)
    zero = jnp.zeros_like(q)
    qq = jnp.concatenate([jnp.where(lane < DIFF_HEAD_DIM, q, zero),
                          jnp.where(lane >= DIFF_HEAD_DIM, q, zero)], axis=0)
    acc_sc[...] = jnp.zeros(acc_sc.shape, F32)
    n_strips = 2 * tq // ATTN_STRIP

    def scores(j, c, nk):
        off = pl.multiple_of(j * tq, tq)
        return lax.dot_general(k_ref[pl.ds(off, nk), :], qq[c * ATTN_STRIP:(c + 1) * ATTN_STRIP, :],
                               (((1,), (1,)), ((), ())), preferred_element_type=F32)

    def step(j, ms, ls, masked):
        q_offs = [(c * ATTN_STRIP) % tq for c in range(n_strips)]
        nks = [min(tq, qo + ATTN_STRIP) if masked else tq for qo in q_offs]
        ms_new, ls_new = [], []
        s = s0_sc[0:nks[0], :]
        for c in range(n_strips):
            cols = slice(c * ATTN_STRIP, (c + 1) * ATTN_STRIP)
            if c + 1 < n_strips:
                s_next = scores(j, c + 1, nks[c + 1])
            elif not masked:
                s0_sc[...] = scores(j + 1, 0, tq)
            if masked:
                r = lax.broadcasted_iota(jnp.int32, s.shape, 0)
                col = lax.broadcasted_iota(jnp.int32, s.shape, 1)
                s = jnp.where((r // CHUNK) <= ((col + q_offs[c]) // CHUNK), s, NEG_BIG)
            m_new = jnp.maximum(ms[c], jnp.max(s, axis=0, keepdims=True))
            p = jnp.exp2(s - m_new)
            alpha = jnp.exp2(ms[c] - m_new)
            ls_new.append(alpha * ls[c] + jnp.sum(p, axis=0, keepdims=True))
            ms_new.append(m_new)
            acc_sc[:, cols] = alpha * acc_sc[:, cols] + jnp.dot(
                vt_ref[0, 0, j][:, :nks[c]], p.astype(BF16), preferred_element_type=F32)
            if c + 1 < n_strips:
                s = s_next
        return tuple(ms_new), tuple(ls_new)

    s0_sc[...] = scores(0, 0, tq)
    m0 = tuple(jnp.full((1, ATTN_STRIP), NEG_BIG, F32) for _ in range(n_strips))
    l0 = tuple(jnp.zeros((1, ATTN_STRIP), F32) for _ in range(n_strips))
    m, l = lax.fori_loop(0, i, lambda j, c: step(j, c[0], c[1], False), (m0, l0))
    m, l = step(i, m, l, True)

    ot = acc_sc[...] / jnp.concatenate(l, axis=1)
    lam = (jnp.exp(jnp.sum(lam_ref[0:1, :] * lam_ref[1:2, :], axis=-1, keepdims=True))
           - jnp.exp(jnp.sum(lam_ref[2:3, :] * lam_ref[3:4, :], axis=-1, keepdims=True))
           + lam_init)
    d = (ot[:, :tq] - lam * ot[:, tq:]).T
    o_ref[...] = (_rms(d) * gs_ref[...] * (1.0 - lam_init)).astype(BF16)


def _diff_attn(qn, kn, vt, lam_rows, g_subln, batch, seq, lam_init):
    t = qn.shape[0]
    nq = seq // TQ
    return pl.pallas_call(
        functools.partial(_diff_attn_kernel, lam_init=lam_init),
        grid=(batch, DIFF_HEADS, nq),
        in_specs=[
            pl.BlockSpec((TQ, DIFF_V_DIM), lambda b, h, i: (b * nq + i, h)),
            pl.BlockSpec((seq, DIFF_V_DIM), lambda b, h, i: (b, h)),
            pl.BlockSpec((1, 1, nq, DIFF_V_DIM, TQ), lambda b, h, i: (b, h, 0, 0, 0)),
            pl.BlockSpec((4, DIFF_HEAD_DIM), lambda b, h, i: (0, 0)),
            pl.BlockSpec((1, DIFF_V_DIM), lambda b, h, i: (0, 0)),
        ],
        out_specs=pl.BlockSpec((TQ, DIFF_V_DIM), lambda b, h, i: (b * nq + i, h)),
        out_shape=jax.ShapeDtypeStruct((t, SEC), BF16),
        scratch_shapes=[pltpu.VMEM((DIFF_V_DIM, 2 * TQ), F32), pltpu.VMEM((TQ, ATTN_STRIP), F32)],
        compiler_params=pltpu.CompilerParams(vmem_limit_bytes=VMEM_LIMIT),
        name="diff_attn",
    )(qn, kn, vt, lam_rows, g_subln)


def _post_kernel(x_ref, yd_ref, cb_ref, u_ref, up_ref, mq_ref, sig_ref, km_ref, vm_ref,
                 cw_ref, wb_ref, wo_ref, gf_ref, wr_ref, br_ref, tri_ref, ltri_ref,
                 x1_out, rows_out, slot_out, gate_out, cbase_out, ntile_out, off_out, cnt_out, carry_sc,
                 *, seq):
    i = pl.program_id(0)
    tm = x_ref.shape[0]

    @pl.when(i == 0)
    def _():
        carry_sc[...] = jnp.zeros(carry_sc.shape, F32)

    u = u_ref[...]
    seq_start = (i * tm) % seq == 0
    up = jnp.where(seq_start, 0.0, up_ref[...])
    r = lax.broadcasted_iota(jnp.int32, u.shape, 0)
    u1 = jnp.where(r == 0, up[7:8, :], pltpu.roll(u, 1, 0))
    u2 = jnp.where(r == 0, up[6:7, :], jnp.where(r == 1, up[7:8, :], pltpu.roll(u, 2, 0)))
    y_conv = cb_ref[...].astype(F32) * (cw_ref[0:1, :] * u2 + cw_ref[1:2, :] * u1 + cw_ref[2:3, :] * u)

    y_mem = []
    for hd in range(MEM_HEADS):
        sl = slice(hd * MEM_HEAD_DIM, (hd + 1) * MEM_HEAD_DIM)
        s = lax.dot_general(mq_ref[:, sl], km_ref[:, sl], (((1,), (1,)), ((), ())),
                            preferred_element_type=F32)
        p = jnp.exp2(s - jnp.max(s, axis=-1, keepdims=True))
        o = jnp.dot(p.astype(BF16), vm_ref[:, sl], preferred_element_type=F32)
        y_mem.append(o / jnp.sum(p, axis=-1, keepdims=True))

    merged = (sig_ref[:, 0:D_MODEL].astype(F32)
              * jnp.dot(yd_ref[...], wb_ref[0], preferred_element_type=F32))
    merged += (sig_ref[:, D_MODEL:2 * D_MODEL].astype(F32)
               * jnp.dot(y_conv.astype(BF16), wb_ref[1], preferred_element_type=F32))
    merged += (sig_ref[:, 2 * D_MODEL:3 * D_MODEL].astype(F32)
               * jnp.dot(jnp.concatenate(y_mem, axis=1).astype(BF16), wb_ref[2],
                         preferred_element_type=F32))
    x1 = x_ref[...] + jnp.dot(merged.astype(BF16), wo_ref[...], preferred_element_type=F32)
    x1_out[...] = x1

    h2 = _rms(x1) * gf_ref[...]

    nt = (((1,), (1,)), ((), ()))
    w = wr_ref[...]
    w_hi = w.astype(BF16)
    w_lo = (w - w_hi.astype(F32)).astype(BF16)
    h_hi = h2.astype(BF16)
    h_lo = (h2 - h_hi.astype(F32)).astype(BF16)
    part = lax.dot_general(jnp.concatenate([w_hi, w_lo], axis=0), h_hi, nt,
                           preferred_element_type=F32)
    logits = (part[:N_EXPERTS] + part[N_EXPERTS:]
              + lax.dot_general(w_hi, h_lo, nt, preferred_element_type=F32) + br_ref[...])
    eio = lax.broadcasted_iota(jnp.int32, logits.shape, 0)
    work = logits
    vals, hots = [], []
    for k in range(TOP_K):
        mk = jnp.max(work, axis=0, keepdims=True)
        ik = jnp.min(jnp.where(work == mk, eio, N_EXPERTS), axis=0, keepdims=True)
        hot = eio == ik
        work = jnp.where(hot, -jnp.inf, work)
        vals.append(mk)
        hots.append(hot)
    ex = [jnp.exp(v - vals[0]) for v in vals]
    den = ex[0] + ex[1] + ex[2] + ex[3]
    gates = [e / den for e in ex]

    assign = jnp.zeros(logits.shape, F32)
    for hot in hots:
        assign = jnp.where(hot, 1.0, assign)
    earlier = jnp.dot(assign.astype(BF16), tri_ref[...], preferred_element_type=F32)
    n_col = jnp.sum(assign, axis=1, keepdims=True)
    off_col = jnp.dot(ltri_ref[...], jnp.broadcast_to(n_col, (N_EXPERTS, LANES)),
                      precision=lax.Precision.HIGHEST, preferred_element_type=F32)[:, 0:1]
    slots = [jnp.sum(jnp.where(hot, earlier + off_col, 0.0), axis=0, keepdims=True).astype(jnp.int32)
             for hot in hots]
    for k in range(TOP_K):
        slot_out[k:k + 1, :] = slots[k]
        gate_out[k:k + 1, :] = gates[k]

    n_slots = TOP_K * tm
    jdx = lax.broadcasted_iota(jnp.int32, (n_slots, tm), 0)
    pick = jnp.where(jdx == slots[0], 1.0, jnp.where(jdx == slots[1], 1.0, jnp.where(
        jdx == slots[2], 1.0, jnp.where(jdx == slots[3], 1.0, 0.0)))).astype(BF16)
    rows = jnp.dot(pick, h_hi, preferred_element_type=F32)
    _store_rows(rows_out, n_slots, rows)

    cbase_out[...] = jnp.broadcast_to(carry_sc[...], cbase_out.shape).astype(jnp.int32)
    ntile_out[...] = jnp.broadcast_to(n_col, ntile_out.shape).astype(jnp.int32)
    off_out[...] = jnp.broadcast_to(off_col, off_out.shape).astype(jnp.int32)
    carry_sc[...] = carry_sc[...] + n_col
    cnt_out[...] = jnp.broadcast_to(carry_sc[...], cnt_out.shape).astype(jnp.int32)


def _post(x2d, y_diff, cb, u, mqn, sig, km, vm, conv_w, w_branch, w_out, g_ffn, w_rt, b_r, tri, ltri,
          seq):
    t = x2d.shape[0]
    tm = TM_POST
    n_tiles = t // tm
    row = lambda i: (i, 0)
    const = lambda i: (0, 0)
    table = jax.ShapeDtypeStruct((n_tiles * N_EXPERTS, LANES), jnp.int32)
    outs = [
        jax.ShapeDtypeStruct((t, D_MODEL), F32),
        jax.ShapeDtypeStruct((TOP_K * t * ROW_CHUNKS, LANES), F32),
        jax.ShapeDtypeStruct((TOP_K, t), jnp.int32),
        jax.ShapeDtypeStruct((TOP_K, t), F32),
        table, table, table,
        jax.ShapeDtypeStruct((N_EXPERTS, LANES), jnp.int32),
    ]
    return pl.pallas_call(
        functools.partial(_post_kernel, seq=seq),
        grid=(t // tm,),
        in_specs=[
            pl.BlockSpec((tm, D_MODEL), row),
            pl.BlockSpec((tm, SEC), row),
            pl.BlockSpec((tm, SEC), row),
            pl.BlockSpec((tm, SEC), row),
            pl.BlockSpec((8, SEC), lambda i: (jnp.maximum(i * (tm // 8) - 1, 0), 0)),
            pl.BlockSpec((tm, SEC), row),
            pl.BlockSpec((tm, 3 * D_MODEL), row),
            pl.BlockSpec((N_MEM, SEC), lambda i: ((i * tm) // seq, 0)),
            pl.BlockSpec((N_MEM, SEC), lambda i: ((i * tm) // seq, 0)),
            pl.BlockSpec((3, SEC), const),
            pl.BlockSpec((3, SEC, D_MODEL), lambda i: (0, 0, 0)),
            pl.BlockSpec((D_MODEL, D_MODEL), const),
            pl.BlockSpec((1, D_MODEL), const),
            pl.BlockSpec((N_EXPERTS, D_MODEL), const),
            pl.BlockSpec((N_EXPERTS, 1), const),
            pl.BlockSpec((tm, tm), const),
            pl.BlockSpec((N_EXPERTS, N_EXPERTS), const),
        ],
        out_specs=[
            pl.BlockSpec((tm, D_MODEL), row),
            pl.BlockSpec((TOP_K * tm * ROW_CHUNKS, LANES), row),
            pl.BlockSpec((TOP_K, tm), lambda i: (0, i)),
            pl.BlockSpec((TOP_K, tm), lambda i: (0, i)),
            pl.BlockSpec((N_EXPERTS, LANES), row),
            pl.BlockSpec((N_EXPERTS, LANES), row),
            pl.BlockSpec((N_EXPERTS, LANES), row),
            pl.BlockSpec((N_EXPERTS, LANES), const),
        ],
        out_shape=outs,
        scratch_shapes=[pltpu.VMEM((N_EXPERTS, 1), F32)],
        compiler_params=pltpu.CompilerParams(dimension_semantics=("arbitrary",),
                                             vmem_limit_bytes=VMEM_LIMIT),
        name="post",
    )(x2d, y_diff, cb, u, u, mqn, sig, km, vm, conv_w, w_branch, w_out, g_ffn, w_rt, b_r, tri, ltri)


def _row_span(ref, row, n):
    return ref.at[pl.ds(pl.multiple_of(row * ROW_CHUNKS, ROW_CHUNKS), n * ROW_CHUNKS), :]


def _chunk_rows(row0, n, c):
    return pl.ds(row0 * ROW_CHUNKS + c, n, stride=ROW_CHUNKS)


def _load_rows(ref, row0, n):
    return jnp.concatenate([ref[_chunk_rows(row0, n, c), :] for c in range(ROW_CHUNKS)], axis=1)


def _store_rows(ref, n, val):
    for c in range(ROW_CHUNKS):
        ref[_chunk_rows(0, n, c), :] = val[:, c * LANES:(c + 1) * LANES]


def _copy_run(src_hbm, buf, sem, src_row, dst_row, n):
    piece = MAX_RUN
    while piece >= 1:
        take = (n & piece) != 0

        @pl.when(take)
        def _(piece=piece, src_row=src_row, dst_row=dst_row):
            pltpu.make_async_copy(_row_span(src_hbm, src_row, piece), _row_span(buf, dst_row, piece),
                                  sem).start()

        step = jnp.where(take, piece, 0)
        src_row = src_row + step
        dst_row = dst_row + step
        piece //= 2


def _wait_rows(n, src_hbm, buf, sem, base):
    pltpu.make_async_copy(_row_span(src_hbm, 0, n), _row_span(buf, base, n), sem).wait()


def _experts_kernel(be_ref, t0_ref, na_ref, ps_ref, cnt_ref, cb_ref, nt_ref, of_ref,
                    rows_hbm, zeros_hbm, wgu_f32, bgu_ref, wd_f32, bd_ref,
                    ys_out, xbuf, sem, wgu_ref, wd_ref, *, n_tiles, slots_per_tile):
    b = pl.program_id(0)
    n_active = na_ref[0]
    slot = b % 2

    @pl.when(jnp.logical_and(b < n_active,
                             jnp.logical_or(b == 0, be_ref[b] != be_ref[jnp.maximum(b - 1, 0)])))
    def _():
        wgu_ref[...] = wgu_f32[0].astype(BF16)
        wd_ref[...] = wd_f32[0].astype(BF16)

    def issue(blk, buf_slot):
        e = be_ref[blk]
        r0 = blk * MOE_BLOCK - ps_ref[e]
        r1 = r0 + MOE_BLOCK
        base = buf_slot * MOE_BLOCK
        sm = sem.at[buf_slot]

        def cond(i):
            return jnp.logical_and(i < n_tiles, cb_ref[e * n_tiles + jnp.minimum(i, n_tiles - 1)] < r1)

        def body(i):
            c = cb_ref[e * n_tiles + i]
            lo = jnp.maximum(c, r0)
            hi = jnp.minimum(c + nt_ref[e * n_tiles + i], r1)
            src = i * slots_per_tile + of_ref[e * n_tiles + i] + (lo - c)
            _copy_run(rows_hbm, xbuf, sm, src, base + (lo - r0), jnp.maximum(hi - lo, 0))
            return i + 1

        lax.while_loop(cond, body, t0_ref[blk])
        valid = jnp.clip(cnt_ref[e] - r0, 0, MOE_BLOCK)
        _copy_run(zeros_hbm, xbuf, sm, 0, base + valid, MOE_BLOCK - valid)

    @pl.when(b == 0)
    def _():
        issue(b, 0)

    @pl.when(b + 1 < n_active)
    def _():
        issue(b + 1, 1 - slot)

    @pl.when(b < n_active)
    def _():
        base = pl.multiple_of(slot * MOE_BLOCK, MOE_BLOCK)
        _wait_rows(MOE_BLOCK, rows_hbm, xbuf, sem.at[slot], base)
        x = _load_rows(xbuf, base, MOE_BLOCK).astype(BF16)
        d_ff = wd_ref.shape[0]

        def gate_up(j):
            gs = slice(j * FF_CHUNK, (j + 1) * FF_CHUNK)
            us = slice(d_ff + j * FF_CHUNK, d_ff + (j + 1) * FF_CHUNK)
            return (jnp.dot(x, wgu_ref[:, gs], preferred_element_type=F32) + bgu_ref[0, :, gs],
                    jnp.dot(x, wgu_ref[:, us], preferred_element_type=F32) + bgu_ref[0, :, us])

        acts = []
        nxt = gate_up(0)
        for j in range(d_ff // FF_CHUNK):
            g, u = nxt
            if (j + 1) * FF_CHUNK < d_ff:
                nxt = gate_up(j + 1)
            g = jnp.minimum(g, SWIGLU_LIMIT)
            u = jnp.clip(u, -SWIGLU_LIMIT, SWIGLU_LIMIT)
            acts.append(((u + 1.0) * (g * (1.0 / (1.0 + jnp.exp(-SWIGLU_ALPHA * g))))).astype(BF16))
        act = jnp.concatenate(acts, axis=1)
        for n in range(D_MODEL // FF_CHUNK):
            cols = slice(n * FF_CHUNK, (n + 1) * FF_CHUNK)
            yn = jnp.dot(act, wd_ref[:, cols], preferred_element_type=F32) + bd_ref[0, :, cols]
            for c in range(FF_CHUNK // LANES):
                ys_out[_chunk_rows(0, MOE_BLOCK, n * (FF_CHUNK // LANES) + c), :] = (
                    yn[:, c * LANES:(c + 1) * LANES])

    @pl.when(b >= n_active)
    def _():
        ys_out[...] = jnp.zeros(ys_out.shape, F32)


def _experts(block_expert, first_tile, n_active, pstart, counts, cbase_e, ntile_e, off_e,
             rows, zero_rows, wgu, bgu, wd, bd, n_tiles):
    n_blocks = block_expert.shape[0]
    ff2 = wgu.shape[2]
    by_expert = lambda b, be, *_: (be[b], 0, 0)
    grid_spec = pltpu.PrefetchScalarGridSpec(
        num_scalar_prefetch=8,
        grid=(n_blocks,),
        in_specs=[
            pl.BlockSpec(memory_space=pl.ANY),
            pl.BlockSpec(memory_space=pl.ANY),
            pl.BlockSpec((1, D_MODEL, ff2), by_expert),
            pl.BlockSpec((1, 1, ff2), by_expert),
            pl.BlockSpec((1, ff2 // 2, D_MODEL), by_expert),
            pl.BlockSpec((1, 1, D_MODEL), by_expert),
        ],
        out_specs=pl.BlockSpec((MOE_BLOCK * ROW_CHUNKS, LANES), lambda b, *_: (b, 0)),
        scratch_shapes=[
            pltpu.VMEM((2 * MOE_BLOCK * ROW_CHUNKS, LANES), F32),
            pltpu.SemaphoreType.DMA((2,)),
            pltpu.VMEM((D_MODEL, ff2), BF16),
            pltpu.VMEM((ff2 // 2, D_MODEL), BF16),
        ],
    )
    return pl.pallas_call(
        functools.partial(_experts_kernel, n_tiles=n_tiles, slots_per_tile=TOP_K * TM_POST),
        grid_spec=grid_spec,
        out_shape=jax.ShapeDtypeStruct((n_blocks * MOE_BLOCK * ROW_CHUNKS, LANES), F32),
        compiler_params=pltpu.CompilerParams(dimension_semantics=("arbitrary",),
                                             vmem_limit_bytes=VMEM_LIMIT_EXPERTS),
        name="experts",
    )(block_expert, first_tile, n_active, pstart, counts, cbase_e, ntile_e, off_e,
      rows, zero_rows, wgu, bgu, wd, bd)


def _combine_kernel(ps_ref, cb_ref, nt_ref, of_ref, slot_ref, gate_ref, x1_ref, ys_hbm, o_ref, buf,
                    sem):
    i = pl.program_id(0)
    n = pl.num_programs(0)
    tm = x1_ref.shape[0]
    n_slots = TOP_K * tm
    cur = i % 2

    def issue(tile, buf_slot):
        def body(e, carry):
            k = tile * N_EXPERTS + e
            _copy_run(ys_hbm, buf, sem.at[buf_slot], ps_ref[e] + cb_ref[k],
                      buf_slot * n_slots + of_ref[k], nt_ref[k])
            return carry
        lax.fori_loop(0, N_EXPERTS, body, 0)

    @pl.when(i == 0)
    def _():
        issue(i, 0)

    @pl.when(i + 1 < n)
    def _():
        issue(i + 1, 1 - cur)

    base = pl.multiple_of(cur * n_slots, n_slots)
    _wait_rows(n_slots, ys_hbm, buf, sem.at[cur], base)
    y = _load_rows(buf, base, n_slots).astype(BF16)
    jdx = lax.broadcasted_iota(jnp.int32, (n_slots, tm), 0)
    pick = jnp.where(jdx == slot_ref[0:1, :], gate_ref[0:1, :], jnp.where(
        jdx == slot_ref[1:2, :], gate_ref[1:2, :], jnp.where(
            jdx == slot_ref[2:3, :], gate_ref[2:3, :], jnp.where(
                jdx == slot_ref[3:4, :], gate_ref[3:4, :], 0.0)))).astype(BF16)
    o_ref[...] = x1_ref[...] + lax.dot_general(pick, y, (((0,), (0,)), ((), ())),
                                               preferred_element_type=F32)


def _combine(pstart, cbase_t, ntile_t, off_t, slots, gates, x1, ys_rows):
    t = x1.shape[0]
    tm = TM_POST
    n_slots = TOP_K * tm
    grid_spec = pltpu.PrefetchScalarGridSpec(
        num_scalar_prefetch=4,
        grid=(t // tm,),
        in_specs=[
            pl.BlockSpec((TOP_K, tm), lambda i, *_: (0, i)),
            pl.BlockSpec((TOP_K, tm), lambda i, *_: (0, i)),
            pl.BlockSpec((tm, D_MODEL), lambda i, *_: (i, 0)),
            pl.BlockSpec(memory_space=pl.ANY),
        ],
        out_specs=pl.BlockSpec((tm, D_MODEL), lambda i, *_: (i, 0)),
        scratch_shapes=[
            pltpu.VMEM((2 * n_slots * ROW_CHUNKS, LANES), F32),
            pltpu.SemaphoreType.DMA((2,)),
        ],
    )
    return pl.pallas_call(
        _combine_kernel,
        grid_spec=grid_spec,
        out_shape=jax.ShapeDtypeStruct((t, D_MODEL), F32),
        compiler_params=pltpu.CompilerParams(dimension_semantics=("arbitrary",),
                                             vmem_limit_bytes=VMEM_LIMIT),
        name="combine",
    )(pstart, cbase_t, ntile_t, off_t, slots, gates, x1, ys_rows)


def kernel(x, mem, g_mix, w_in, g_q_diff, g_k_diff, lambda_q1, lambda_k1, lambda_q2, lambda_k2,
           g_subln, conv_w, g_mem, w_mem_kv, g_q_mem, g_k_mem, w_branch, w_out, g_ffn, w_router,
           b_router, w_gate_up, b_gate_up, w_down, b_down):
    batch, seq, d = x.shape
    t = batch * seq
    depth = g_mix.shape[0]
    assert depth == 1 and d == D_MODEL and seq % TQ == 0 and t % TM_PROJ == 0

    pos = jnp.arange(seq, dtype=F32)
    inv_freq = 1.0 / (ROPE_THETA ** (jnp.arange(0, DIFF_HEAD_DIM, 2, dtype=F32) / DIFF_HEAD_DIM))
    ang = pos[:, None] * inv_freq[None, :]
    cos_t = jnp.tile(jnp.cos(ang), (1, 4))
    sin_t = jnp.tile(jnp.concatenate([-jnp.sin(ang), jnp.sin(ang)], axis=1), (1, 2))
    grp = jnp.arange(SEC) // DIFF_HEAD_DIM
    bd = jnp.where(grp[:, None] == grp[None, :], 1.0 / DIFF_HEAD_DIM, 0.0).astype(BF16)
    tok = jnp.arange(TM_POST)
    tri = (tok[:, None] < tok[None, :]).astype(BF16)

    x2d = x.reshape(t, d)
    l = 0
    lam_init = 0.8 - 0.6 * math.exp(-0.3 * l)
    km, vm = _mem_kv(mem.reshape(batch * N_MEM, d), g_mem[l][None, :], w_mem_kv[l].astype(BF16),
                     g_k_mem[l][None, :])
    qn, kn, vt, cb, u, mqn, sig = _in_proj(
        x2d, g_mix[l][None, :], w_in[l].astype(BF16), cos_t, sin_t, bd,
        jnp.tile(g_q_diff[l], SEC // DIFF_HEAD_DIM)[None, :],
        jnp.tile(g_k_diff[l], SEC // DIFF_HEAD_DIM)[None, :],
        g_q_mem[l][None, :], seq)
    lam_rows = jnp.stack([lambda_q1[l], lambda_k1[l], lambda_q2[l], lambda_k2[l]]).astype(F32)
    y_diff = _diff_attn(qn, kn, vt, lam_rows, g_subln[l][None, :], batch, seq, lam_init)
    ex = jnp.arange(N_EXPERTS)
    ltri = (ex[None, :] < ex[:, None]).astype(F32)
    x1, rows, slots, gates, cbase, ntile, off, counts = _post(
        x2d, y_diff, cb, u, mqn, sig, km, vm, conv_w[l], w_branch[l].astype(BF16),
        w_out[l].astype(BF16), g_ffn[l][None, :], w_router[l].T, b_router[l][:, None], tri, ltri, seq)

    n_tiles = t // TM_POST
    n_assign = t * TOP_K
    n_rows = -(-(n_assign + N_EXPERTS * (MOE_BLOCK - 1)) // MOE_BLOCK) * MOE_BLOCK
    n_blocks = n_rows // MOE_BLOCK
    cnt = counts[:, 0]
    padded = (cnt + MOE_BLOCK - 1) // MOE_BLOCK * MOE_BLOCK
    pend = jnp.cumsum(padded)
    pstart = pend - padded
    block_row0 = jnp.arange(n_blocks, dtype=jnp.int32) * MOE_BLOCK
    block_expert = jnp.minimum(
        jnp.sum((pend[None, :] <= block_row0[:, None]).astype(jnp.int32), axis=1), N_EXPERTS - 1)
    n_active = (pend[-1:] // MOE_BLOCK).astype(jnp.int32)
    cbase_t = cbase[:, 0].reshape(n_tiles, N_EXPERTS)
    ntile_t = ntile[:, 0].reshape(n_tiles, N_EXPERTS)
    off_t = off[:, 0].reshape(n_tiles, N_EXPERTS)
    cend_b = (cbase_t + ntile_t).T[block_expert]
    r0 = block_row0 - pstart[block_expert]
    first_tile = jnp.sum((cend_b <= r0[:, None]).astype(jnp.int32), axis=1)
    first_tile = jnp.minimum(first_tile, n_tiles - 1).astype(jnp.int32)

    zero_rows = jnp.zeros((MOE_BLOCK * ROW_CHUNKS, LANES), F32)
    ys_rows = _experts(block_expert, first_tile, n_active, pstart, cnt,
                       cbase_t.T.reshape(-1), ntile_t.T.reshape(-1), off_t.T.reshape(-1),
                       rows, zero_rows, w_gate_up[l], b_gate_up[l][:, None, :],
                       w_down[l], b_down[l][:, None, :], n_tiles)
    out = _combine(pstart, cbase_t.reshape(-1), ntile_t.reshape(-1), off_t.reshape(-1), slots, gates,
                   x1, ys_rows)
    return out.reshape(batch, seq, d)
```

```python
import functools
import math

import jax
import jax.numpy as jnp
from jax import lax
from jax.experimental import pallas as pl
from jax.experimental.pallas import tpu as pltpu

F32 = jnp.float32
BF16 = jnp.bfloat16

D_MODEL = 1024
CHUNK = 64
EPS = 1e-6
ROPE_THETA = 10000.0
DIFF_HEADS = 4
DIFF_HEAD_DIM = 64
DIFF_V_DIM = 2 * DIFF_HEAD_DIM
N_MEM = 256
MEM_HEADS = 4
MEM_HEAD_DIM = 128
SEC = 512
N_SEC = 13
N_EXPERTS = 32
TOP_K = 4
SWIGLU_LIMIT = 7.0
SWIGLU_ALPHA = 1.702
MOE_BLOCK = 512
LANES = 128
ROW_CHUNKS = D_MODEL // LANES
LOG2E = 1.4426950408889634
NEG_BIG = -1e30

TM_PROJ = 512
TQ = 1024
ATTN_STRIP = 512
TM_POST = 256
MAX_RUN = max(MOE_BLOCK, TM_POST)
SHORT_RUN = 64
FF_CHUNK = 256
VMEM_LIMIT = 48 * 1024 * 1024
VMEM_LIMIT_EXPERTS = 56 * 1024 * 1024


def _rms(x, eps=EPS):
    return x * lax.rsqrt(jnp.mean(x * x, axis=-1, keepdims=True) + eps)


def _mem_kv_kernel(mem_ref, g_ref, w_ref, gk_ref, k_out, v_out):
    h = (_rms(mem_ref[...]) * g_ref[...]).astype(BF16)
    kv = jnp.dot(h, w_ref[...], preferred_element_type=F32)
    for hd in range(MEM_HEADS):
        sl = slice(hd * MEM_HEAD_DIM, (hd + 1) * MEM_HEAD_DIM)
        k_out[:, sl] = (_rms(kv[:, sl]) * gk_ref[...]).astype(BF16)
    v_out[...] = kv[:, SEC:].astype(BF16)


def _mem_kv(mem2d, g_mem, w_kv, g_k):
    rows = mem2d.shape[0]
    return pl.pallas_call(
        _mem_kv_kernel,
        grid=(rows // N_MEM,),
        in_specs=[
            pl.BlockSpec((N_MEM, D_MODEL), lambda i: (i, 0)),
            pl.BlockSpec((1, D_MODEL), lambda i: (0, 0)),
            pl.BlockSpec((D_MODEL, 2 * SEC), lambda i: (0, 0)),
            pl.BlockSpec((1, MEM_HEAD_DIM), lambda i: (0, 0)),
        ],
        out_specs=[pl.BlockSpec((N_MEM, SEC), lambda i: (i, 0))] * 2,
        out_shape=[jax.ShapeDtypeStruct((rows, SEC), BF16)] * 2,
        compiler_params=pltpu.CompilerParams(vmem_limit_bytes=VMEM_LIMIT),
        name="mem_kv",
    )(mem2d, g_mem, w_kv, g_k)


def _in_proj_kernel(x_ref, g_ref, w_ref, cos_ref, sin_ref, bd_ref, gq_ref, gk_ref, gqm_ref,
                    q_out, k_out, v_out, cb_out, u_out, mq_out, sig_out):
    tm = x_ref.shape[0]
    h = (_rms(x_ref[...]) * g_ref[...]).astype(BF16)

    def proj(sec):
        return jnp.dot(h, w_ref[:, sec * SEC:(sec + 1) * SEC], preferred_element_type=F32)

    cos = jnp.concatenate([cos_ref[...]] * (SEC // LANES), axis=1)
    sin = jnp.concatenate([sin_ref[...]] * (SEC // LANES), axis=1)
    lane = lax.broadcasted_iota(jnp.int32, (tm, SEC), 1)
    first_half = (lane & (DIFF_HEAD_DIM // 2)) == 0

    def norm_rope(a, g, scale):
        ms = jnp.dot((a * a).astype(BF16), bd_ref[...], preferred_element_type=F32)
        y = a * lax.rsqrt(ms + EPS) * g
        partner = jnp.where(first_half,
                            pltpu.roll(y, SEC - DIFF_HEAD_DIM // 2, 1),
                            pltpu.roll(y, DIFF_HEAD_DIM // 2, 1))
        return (y * cos + partner * sin) * scale

    q_out[...] = norm_rope(proj(0), gq_ref[...], DIFF_HEAD_DIM ** -0.5 * LOG2E).astype(BF16)
    k_out[...] = norm_rope(proj(1), gk_ref[...], 1.0).astype(BF16)
    v = proj(2)
    for hd in range(DIFF_HEADS):
        v_out[0, hd, 0] = v[:, hd * DIFF_V_DIM:(hd + 1) * DIFF_V_DIM].T.astype(BF16)
    cb_out[...] = proj(3).astype(BF16)
    u_out[...] = proj(4) * proj(5)
    mq = proj(6)
    for hd in range(MEM_HEADS):
        sl = slice(hd * MEM_HEAD_DIM, (hd + 1) * MEM_HEAD_DIM)
        mq_out[:, sl] = (_rms(mq[:, sl]) * gqm_ref[...]
                         * (MEM_HEAD_DIM ** -0.5 * LOG2E)).astype(BF16)
    for s in range(7, N_SEC):
        a = proj(s)
        sig_out[:, (s - 7) * SEC:(s - 6) * SEC] = (1.0 / (1.0 + jnp.exp(-a))).astype(BF16)


def _in_proj(x2d, g_mix, w_in, cos_t, sin_t, bd, gq, gk, gqm, seq):
    t = x2d.shape[0]
    tm = TM_PROJ
    tiles_per_seq = seq // tm
    tiles_per_key = TQ // tm
    row = lambda i: (i, 0)
    const = lambda i: (0, 0)
    outs = [
        jax.ShapeDtypeStruct((t, SEC), BF16),
        jax.ShapeDtypeStruct((t, SEC), BF16),
        jax.ShapeDtypeStruct((t // seq, DIFF_HEADS, seq // TQ, DIFF_V_DIM, TQ), BF16),
        jax.ShapeDtypeStruct((t, SEC), BF16),
        jax.ShapeDtypeStruct((t, SEC), F32),
        jax.ShapeDtypeStruct((t, SEC), BF16),
        jax.ShapeDtypeStruct((t, 3 * D_MODEL), BF16),
    ]
    return pl.pallas_call(
        _in_proj_kernel,
        grid=(t // tm,),
        in_specs=[
            pl.BlockSpec((tm, D_MODEL), row),
            pl.BlockSpec((1, D_MODEL), const),
            pl.BlockSpec((D_MODEL, N_SEC * SEC), const),
            pl.BlockSpec((tm, LANES), lambda i: (i % tiles_per_seq, 0)),
            pl.BlockSpec((tm, LANES), lambda i: (i % tiles_per_seq, 0)),
            pl.BlockSpec((SEC, SEC), const),
            pl.BlockSpec((1, SEC), const),
            pl.BlockSpec((1, SEC), const),
            pl.BlockSpec((1, MEM_HEAD_DIM), const),
        ],
        out_specs=[
            pl.BlockSpec((1, DIFF_HEADS, 1, DIFF_V_DIM, tm),
                         lambda i: (i // tiles_per_seq, 0, (i % tiles_per_seq) // tiles_per_key,
                                    0, i % tiles_per_key))
            if o.ndim == 5 else pl.BlockSpec((tm, o.shape[1]), row) for o in outs],
        out_shape=outs,
        compiler_params=pltpu.CompilerParams(vmem_limit_bytes=VMEM_LIMIT),
        name="in_proj",
    )(x2d, g_mix, w_in, cos_t, sin_t, bd, gq, gk, gqm)


def _diff_attn_kernel(q_ref, k_ref, vt_ref, lam_ref, gs_ref, o_ref, acc_sc, s0_sc, *, lam_init):
    i = pl.program_id(2)
    tq = q_ref.shape[0]
    q = q_ref[...]
    lane = lax.broadcasted_iota(jnp.int32, q.shape, 1)
    zero = jnp.zeros_like(q)
    qq = jnp.concatenate([jnp.where(lane < DIFF_HEAD_DIM, q, zero),
                          jnp.where(lane >= DIFF_HEAD_DIM, q, zero)], axis=0)
    acc_sc[...] = jnp.zeros(acc_sc.shape, F32)
    n_strips = 2 * tq // ATTN_STRIP

    def scores(j, c, nk):
        off = pl.multiple_of(j * tq, tq)
        return lax.dot_general(k_ref[pl.ds(off, nk), :], qq[c * ATTN_STRIP:(c + 1) * ATTN_STRIP, :],
                               (((1,), (1,)), ((), ())), preferred_element_type=F32)

    def step(j, ms, ls, masked):
        q_offs = [(c * ATTN_STRIP) % tq for c in range(n_strips)]
        nks = [min(tq, qo + ATTN_STRIP) if masked else tq for qo in q_offs]
        ms_new, ls_new = [], []
        s = s0_sc[0:nks[0], :]
        for c in range(n_strips):
            cols = slice(c * ATTN_STRIP, (c + 1) * ATTN_STRIP)
            if c + 1 < n_strips:
                s_next = scores(j, c + 1, nks[c + 1])
            elif not masked:
                s0_sc[...] = scores(j + 1, 0, tq)
            if masked:
                r = lax.broadcasted_iota(jnp.int32, s.shape, 0)
                col = lax.broadcasted_iota(jnp.int32, s.shape, 1)
                s = jnp.where((r // CHUNK) <= ((col + q_offs[c]) // CHUNK), s, NEG_BIG)
            m_new = jnp.maximum(ms[c], jnp.max(s, axis=0, keepdims=True))
            p = jnp.exp2(s - m_new)
            alpha = jnp.exp2(ms[c] - m_new)
            ls_new.append(alpha * ls[c] + jnp.sum(p, axis=0, keepdims=True))
            ms_new.append(m_new)
            acc_sc[:, cols] = alpha * acc_sc[:, cols] + jnp.dot(
                vt_ref[0, 0, j][:, :nks[c]], p.astype(BF16), preferred_element_type=F32)
            if c + 1 < n_strips:
                s = s_next
        return tuple(ms_new), tuple(ls_new)

    s0_sc[...] = scores(0, 0, tq)
    m0 = tuple(jnp.full((1, ATTN_STRIP), NEG_BIG, F32) for _ in range(n_strips))
    l0 = tuple(jnp.zeros((1, ATTN_STRIP), F32) for _ in range(n_strips))
    m, l = lax.fori_loop(0, i, lambda j, c: step(j, c[0], c[1], False), (m0, l0))
    m, l = step(i, m, l, True)

    ot = acc_sc[...] / jnp.concatenate(l, axis=1)
    lam = (jnp.exp(jnp.sum(lam_ref[0:1, :] * lam_ref[1:2, :], axis=-1, keepdims=True))
           - jnp.exp(jnp.sum(lam_ref[2:3, :] * lam_ref[3:4, :], axis=-1, keepdims=True))
           + lam_init)
    d = (ot[:, :tq] - lam * ot[:, tq:]).T
    o_ref[...] = (_rms(d) * gs_ref[...] * (1.0 - lam_init)).astype(BF16)


def _diff_attn(qn, kn, vt, lam_rows, g_subln, batch, seq, lam_init):
    t = qn.shape[0]
    nq = seq // TQ
    return pl.pallas_call(
        functools.partial(_diff_attn_kernel, lam_init=lam_init),
        grid=(batch, DIFF_HEADS, nq),
        in_specs=[
            pl.BlockSpec((TQ, DIFF_V_DIM), lambda b, h, i: (b * nq + i, h)),
            pl.BlockSpec((seq, DIFF_V_DIM), lambda b, h, i: (b, h)),
            pl.BlockSpec((1, 1, nq, DIFF_V_DIM, TQ), lambda b, h, i: (b, h, 0, 0, 0)),
            pl.BlockSpec((4, DIFF_HEAD_DIM), lambda b, h, i: (0, 0)),
            pl.BlockSpec((1, DIFF_V_DIM), lambda b, h, i: (0, 0)),
        ],
        out_specs=pl.BlockSpec((TQ, DIFF_V_DIM), lambda b, h, i: (b * nq + i, h)),
        out_shape=jax.ShapeDtypeStruct((t, SEC), BF16),
        scratch_shapes=[pltpu.VMEM((DIFF_V_DIM, 2 * TQ), F32), pltpu.VMEM((TQ, ATTN_STRIP), F32)],
        compiler_params=pltpu.CompilerParams(vmem_limit_bytes=VMEM_LIMIT),
        name="diff_attn",
    )(qn, kn, vt, lam_rows, g_subln)


def _row_span(ref, row, n):
    return ref.at[pl.ds(pl.multiple_of(row * ROW_CHUNKS, ROW_CHUNKS), n * ROW_CHUNKS), :]


def _chunk_rows(row0, n, c):
    return pl.ds(row0 * ROW_CHUNKS + c, n, stride=ROW_CHUNKS)


def _load_rows(ref, row0, n):
    return jnp.concatenate([ref[_chunk_rows(row0, n, c), :] for c in range(ROW_CHUNKS)], axis=1)


def _store_rows(ref, n, val):
    for c in range(ROW_CHUNKS):
        ref[_chunk_rows(0, n, c), :] = val[:, c * LANES:(c + 1) * LANES]


def _copy_run(src_hbm, buf, sem, src_row, dst_row, n):
    def pieces(sizes, src_row, dst_row):
        for piece in sizes:
            take = (n & piece) != 0

            @pl.when(take)
            def _(piece=piece, src_row=src_row, dst_row=dst_row):
                pltpu.make_async_copy(_row_span(src_hbm, src_row, piece),
                                      _row_span(buf, dst_row, piece), sem).start()

            step = jnp.where(take, piece, 0)
            src_row = src_row + step
            dst_row = dst_row + step

    sizes = [MAX_RUN >> k for k in range(MAX_RUN.bit_length())]

    @pl.when(n >= SHORT_RUN)
    def _():
        pieces([p for p in sizes if p >= SHORT_RUN], src_row, dst_row)

    covered = n - (n & (SHORT_RUN - 1))
    pieces([p for p in sizes if p < SHORT_RUN], src_row + covered, dst_row + covered)


def _wait_rows(n, src_hbm, buf, sem, base):
    pltpu.make_async_copy(_row_span(src_hbm, 0, n), _row_span(buf, base, n), sem).wait()


def _post_kernel(x_ref, yd_ref, cb_ref, u_ref, up_ref, mq_ref, sig_ref, km_ref, vm_ref,
                 cw_ref, wb_ref, wo_ref, gf_ref, wr_ref, br_ref, tri_ref, ltri_ref,
                 x1_out, rows_out, slot_out, gate_out, cbase_out, ntile_out, off_out, cnt_out, carry_sc,
                 *, seq):
    i = pl.program_id(0)
    tm = x_ref.shape[0]

    @pl.when(i == 0)
    def _():
        carry_sc[...] = jnp.zeros(carry_sc.shape, F32)

    u = u_ref[...]
    seq_start = (i * tm) % seq == 0
    up = jnp.where(seq_start, 0.0, up_ref[...])
    r = lax.broadcasted_iota(jnp.int32, u.shape, 0)
    u1 = jnp.where(r == 0, up[7:8, :], pltpu.roll(u, 1, 0))
    u2 = jnp.where(r == 0, up[6:7, :], jnp.where(r == 1, up[7:8, :], pltpu.roll(u, 2, 0)))
    y_conv = cb_ref[...].astype(F32) * (cw_ref[0:1, :] * u2 + cw_ref[1:2, :] * u1 + cw_ref[2:3, :] * u)

    def mem_scores(hd):
        sl = slice(hd * MEM_HEAD_DIM, (hd + 1) * MEM_HEAD_DIM)
        return lax.dot_general(mq_ref[:, sl], km_ref[:, sl], (((1,), (1,)), ((), ())),
                               preferred_element_type=F32)

    y_mem = []
    s_next = mem_scores(0)
    for hd in range(MEM_HEADS):
        sl = slice(hd * MEM_HEAD_DIM, (hd + 1) * MEM_HEAD_DIM)
        s = s_next
        if hd + 1 < MEM_HEADS:
            s_next = mem_scores(hd + 1)
        if hd == 0:
            merged = (sig_ref[:, 0:D_MODEL].astype(F32)
                      * jnp.dot(yd_ref[...], wb_ref[0], preferred_element_type=F32))
        elif hd == 1:
            merged += (sig_ref[:, D_MODEL:2 * D_MODEL].astype(F32)
                       * jnp.dot(y_conv.astype(BF16), wb_ref[1], preferred_element_type=F32))
        p = jnp.exp2(s - jnp.max(s, axis=-1, keepdims=True))
        o = jnp.dot(p.astype(BF16), vm_ref[:, sl], preferred_element_type=F32)
        y_mem.append(o / jnp.sum(p, axis=-1, keepdims=True))

    merged += (sig_ref[:, 2 * D_MODEL:3 * D_MODEL].astype(F32)
               * jnp.dot(jnp.concatenate(y_mem, axis=1).astype(BF16), wb_ref[2],
                         preferred_element_type=F32))
    x1 = x_ref[...] + jnp.dot(merged.astype(BF16), wo_ref[...], preferred_element_type=F32)
    x1_out[...] = x1

    h2 = _rms(x1) * gf_ref[...]

    nt = (((1,), (1,)), ((), ()))
    w = wr_ref[...]
    w_hi = w.astype(BF16)
    w_lo = (w - w_hi.astype(F32)).astype(BF16)
    h_hi = h2.astype(BF16)
    h_lo = (h2 - h_hi.astype(F32)).astype(BF16)
    part = lax.dot_general(jnp.concatenate([w_hi, w_lo], axis=0), h_hi, nt,
                           preferred_element_type=F32)
    logits = (part[:N_EXPERTS] + part[N_EXPERTS:]
              + lax.dot_general(w_hi, h_lo, nt, preferred_element_type=F32) + br_ref[...])
    eio = lax.broadcasted_iota(jnp.int32, logits.shape, 0)
    work = logits
    vals, hots = [], []
    for k in range(TOP_K):
        mk = jnp.max(work, axis=0, keepdims=True)
        ik = jnp.min(jnp.where(work == mk, eio, N_EXPERTS), axis=0, keepdims=True)
        hot = eio == ik
        work = jnp.where(hot, -jnp.inf, work)
        vals.append(mk)
        hots.append(hot)
    ex = [jnp.exp(v - vals[0]) for v in vals]
    den = ex[0] + ex[1] + ex[2] + ex[3]
    gates = [e / den for e in ex]

    assign = jnp.zeros(logits.shape, F32)
    for hot in hots:
        assign = jnp.where(hot, 1.0, assign)
    earlier = jnp.dot(assign.astype(BF16), tri_ref[...], preferred_element_type=F32)
    n_col = jnp.sum(assign, axis=1, keepdims=True)
    off_col = jnp.dot(ltri_ref[...], jnp.broadcast_to(n_col, (N_EXPERTS, LANES)),
                      precision=lax.Precision.HIGHEST, preferred_element_type=F32)[:, 0:1]
    slots = [jnp.sum(jnp.where(hot, earlier + off_col, 0.0), axis=0, keepdims=True).astype(jnp.int32)
             for hot in hots]
    for k in range(TOP_K):
        slot_out[k:k + 1, :] = slots[k]
        gate_out[k:k + 1, :] = gates[k]

    n_slots = TOP_K * tm
    jdx = lax.broadcasted_iota(jnp.int32, (n_slots, tm), 0)
    pick = jnp.where(jdx == slots[0], 1.0, jnp.where(jdx == slots[1], 1.0, jnp.where(
        jdx == slots[2], 1.0, jnp.where(jdx == slots[3], 1.0, 0.0)))).astype(BF16)
    rows = jnp.dot(pick, h_hi, preferred_element_type=F32)
    _store_rows(rows_out, n_slots, rows)

    cbase_out[...] = jnp.broadcast_to(carry_sc[...], cbase_out.shape).astype(jnp.int32)
    ntile_out[...] = jnp.broadcast_to(n_col, ntile_out.shape).astype(jnp.int32)
    off_out[...] = jnp.broadcast_to(off_col, off_out.shape).astype(jnp.int32)
    carry_sc[...] = carry_sc[...] + n_col
    cnt_out[...] = jnp.broadcast_to(carry_sc[...], cnt_out.shape).astype(jnp.int32)


def _post(x2d, y_diff, cb, u, mqn, sig, km, vm, conv_w, w_branch, w_out, g_ffn, w_rt, b_r, tri, ltri,
          seq):
    t = x2d.shape[0]
    tm = TM_POST
    n_tiles = t // tm
    row = lambda i: (i, 0)
    const = lambda i: (0, 0)
    table = jax.ShapeDtypeStruct((n_tiles * N_EXPERTS, LANES), jnp.int32)
    outs = [
        jax.ShapeDtypeStruct((t, D_MODEL), F32),
        jax.ShapeDtypeStruct((TOP_K * t * ROW_CHUNKS, LANES), F32),
        jax.ShapeDtypeStruct((TOP_K, t), jnp.int32),
        jax.ShapeDtypeStruct((TOP_K, t), F32),
        table, table, table,
        jax.ShapeDtypeStruct((N_EXPERTS, LANES), jnp.int32),
    ]
    return pl.pallas_call(
        functools.partial(_post_kernel, seq=seq),
        grid=(t // tm,),
        in_specs=[
            pl.BlockSpec((tm, D_MODEL), row),
            pl.BlockSpec((tm, SEC), row),
            pl.BlockSpec((tm, SEC), row),
            pl.BlockSpec((tm, SEC), row),
            pl.BlockSpec((8, SEC), lambda i: (jnp.maximum(i * (tm // 8) - 1, 0), 0)),
            pl.BlockSpec((tm, SEC), row),
            pl.BlockSpec((tm, 3 * D_MODEL), row),
            pl.BlockSpec((N_MEM, SEC), lambda i: ((i * tm) // seq, 0)),
            pl.BlockSpec((N_MEM, SEC), lambda i: ((i * tm) // seq, 0)),
            pl.BlockSpec((3, SEC), const),
            pl.BlockSpec((3, SEC, D_MODEL), lambda i: (0, 0, 0)),
            pl.BlockSpec((D_MODEL, D_MODEL), const),
            pl.BlockSpec((1, D_MODEL), const),
            pl.BlockSpec((N_EXPERTS, D_MODEL), const),
            pl.BlockSpec((N_EXPERTS, 1), const),
            pl.BlockSpec((tm, tm), const),
            pl.BlockSpec((N_EXPERTS, N_EXPERTS), const),
        ],
        out_specs=[
            pl.BlockSpec((tm, D_MODEL), row),
            pl.BlockSpec((TOP_K * tm * ROW_CHUNKS, LANES), row),
            pl.BlockSpec((TOP_K, tm), lambda i: (0, i)),
            pl.BlockSpec((TOP_K, tm), lambda i: (0, i)),
            pl.BlockSpec((N_EXPERTS, LANES), row),
            pl.BlockSpec((N_EXPERTS, LANES), row),
            pl.BlockSpec((N_EXPERTS, LANES), row),
            pl.BlockSpec((N_EXPERTS, LANES), const),
        ],
        out_shape=outs,
        scratch_shapes=[pltpu.VMEM((N_EXPERTS, 1), F32)],
        compiler_params=pltpu.CompilerParams(dimension_semantics=("arbitrary",),
                                             vmem_limit_bytes=VMEM_LIMIT),
        name="post",
    )(x2d, y_diff, cb, u, u, mqn, sig, km, vm, conv_w, w_branch, w_out, g_ffn, w_rt, b_r, tri, ltri)


def _experts_kernel(be_ref, t0_ref, na_ref, ps_ref, cnt_ref, cb_ref, nt_ref, of_ref,
                    rows_hbm, zeros_hbm, wgu_f32, bgu_ref, wd_f32, bd_ref,
                    ys_out, xbuf, sem, wgu_ref, wd_ref, *, n_tiles, slots_per_tile):
    b = pl.program_id(0)
    n_active = na_ref[0]
    slot = b % 2

    @pl.when(jnp.logical_and(b < n_active,
                             jnp.logical_or(b == 0, be_ref[b] != be_ref[jnp.maximum(b - 1, 0)])))
    def _():
        wgu_ref[...] = wgu_f32[0].astype(BF16)
        wd_ref[...] = wd_f32[0].astype(BF16)

    def issue(blk, buf_slot):
        e = be_ref[blk]
        r0 = blk * MOE_BLOCK - ps_ref[e]
        r1 = r0 + MOE_BLOCK
        base = buf_slot * MOE_BLOCK
        sm = sem.at[buf_slot]

        def cond(i):
            return jnp.logical_and(i < n_tiles, cb_ref[e * n_tiles + jnp.minimum(i, n_tiles - 1)] < r1)

        def body(i):
            c = cb_ref[e * n_tiles + i]
            lo = jnp.maximum(c, r0)
            hi = jnp.minimum(c + nt_ref[e * n_tiles + i], r1)
            src = i * slots_per_tile + of_ref[e * n_tiles + i] + (lo - c)
            _copy_run(rows_hbm, xbuf, sm, src, base + (lo - r0), jnp.maximum(hi - lo, 0))
            return i + 1

        lax.while_loop(cond, body, t0_ref[blk])
        valid = jnp.clip(cnt_ref[e] - r0, 0, MOE_BLOCK)
        _copy_run(zeros_hbm, xbuf, sm, 0, base + valid, MOE_BLOCK - valid)

    @pl.when(b == 0)
    def _():
        issue(b, 0)

    @pl.when(b + 1 < n_active)
    def _():
        issue(b + 1, 1 - slot)

    def compute(n_rows):
        base = pl.multiple_of(slot * MOE_BLOCK, MOE_BLOCK)
        _wait_rows(MOE_BLOCK, rows_hbm, xbuf, sem.at[slot], base)
        x = _load_rows(xbuf, base, n_rows).astype(BF16)
        d_ff = wd_ref.shape[0]

        def gate_up(j):
            gs = slice(j * FF_CHUNK, (j + 1) * FF_CHUNK)
            us = slice(d_ff + j * FF_CHUNK, d_ff + (j + 1) * FF_CHUNK)
            return (jnp.dot(x, wgu_ref[:, gs], preferred_element_type=F32) + bgu_ref[0, :, gs],
                    jnp.dot(x, wgu_ref[:, us], preferred_element_type=F32) + bgu_ref[0, :, us])

        acts = []
        nxt = gate_up(0)
        for j in range(d_ff // FF_CHUNK):
            g, u = nxt
            if (j + 1) * FF_CHUNK < d_ff:
                nxt = gate_up(j + 1)
            g = jnp.minimum(g, SWIGLU_LIMIT)
            u = jnp.clip(u, -SWIGLU_LIMIT, SWIGLU_LIMIT)
            acts.append(((u + 1.0) * (g * (1.0 / (1.0 + jnp.exp(-SWIGLU_ALPHA * g))))).astype(BF16))
        act = jnp.concatenate(acts, axis=1)
        for n in range(D_MODEL // FF_CHUNK):
            cols = slice(n * FF_CHUNK, (n + 1) * FF_CHUNK)
            yn = jnp.dot(act, wd_ref[:, cols], preferred_element_type=F32) + bd_ref[0, :, cols]
            for c in range(FF_CHUNK // LANES):
                ys_out[_chunk_rows(0, n_rows, n * (FF_CHUNK // LANES) + c), :] = (
                    yn[:, c * LANES:(c + 1) * LANES])
        if n_rows < MOE_BLOCK:
            ys_out[n_rows * ROW_CHUNKS:, :] = jnp.zeros(((MOE_BLOCK - n_rows) * ROW_CHUNKS, LANES), F32)

    e_cur = be_ref[b]
    valid_cur = cnt_ref[e_cur] - (b * MOE_BLOCK - ps_ref[e_cur])

    @pl.when(jnp.logical_and(b < n_active, valid_cur > MOE_BLOCK // 2))
    def _():
        compute(MOE_BLOCK)

    @pl.when(jnp.logical_and(b < n_active, valid_cur <= MOE_BLOCK // 2))
    def _():
        compute(MOE_BLOCK // 2)

    @pl.when(b >= n_active)
    def _():
        ys_out[...] = jnp.zeros(ys_out.shape, F32)


def _experts(block_expert, first_tile, n_active, pstart, counts, cbase_e, ntile_e, off_e,
             rows, zero_rows, wgu, bgu, wd, bd, n_tiles):
    n_blocks = block_expert.shape[0]
    ff2 = wgu.shape[2]
    by_expert = lambda b, be, *_: (be[b], 0, 0)
    grid_spec = pltpu.PrefetchScalarGridSpec(
        num_scalar_prefetch=8,
        grid=(n_blocks,),
        in_specs=[
            pl.BlockSpec(memory_space=pl.ANY),
            pl.BlockSpec(memory_space=pl.ANY),
            pl.BlockSpec((1, D_MODEL, ff2), by_expert),
            pl.BlockSpec((1, 1, ff2), by_expert),
            pl.BlockSpec((1, ff2 // 2, D_MODEL), by_expert),
            pl.BlockSpec((1, 1, D_MODEL), by_expert),
        ],
        out_specs=pl.BlockSpec((MOE_BLOCK * ROW_CHUNKS, LANES), lambda b, *_: (b, 0)),
        scratch_shapes=[
            pltpu.VMEM((2 * MOE_BLOCK * ROW_CHUNKS, LANES), F32),
            pltpu.SemaphoreType.DMA((2,)),
            pltpu.VMEM((D_MODEL, ff2), BF16),
            pltpu.VMEM((ff2 // 2, D_MODEL), BF16),
        ],
    )
    return pl.pallas_call(
        functools.partial(_experts_kernel, n_tiles=n_tiles, slots_per_tile=TOP_K * TM_POST),
        grid_spec=grid_spec,
        out_shape=jax.ShapeDtypeStruct((n_blocks * MOE_BLOCK * ROW_CHUNKS, LANES), F32),
        compiler_params=pltpu.CompilerParams(dimension_semantics=("arbitrary",),
                                             vmem_limit_bytes=VMEM_LIMIT_EXPERTS),
        name="experts",
    )(block_expert, first_tile, n_active, pstart, counts, cbase_e, ntile_e, off_e,
      rows, zero_rows, wgu, bgu, wd, bd)


def _combine_kernel(ps_ref, cb_ref, nt_ref, of_ref, slot_ref, gate_ref, x1_ref, ys_hbm, o_ref, buf,
                    sem):
    i = pl.program_id(0)
    n = pl.num_programs(0)
    tm = x1_ref.shape[0]
    n_slots = TOP_K * tm
    cur = i % 2

    def issue(tile, buf_slot):
        def body(e, carry):
            k = tile * N_EXPERTS + e
            _copy_run(ys_hbm, buf, sem.at[buf_slot], ps_ref[e] + cb_ref[k],
                      buf_slot * n_slots + of_ref[k], nt_ref[k])
            return carry
        lax.fori_loop(0, N_EXPERTS, body, 0)

    @pl.when(i == 0)
    def _():
        issue(i, 0)

    @pl.when(i + 1 < n)
    def _():
        issue(i + 1, 1 - cur)

    base = pl.multiple_of(cur * n_slots, n_slots)
    _wait_rows(n_slots, ys_hbm, buf, sem.at[cur], base)
    y = _load_rows(buf, base, n_slots).astype(BF16)
    jdx = lax.broadcasted_iota(jnp.int32, (n_slots, tm), 0)
    pick = jnp.where(jdx == slot_ref[0:1, :], gate_ref[0:1, :], jnp.where(
        jdx == slot_ref[1:2, :], gate_ref[1:2, :], jnp.where(
            jdx == slot_ref[2:3, :], gate_ref[2:3, :], jnp.where(
                jdx == slot_ref[3:4, :], gate_ref[3:4, :], 0.0)))).astype(BF16)
    o_ref[...] = x1_ref[...] + lax.dot_general(pick, y, (((0,), (0,)), ((), ())),
                                               preferred_element_type=F32)


def _combine(pstart, cbase_t, ntile_t, off_t, slots, gates, x1, ys_rows):
    t = x1.shape[0]
    tm = TM_POST
    n_slots = TOP_K * tm
    grid_spec = pltpu.PrefetchScalarGridSpec(
        num_scalar_prefetch=4,
        grid=(t // tm,),
        in_specs=[
            pl.BlockSpec((TOP_K, tm), lambda i, *_: (0, i)),
            pl.BlockSpec((TOP_K, tm), lambda i, *_: (0, i)),
            pl.BlockSpec((tm, D_MODEL), lambda i, *_: (i, 0)),
            pl.BlockSpec(memory_space=pl.ANY),
        ],
        out_specs=pl.BlockSpec((tm, D_MODEL), lambda i, *_: (i, 0)),
        scratch_shapes=[
            pltpu.VMEM((2 * n_slots * ROW_CHUNKS, LANES), F32),
            pltpu.SemaphoreType.DMA((2,)),
        ],
    )
    return pl.pallas_call(
        _combine_kernel,
        grid_spec=grid_spec,
        out_shape=jax.ShapeDtypeStruct((t, D_MODEL), F32),
        compiler_params=pltpu.CompilerParams(dimension_semantics=("arbitrary",),
                                             vmem_limit_bytes=VMEM_LIMIT),
        name="combine",
    )(pstart, cbase_t, ntile_t, off_t, slots, gates, x1, ys_rows)


def kernel(x, mem, g_mix, w_in, g_q_diff, g_k_diff, lambda_q1, lambda_k1, lambda_q2, lambda_k2,
           g_subln, conv_w, g_mem, w_mem_kv, g_q_mem, g_k_mem, w_branch, w_out, g_ffn, w_router,
           b_router, w_gate_up, b_gate_up, w_down, b_down):
    batch, seq, d = x.shape
    t = batch * seq
    depth = g_mix.shape[0]
    assert depth == 1 and d == D_MODEL and seq % TQ == 0 and t % TM_PROJ == 0

    pos = jnp.arange(seq, dtype=F32)
    inv_freq = 1.0 / (ROPE_THETA ** (jnp.arange(0, DIFF_HEAD_DIM, 2, dtype=F32) / DIFF_HEAD_DIM))
    ang = pos[:, None] * inv_freq[None, :]
    cos_t = jnp.tile(jnp.cos(ang), (1, 4))
    sin_t = jnp.tile(jnp.concatenate([-jnp.sin(ang), jnp.sin(ang)], axis=1), (1, 2))
    grp = jnp.arange(SEC) // DIFF_HEAD_DIM
    bd = jnp.where(grp[:, None] == grp[None, :], 1.0 / DIFF_HEAD_DIM, 0.0).astype(BF16)
    tok = jnp.arange(TM_POST)
    tri = (tok[:, None] < tok[None, :]).astype(BF16)

    x2d = x.reshape(t, d)
    l = 0
    lam_init = 0.8 - 0.6 * math.exp(-0.3 * l)
    km, vm = _mem_kv(mem.reshape(batch * N_MEM, d), g_mem[l][None, :], w_mem_kv[l].astype(BF16),
                     g_k_mem[l][None, :])
    qn, kn, vt, cb, u, mqn, sig = _in_proj(
        x2d, g_mix[l][None, :], w_in[l].astype(BF16), cos_t, sin_t, bd,
        jnp.tile(g_q_diff[l], SEC // DIFF_HEAD_DIM)[None, :],
        jnp.tile(g_k_diff[l], SEC // DIFF_HEAD_DIM)[None, :],
        g_q_mem[l][None, :], seq)
    lam_rows = jnp.stack([lambda_q1[l], lambda_k1[l], lambda_q2[l], lambda_k2[l]]).astype(F32)
    y_diff = _diff_attn(qn, kn, vt, lam_rows, g_subln[l][None, :], batch, seq, lam_init)
    ex = jnp.arange(N_EXPERTS)
    ltri = (ex[None, :] < ex[:, None]).astype(F32)
    x1, rows, slots, gates, cbase, ntile, off, counts = _post(
        x2d, y_diff, cb, u, mqn, sig, km, vm, conv_w[l], w_branch[l].astype(BF16),
        w_out[l].astype(BF16), g_ffn[l][None, :], w_router[l].T, b_router[l][:, None], tri, ltri, seq)

    n_tiles = t // TM_POST
    n_assign = t * TOP_K
    n_rows = -(-(n_assign + N_EXPERTS * (MOE_BLOCK - 1)) // MOE_BLOCK) * MOE_BLOCK
    n_blocks = n_rows // MOE_BLOCK
    cnt = counts[:, 0]
    padded = (cnt + MOE_BLOCK - 1) // MOE_BLOCK * MOE_BLOCK
    pend = jnp.cumsum(padded)
    pstart = pend - padded
    block_row0 = jnp.arange(n_blocks, dtype=jnp.int32) * MOE_BLOCK
    block_expert = jnp.minimum(
        jnp.sum((pend[None, :] <= block_row0[:, None]).astype(jnp.int32), axis=1), N_EXPERTS - 1)
    n_active = (pend[-1:] // MOE_BLOCK).astype(jnp.int32)
    cbase_t = cbase[:, 0].reshape(n_tiles, N_EXPERTS)
    ntile_t = ntile[:, 0].reshape(n_tiles, N_EXPERTS)
    off_t = off[:, 0].reshape(n_tiles, N_EXPERTS)
    cend_b = (cbase_t + ntile_t).T[block_expert]
    r0 = block_row0 - pstart[block_expert]
    first_tile = jnp.sum((cend_b <= r0[:, None]).astype(jnp.int32), axis=1)
    first_tile = jnp.minimum(first_tile, n_tiles - 1).astype(jnp.int32)

    zero_rows = jnp.zeros((MOE_BLOCK * ROW_CHUNKS, LANES), F32)
    ys_rows = _experts(block_expert, first_tile, n_active, pstart, cnt,
                       cbase_t.T.reshape(-1), ntile_t.T.reshape(-1), off_t.T.reshape(-1),
                       rows, zero_rows, w_gate_up[l], b_gate_up[l][:, None, :],
                       w_down[l], b_down[l][:, None, :], n_tiles)
    out = _combine(pstart, cbase_t.reshape(-1), ntile_t.reshape(-1), off_t.reshape(-1), slots, gates,
                   x1, ys_rows)
    return out.reshape(batch, seq, d)
```

```python
import functools
import math

import jax
import jax.numpy as jnp
from jax import lax
from jax.experimental import pallas as pl
from jax.experimental.pallas import tpu as pltpu

F32 = jnp.float32
BF16 = jnp.bfloat16

D_MODEL = 1024
CHUNK = 64
EPS = 1e-6
ROPE_THETA = 10000.0
DIFF_HEADS = 4
DIFF_HEAD_DIM = 64
DIFF_V_DIM = 2 * DIFF_HEAD_DIM
N_MEM = 256
MEM_HEADS = 4
MEM_HEAD_DIM = 128
SEC = 512
N_SEC = 13
N_EXPERTS = 32
TOP_K = 4
SWIGLU_LIMIT = 7.0
SWIGLU_ALPHA = 1.702
MOE_BLOCK = 512
LANES = 128
ROW_CHUNKS = D_MODEL // LANES
LOG2E = 1.4426950408889634
NEG_BIG = -1e30

TM_PROJ = 512
TQ = 1024
ATTN_STRIP = 512
TM_POST = 256
FF_CHUNK = 256
VMEM_LIMIT = 48 * 1024 * 1024
VMEM_LIMIT_EXPERTS = 56 * 1024 * 1024


def _rms(x, eps=EPS):
    return x * lax.rsqrt(jnp.mean(x * x, axis=-1, keepdims=True) + eps)


def _mem_kv_kernel(mem_ref, g_ref, w_ref, gk_ref, k_out, v_out):
    h = (_rms(mem_ref[...]) * g_ref[...]).astype(BF16)
    kv = jnp.dot(h, w_ref[...], preferred_element_type=F32)
    for hd in range(MEM_HEADS):
        sl = slice(hd * MEM_HEAD_DIM, (hd + 1) * MEM_HEAD_DIM)
        k_out[:, sl] = (_rms(kv[:, sl]) * gk_ref[...]).astype(BF16)
    v_out[...] = kv[:, SEC:].astype(BF16)


def _mem_kv(mem2d, g_mem, w_kv, g_k):
    rows = mem2d.shape[0]
    return pl.pallas_call(
        _mem_kv_kernel,
        grid=(rows // N_MEM,),
        in_specs=[
            pl.BlockSpec((N_MEM, D_MODEL), lambda i: (i, 0)),
            pl.BlockSpec((1, D_MODEL), lambda i: (0, 0)),
            pl.BlockSpec((D_MODEL, 2 * SEC), lambda i: (0, 0)),
            pl.BlockSpec((1, MEM_HEAD_DIM), lambda i: (0, 0)),
        ],
        out_specs=[pl.BlockSpec((N_MEM, SEC), lambda i: (i, 0))] * 2,
        out_shape=[jax.ShapeDtypeStruct((rows, SEC), BF16)] * 2,
        compiler_params=pltpu.CompilerParams(vmem_limit_bytes=VMEM_LIMIT),
        name="mem_kv",
    )(mem2d, g_mem, w_kv, g_k)


def _in_proj_kernel(x_ref, g_ref, w_ref, cos_ref, sin_ref, bd_ref, gq_ref, gk_ref, gqm_ref,
                    q_out, k_out, v_out, cb_out, u_out, mq_out, sig_out):
    tm = x_ref.shape[0]
    h = (_rms(x_ref[...]) * g_ref[...]).astype(BF16)

    def proj(sec):
        return jnp.dot(h, w_ref[:, sec * SEC:(sec + 1) * SEC], preferred_element_type=F32)

    cos = jnp.concatenate([cos_ref[...]] * (SEC // LANES), axis=1)
    sin = jnp.concatenate([sin_ref[...]] * (SEC // LANES), axis=1)
    lane = lax.broadcasted_iota(jnp.int32, (tm, SEC), 1)
    first_half = (lane & (DIFF_HEAD_DIM // 2)) == 0

    def norm_rope(a, g, scale):
        ms = jnp.dot((a * a).astype(BF16), bd_ref[...], preferred_element_type=F32)
        y = a * lax.rsqrt(ms + EPS) * g
        partner = jnp.where(first_half,
                            pltpu.roll(y, SEC - DIFF_HEAD_DIM // 2, 1),
                            pltpu.roll(y, DIFF_HEAD_DIM // 2, 1))
        return (y * cos + partner * sin) * scale

    q_out[...] = norm_rope(proj(0), gq_ref[...], DIFF_HEAD_DIM ** -0.5 * LOG2E).astype(BF16)
    k_out[...] = norm_rope(proj(1), gk_ref[...], 1.0).astype(BF16)
    v = proj(2)
    for hd in range(DIFF_HEADS):
        v_out[0, hd, 0] = v[:, hd * DIFF_V_DIM:(hd + 1) * DIFF_V_DIM].T.astype(BF16)
    cb_out[...] = proj(3).astype(BF16)
    u_out[...] = proj(4) * proj(5)
    mq = proj(6)
    for hd in range(MEM_HEADS):
        sl = slice(hd * MEM_HEAD_DIM, (hd + 1) * MEM_HEAD_DIM)
        mq_out[:, sl] = (_rms(mq[:, sl]) * gqm_ref[...]
                         * (MEM_HEAD_DIM ** -0.5 * LOG2E)).astype(BF16)
    for s in range(7, N_SEC):
        a = proj(s)
        sig_out[:, (s - 7) * SEC:(s - 6) * SEC] = (1.0 / (1.0 + jnp.exp(-a))).astype(BF16)


def _in_proj(x2d, g_mix, w_in, cos_t, sin_t, bd, gq, gk, gqm, seq):
    t = x2d.shape[0]
    tm = TM_PROJ
    tiles_per_seq = seq // tm
    tiles_per_key = TQ // tm
    row = lambda i: (i, 0)
    const = lambda i: (0, 0)
    outs = [
        jax.ShapeDtypeStruct((t, SEC), BF16),
        jax.ShapeDtypeStruct((t, SEC), BF16),
        jax.ShapeDtypeStruct((t // seq, DIFF_HEADS, seq // TQ, DIFF_V_DIM, TQ), BF16),
        jax.ShapeDtypeStruct((t, SEC), BF16),
        jax.ShapeDtypeStruct((t, SEC), F32),
        jax.ShapeDtypeStruct((t, SEC), BF16),
        jax.ShapeDtypeStruct((t, 3 * D_MODEL), BF16),
    ]
    return pl.pallas_call(
        _in_proj_kernel,
        grid=(t // tm,),
        in_specs=[
            pl.BlockSpec((tm, D_MODEL), row),
            pl.BlockSpec((1, D_MODEL), const),
            pl.BlockSpec((D_MODEL, N_SEC * SEC), const),
            pl.BlockSpec((tm, LANES), lambda i: (i % tiles_per_seq, 0)),
            pl.BlockSpec((tm, LANES), lambda i: (i % tiles_per_seq, 0)),
            pl.BlockSpec((SEC, SEC), const),
            pl.BlockSpec((1, SEC), const),
            pl.BlockSpec((1, SEC), const),
            pl.BlockSpec((1, MEM_HEAD_DIM), const),
        ],
        out_specs=[
            pl.BlockSpec((1, DIFF_HEADS, 1, DIFF_V_DIM, tm),
                         lambda i: (i // tiles_per_seq, 0, (i % tiles_per_seq) // tiles_per_key,
                                    0, i % tiles_per_key))
            if o.ndim == 5 else pl.BlockSpec((tm, o.shape[1]), row) for o in outs],
        out_shape=outs,
        compiler_params=pltpu.CompilerParams(vmem_limit_bytes=VMEM_LIMIT),
        name="in_proj",
    )(x2d, g_mix, w_in, cos_t, sin_t, bd, gq, gk, gqm)


def _diff_attn_kernel(q_ref, k_ref, vt_ref, lam_ref, gs_ref, o_ref, acc_sc, s0_sc, *, lam_init):
    i = pl.program_id(2)
    tq = q_ref.shape[0]
    q = q_ref[...]
    lane = lax.broadcasted_iota(jnp.int32, q.shape, 1)
    zero = jnp.zeros_like(q)
    qq = jnp.concatenate([jnp.where(lane < DIFF_HEAD_DIM, q, zero),
                          jnp.where(lane >= DIFF_HEAD_DIM, q, zero)], axis=0)
    acc_sc[...] = jnp.zeros(acc_sc.shape, F32)
    n_strips = 2 * tq // ATTN_STRIP

    def scores(j, c, nk):
        off = pl.multiple_of(j * tq, tq)
        return lax.dot_general(k_ref[pl.ds(off, nk), :], qq[c * ATTN_STRIP:(c + 1) * ATTN_STRIP, :],
                               (((1,), (1,)), ((), ())), preferred_element_type=F32)

    def step(j, ms, ls, masked):
        q_offs = [(c * ATTN_STRIP) % tq for c in range(n_strips)]
        nks = [min(tq, qo + ATTN_STRIP) if masked else tq for qo in q_offs]
        ms_new, ls_new = [], []
        s = s0_sc[0:nks[0], :]
        for c in range(n_strips):
            cols = slice(c * ATTN_STRIP, (c + 1) * ATTN_STRIP)
            if c + 1 < n_strips:
                s_next = scores(j, c + 1, nks[c + 1])
            elif not masked:
                s0_sc[...] = scores(j + 1, 0, tq)
            if masked:
                r = lax.broadcasted_iota(jnp.int32, s.shape, 0)
                col = lax.broadcasted_iota(jnp.int32, s.shape, 1)
                s = jnp.where((r // CHUNK) <= ((col + q_offs[c]) // CHUNK), s, NEG_BIG)
            m_new = jnp.maximum(ms[c], jnp.max(s, axis=0, keepdims=True))
            p = jnp.exp2(s - m_new)
            alpha = jnp.exp2(ms[c] - m_new)
            ls_new.append(alpha * ls[c] + jnp.sum(p, axis=0, keepdims=True))
            ms_new.append(m_new)
            acc_sc[:, cols] = alpha * acc_sc[:, cols] + jnp.dot(
                vt_ref[0, 0, j][:, :nks[c]], p.astype(BF16), preferred_element_type=F32)
            if c + 1 < n_strips:
                s = s_next
        return tuple(ms_new), tuple(ls_new)

    s0_sc[...] = scores(0, 0, tq)
    m0 = tuple(jnp.full((1, ATTN_STRIP), NEG_BIG, F32) for _ in range(n_strips))
    l0 = tuple(jnp.zeros((1, ATTN_STRIP), F32) for _ in range(n_strips))
    m, l = lax.fori_loop(0, i, lambda j, c: step(j, c[0], c[1], False), (m0, l0))
    m, l = step(i, m, l, True)

    ot = acc_sc[...] / jnp.concatenate(l, axis=1)
    lam = (jnp.exp(jnp.sum(lam_ref[0:1, :] * lam_ref[1:2, :], axis=-1, keepdims=True))
           - jnp.exp(jnp.sum(lam_ref[2:3, :] * lam_ref[3:4, :], axis=-1, keepdims=True))
           + lam_init)
    d = (ot[:, :tq] - lam * ot[:, tq:]).T
    o_ref[...] = (_rms(d) * gs_ref[...] * (1.0 - lam_init)).astype(BF16)


def _diff_attn(qn, kn, vt, lam_rows, g_subln, batch, seq, lam_init):
    t = qn.shape[0]
    nq = seq // TQ
    return pl.pallas_call(
        functools.partial(_diff_attn_kernel, lam_init=lam_init),
        grid=(batch, DIFF_HEADS, nq),
        in_specs=[
            pl.BlockSpec((TQ, DIFF_V_DIM), lambda b, h, i: (b * nq + i, h)),
            pl.BlockSpec((seq, DIFF_V_DIM), lambda b, h, i: (b, h)),
            pl.BlockSpec((1, 1, nq, DIFF_V_DIM, TQ), lambda b, h, i: (b, h, 0, 0, 0)),
            pl.BlockSpec((4, DIFF_HEAD_DIM), lambda b, h, i: (0, 0)),
            pl.BlockSpec((1, DIFF_V_DIM), lambda b, h, i: (0, 0)),
        ],
        out_specs=pl.BlockSpec((TQ, DIFF_V_DIM), lambda b, h, i: (b * nq + i, h)),
        out_shape=jax.ShapeDtypeStruct((t, SEC), BF16),
        scratch_shapes=[pltpu.VMEM((DIFF_V_DIM, 2 * TQ), F32), pltpu.VMEM((TQ, ATTN_STRIP), F32)],
        compiler_params=pltpu.CompilerParams(vmem_limit_bytes=VMEM_LIMIT),
        name="diff_attn",
    )(qn, kn, vt, lam_rows, g_subln)


def _row_span(ref, row, n):
    return ref.at[pl.ds(pl.multiple_of(row * ROW_CHUNKS, ROW_CHUNKS), n * ROW_CHUNKS), :]


def _chunk_rows(row0, n, c):
    return pl.ds(row0 * ROW_CHUNKS + c, n, stride=ROW_CHUNKS)


def _load_rows(ref, row0, n):
    return jnp.concatenate([ref[_chunk_rows(row0, n, c), :] for c in range(ROW_CHUNKS)], axis=1)


def _store_rows(ref, n, val):
    for c in range(ROW_CHUNKS):
        ref[_chunk_rows(0, n, c), :] = val[:, c * LANES:(c + 1) * LANES]


def _copy_run(src_hbm, buf, sem, src_row, dst_row, n, max_n):
    piece = 1 << (max_n.bit_length() - 1)
    while piece >= 1:
        take = (n & piece) != 0

        @pl.when(take)
        def _(piece=piece, src_row=src_row, dst_row=dst_row):
            pltpu.make_async_copy(_row_span(src_hbm, src_row, piece), _row_span(buf, dst_row, piece),
                                  sem).start()

        step = jnp.where(take, piece, 0)
        src_row = src_row + step
        dst_row = dst_row + step
        piece //= 2


def _wait_rows(n, src_hbm, buf, sem, base):
    pltpu.make_async_copy(_row_span(src_hbm, 0, n), _row_span(buf, base, n), sem).wait()


def _post_kernel(x_ref, yd_ref, cb_ref, u_ref, up_ref, mq_ref, sig_ref, km_ref, vm_ref,
                 cw_ref, wb_ref, wo_ref, gf_ref, wr_ref, br_ref, tri_ref, ltri_ref,
                 x1_out, rows_out, slot_out, gate_out, cbase_out, ntile_out, off_out, cnt_out, carry_sc,
                 *, seq):
    i = pl.program_id(0)
    tm = x_ref.shape[0]

    @pl.when(i == 0)
    def _():
        carry_sc[...] = jnp.zeros(carry_sc.shape, F32)

    u = u_ref[...]
    seq_start = (i * tm) % seq == 0
    up = jnp.where(seq_start, 0.0, up_ref[...])
    r = lax.broadcasted_iota(jnp.int32, u.shape, 0)
    u1 = jnp.where(r == 0, up[7:8, :], pltpu.roll(u, 1, 0))
    u2 = jnp.where(r == 0, up[6:7, :], jnp.where(r == 1, up[7:8, :], pltpu.roll(u, 2, 0)))
    y_conv = cb_ref[...].astype(F32) * (cw_ref[0:1, :] * u2 + cw_ref[1:2, :] * u1 + cw_ref[2:3, :] * u)

    def mem_scores(hd):
        sl = slice(hd * MEM_HEAD_DIM, (hd + 1) * MEM_HEAD_DIM)
        return lax.dot_general(mq_ref[:, sl], km_ref[:, sl], (((1,), (1,)), ((), ())),
                               preferred_element_type=F32)

    y_mem = []
    s_next = mem_scores(0)
    for hd in range(MEM_HEADS):
        sl = slice(hd * MEM_HEAD_DIM, (hd + 1) * MEM_HEAD_DIM)
        s = s_next
        if hd + 1 < MEM_HEADS:
            s_next = mem_scores(hd + 1)
        if hd == 0:
            merged = (sig_ref[:, 0:D_MODEL].astype(F32)
                      * jnp.dot(yd_ref[...], wb_ref[0], preferred_element_type=F32))
        elif hd == 1:
            merged += (sig_ref[:, D_MODEL:2 * D_MODEL].astype(F32)
                       * jnp.dot(y_conv.astype(BF16), wb_ref[1], preferred_element_type=F32))
        p = jnp.exp2(s - jnp.max(s, axis=-1, keepdims=True))
        o = jnp.dot(p.astype(BF16), vm_ref[:, sl], preferred_element_type=F32)
        y_mem.append(o / jnp.sum(p, axis=-1, keepdims=True))

    merged += (sig_ref[:, 2 * D_MODEL:3 * D_MODEL].astype(F32)
               * jnp.dot(jnp.concatenate(y_mem, axis=1).astype(BF16), wb_ref[2],
                         preferred_element_type=F32))
    x1 = x_ref[...] + jnp.dot(merged.astype(BF16), wo_ref[...], preferred_element_type=F32)
    x1_out[...] = x1

    h2 = _rms(x1) * gf_ref[...]

    nt = (((1,), (1,)), ((), ()))
    w = wr_ref[...]
    w_hi = w.astype(BF16)
    w_lo = (w - w_hi.astype(F32)).astype(BF16)
    h_hi = h2.astype(BF16)
    h_lo = (h2 - h_hi.astype(F32)).astype(BF16)
    part = lax.dot_general(jnp.concatenate([w_hi, w_lo], axis=0), h_hi, nt,
                           preferred_element_type=F32)
    logits = (part[:N_EXPERTS] + part[N_EXPERTS:]
              + lax.dot_general(w_hi, h_lo, nt, preferred_element_type=F32) + br_ref[...])
    eio = lax.broadcasted_iota(jnp.int32, logits.shape, 0)
    work = logits
    vals, hots = [], []
    for k in range(TOP_K):
        mk = jnp.max(work, axis=0, keepdims=True)
        ik = jnp.min(jnp.where(work == mk, eio, N_EXPERTS), axis=0, keepdims=True)
        hot = eio == ik
        work = jnp.where(hot, -jnp.inf, work)
        vals.append(mk)
        hots.append(hot)
    ex = [jnp.exp(v - vals[0]) for v in vals]
    den = ex[0] + ex[1] + ex[2] + ex[3]
    gates = [e / den for e in ex]

    assign = jnp.zeros(logits.shape, F32)
    for hot in hots:
        assign = jnp.where(hot, 1.0, assign)
    earlier = jnp.dot(assign.astype(BF16), tri_ref[...], preferred_element_type=F32)
    n_col = jnp.sum(assign, axis=1, keepdims=True)
    off_col = jnp.dot(ltri_ref[...], jnp.broadcast_to(n_col, (N_EXPERTS, LANES)),
                      precision=lax.Precision.HIGHEST, preferred_element_type=F32)[:, 0:1]
    slots = [jnp.sum(jnp.where(hot, earlier + off_col, 0.0), axis=0, keepdims=True).astype(jnp.int32)
             for hot in hots]
    for k in range(TOP_K):
        slot_out[k:k + 1, :] = slots[k]
        gate_out[k:k + 1, :] = gates[k]

    n_slots = TOP_K * tm
    jdx = lax.broadcasted_iota(jnp.int32, (n_slots, tm), 0)
    pick = jnp.where(jdx == slots[0], 1.0, jnp.where(jdx == slots[1], 1.0, jnp.where(
        jdx == slots[2], 1.0, jnp.where(jdx == slots[3], 1.0, 0.0)))).astype(BF16)
    rows = jnp.dot(pick, h_hi, preferred_element_type=F32)
    _store_rows(rows_out, n_slots, rows)

    cbase_out[...] = jnp.broadcast_to(carry_sc[...], cbase_out.shape).astype(jnp.int32)
    ntile_out[...] = jnp.broadcast_to(n_col, ntile_out.shape).astype(jnp.int32)
    off_out[...] = jnp.broadcast_to(off_col, off_out.shape).astype(jnp.int32)
    carry_sc[...] = carry_sc[...] + n_col
    cnt_out[...] = jnp.broadcast_to(carry_sc[...], cnt_out.shape).astype(jnp.int32)


def _post(x2d, y_diff, cb, u, mqn, sig, km, vm, conv_w, w_branch, w_out, g_ffn, w_rt, b_r, tri, ltri,
          seq):
    t = x2d.shape[0]
    tm = TM_POST
    n_tiles = t // tm
    row = lambda i: (i, 0)
    const = lambda i: (0, 0)
    table = jax.ShapeDtypeStruct((n_tiles * N_EXPERTS, LANES), jnp.int32)
    outs = [
        jax.ShapeDtypeStruct((t, D_MODEL), F32),
        jax.ShapeDtypeStruct((TOP_K * t * ROW_CHUNKS, LANES), F32),
        jax.ShapeDtypeStruct((TOP_K, t), jnp.int32),
        jax.ShapeDtypeStruct((TOP_K, t), F32),
        table, table, table,
        jax.ShapeDtypeStruct((N_EXPERTS, LANES), jnp.int32),
    ]
    return pl.pallas_call(
        functools.partial(_post_kernel, seq=seq),
        grid=(t // tm,),
        in_specs=[
            pl.BlockSpec((tm, D_MODEL), row),
            pl.BlockSpec((tm, SEC), row),
            pl.BlockSpec((tm, SEC), row),
            pl.BlockSpec((tm, SEC), row),
            pl.BlockSpec((8, SEC), lambda i: (jnp.maximum(i * (tm // 8) - 1, 0), 0)),
            pl.BlockSpec((tm, SEC), row),
            pl.BlockSpec((tm, 3 * D_MODEL), row),
            pl.BlockSpec((N_MEM, SEC), lambda i: ((i * tm) // seq, 0)),
            pl.BlockSpec((N_MEM, SEC), lambda i: ((i * tm) // seq, 0)),
            pl.BlockSpec((3, SEC), const),
            pl.BlockSpec((3, SEC, D_MODEL), lambda i: (0, 0, 0)),
            pl.BlockSpec((D_MODEL, D_MODEL), const),
            pl.BlockSpec((1, D_MODEL), const),
            pl.BlockSpec((N_EXPERTS, D_MODEL), const),
            pl.BlockSpec((N_EXPERTS, 1), const),
            pl.BlockSpec((tm, tm), const),
            pl.BlockSpec((N_EXPERTS, N_EXPERTS), const),
        ],
        out_specs=[
            pl.BlockSpec((tm, D_MODEL), row),
            pl.BlockSpec((TOP_K * tm * ROW_CHUNKS, LANES), row),
            pl.BlockSpec((TOP_K, tm), lambda i: (0, i)),
            pl.BlockSpec((TOP_K, tm), lambda i: (0, i)),
            pl.BlockSpec((N_EXPERTS, LANES), row),
            pl.BlockSpec((N_EXPERTS, LANES), row),
            pl.BlockSpec((N_EXPERTS, LANES), row),
            pl.BlockSpec((N_EXPERTS, LANES), const),
        ],
        out_shape=outs,
        scratch_shapes=[pltpu.VMEM((N_EXPERTS, 1), F32)],
        compiler_params=pltpu.CompilerParams(dimension_semantics=("arbitrary",),
                                             vmem_limit_bytes=VMEM_LIMIT),
        name="post",
    )(x2d, y_diff, cb, u, u, mqn, sig, km, vm, conv_w, w_branch, w_out, g_ffn, w_rt, b_r, tri, ltri)


def _experts_kernel(be_ref, t0_ref, na_ref, ps_ref, cnt_ref, cb_ref, nt_ref, of_ref,
                    rows_hbm, zeros_hbm, wgu_f32, bgu_ref, wd_f32, bd_ref,
                    ys_out, xbuf, sem, wgu_ref, wd_ref, *, n_tiles, slots_per_tile):
    b = pl.program_id(0)
    n_active = na_ref[0]
    slot = b % 2

    @pl.when(jnp.logical_and(b < n_active,
                             jnp.logical_or(b == 0, be_ref[b] != be_ref[jnp.maximum(b - 1, 0)])))
    def _():
        wgu_ref[...] = wgu_f32[0].astype(BF16)
        wd_ref[...] = wd_f32[0].astype(BF16)

    def issue(blk, buf_slot):
        e = be_ref[blk]
        r0 = blk * MOE_BLOCK - ps_ref[e]
        r1 = r0 + MOE_BLOCK
        base = buf_slot * MOE_BLOCK
        sm = sem.at[buf_slot]

        def cond(i):
            return jnp.logical_and(i < n_tiles, cb_ref[e * n_tiles + jnp.minimum(i, n_tiles - 1)] < r1)

        def body(i):
            c = cb_ref[e * n_tiles + i]
            lo = jnp.maximum(c, r0)
            hi = jnp.minimum(c + nt_ref[e * n_tiles + i], r1)
            src = i * slots_per_tile + of_ref[e * n_tiles + i] + (lo - c)
            _copy_run(rows_hbm, xbuf, sm, src, base + (lo - r0), jnp.maximum(hi - lo, 0),
                      min(TM_POST, MOE_BLOCK))
            return i + 1

        lax.while_loop(cond, body, t0_ref[blk])
        valid = jnp.clip(cnt_ref[e] - r0, 0, MOE_BLOCK)
        _copy_run(zeros_hbm, xbuf, sm, 0, base + valid, MOE_BLOCK - valid, MOE_BLOCK)

    @pl.when(b == 0)
    def _():
        issue(b, 0)

    @pl.when(b + 1 < n_active)
    def _():
        issue(b + 1, 1 - slot)

    def compute(n_rows):
        base = pl.multiple_of(slot * MOE_BLOCK, MOE_BLOCK)
        _wait_rows(MOE_BLOCK, rows_hbm, xbuf, sem.at[slot], base)
        x = _load_rows(xbuf, base, n_rows).astype(BF16)
        d_ff = wd_ref.shape[0]

        def gate_up(j):
            gs = slice(j * FF_CHUNK, (j + 1) * FF_CHUNK)
            us = slice(d_ff + j * FF_CHUNK, d_ff + (j + 1) * FF_CHUNK)
            return (jnp.dot(x, wgu_ref[:, gs], preferred_element_type=F32) + bgu_ref[0, :, gs],
                    jnp.dot(x, wgu_ref[:, us], preferred_element_type=F32) + bgu_ref[0, :, us])

        acts = []
        nxt = gate_up(0)
        for j in range(d_ff // FF_CHUNK):
            g, u = nxt
            if (j + 1) * FF_CHUNK < d_ff:
                nxt = gate_up(j + 1)
            g = jnp.minimum(g, SWIGLU_LIMIT)
            u = jnp.clip(u, -SWIGLU_LIMIT, SWIGLU_LIMIT)
            acts.append(((u + 1.0) * (g * (1.0 / (1.0 + jnp.exp(-SWIGLU_ALPHA * g))))).astype(BF16))
        act = jnp.concatenate(acts, axis=1)
        for n in range(D_MODEL // FF_CHUNK):
            cols = slice(n * FF_CHUNK, (n + 1) * FF_CHUNK)
            yn = jnp.dot(act, wd_ref[:, cols], preferred_element_type=F32) + bd_ref[0, :, cols]
            for c in range(FF_CHUNK // LANES):
                ys_out[_chunk_rows(0, n_rows, n * (FF_CHUNK // LANES) + c), :] = (
                    yn[:, c * LANES:(c + 1) * LANES])
        if n_rows < MOE_BLOCK:
            ys_out[n_rows * ROW_CHUNKS:, :] = jnp.zeros(((MOE_BLOCK - n_rows) * ROW_CHUNKS, LANES), F32)

    @pl.when(b < n_active)
    def _():
        compute(MOE_BLOCK)

    @pl.when(b >= n_active)
    def _():
        ys_out[...] = jnp.zeros(ys_out.shape, F32)


def _experts(block_expert, first_tile, n_active, pstart, counts, cbase_e, ntile_e, off_e,
             rows, zero_rows, wgu, bgu, wd, bd, n_tiles):
    n_blocks = block_expert.shape[0]
    ff2 = wgu.shape[2]
    by_expert = lambda b, be, *_: (be[b], 0, 0)
    grid_spec = pltpu.PrefetchScalarGridSpec(
        num_scalar_prefetch=8,
        grid=(n_blocks,),
        in_specs=[
            pl.BlockSpec(memory_space=pl.ANY),
            pl.BlockSpec(memory_space=pl.ANY),
            pl.BlockSpec((1, D_MODEL, ff2), by_expert),
            pl.BlockSpec((1, 1, ff2), by_expert),
            pl.BlockSpec((1, ff2 // 2, D_MODEL), by_expert),
            pl.BlockSpec((1, 1, D_MODEL), by_expert),
        ],
        out_specs=pl.BlockSpec((MOE_BLOCK * ROW_CHUNKS, LANES), lambda b, *_: (b, 0)),
        scratch_shapes=[
            pltpu.VMEM((2 * MOE_BLOCK * ROW_CHUNKS, LANES), F32),
            pltpu.SemaphoreType.DMA((2,)),
            pltpu.VMEM((D_MODEL, ff2), BF16),
            pltpu.VMEM((ff2 // 2, D_MODEL), BF16),
        ],
    )
    return pl.pallas_call(
        functools.partial(_experts_kernel, n_tiles=n_tiles, slots_per_tile=TOP_K * TM_POST),
        grid_spec=grid_spec,
        out_shape=jax.ShapeDtypeStruct((n_blocks * MOE_BLOCK * ROW_CHUNKS, LANES), F32),
        compiler_params=pltpu.CompilerParams(dimension_semantics=("arbitrary",),
                                             vmem_limit_bytes=VMEM_LIMIT_EXPERTS),
        name="experts",
    )(block_expert, first_tile, n_active, pstart, counts, cbase_e, ntile_e, off_e,
      rows, zero_rows, wgu, bgu, wd, bd)


def _combine_kernel(ps_ref, cb_ref, nt_ref, of_ref, slot_ref, gate_ref, x1_ref, ys_hbm, o_ref, buf,
                    sem):
    i = pl.program_id(0)
    n = pl.num_programs(0)
    tm = x1_ref.shape[0]
    n_slots = TOP_K * tm
    cur = i % 2

    def issue(tile, buf_slot):
        def body(e, carry):
            k = tile * N_EXPERTS + e
            _copy_run(ys_hbm, buf, sem.at[buf_slot], ps_ref[e] + cb_ref[k],
                      buf_slot * n_slots + of_ref[k], nt_ref[k], tm)
            return carry
        lax.fori_loop(0, N_EXPERTS, body, 0)

    @pl.when(i == 0)
    def _():
        issue(i, 0)

    @pl.when(i + 1 < n)
    def _():
        issue(i + 1, 1 - cur)

    base = pl.multiple_of(cur * n_slots, n_slots)
    _wait_rows(n_slots, ys_hbm, buf, sem.at[cur], base)
    y = _load_rows(buf, base, n_slots).astype(BF16)
    jdx = lax.broadcasted_iota(jnp.int32, (n_slots, tm), 0)
    pick = jnp.where(jdx == slot_ref[0:1, :], gate_ref[0:1, :], jnp.where(
        jdx == slot_ref[1:2, :], gate_ref[1:2, :], jnp.where(
            jdx == slot_ref[2:3, :], gate_ref[2:3, :], jnp.where(
                jdx == slot_ref[3:4, :], gate_ref[3:4, :], 0.0)))).astype(BF16)
    o_ref[...] = x1_ref[...] + lax.dot_general(pick, y, (((0,), (0,)), ((), ())),
                                               preferred_element_type=F32)


def _combine(pstart, cbase_t, ntile_t, off_t, slots, gates, x1, ys_rows):
    t = x1.shape[0]
    tm = TM_POST
    n_slots = TOP_K * tm
    grid_spec = pltpu.PrefetchScalarGridSpec(
        num_scalar_prefetch=4,
        grid=(t // tm,),
        in_specs=[
            pl.BlockSpec((TOP_K, tm), lambda i, *_: (0, i)),
            pl.BlockSpec((TOP_K, tm), lambda i, *_: (0, i)),
            pl.BlockSpec((tm, D_MODEL), lambda i, *_: (i, 0)),
            pl.BlockSpec(memory_space=pl.ANY),
        ],
        out_specs=pl.BlockSpec((tm, D_MODEL), lambda i, *_: (i, 0)),
        scratch_shapes=[
            pltpu.VMEM((2 * n_slots * ROW_CHUNKS, LANES), F32),
            pltpu.SemaphoreType.DMA((2,)),
        ],
    )
    return pl.pallas_call(
        _combine_kernel,
        grid_spec=grid_spec,
        out_shape=jax.ShapeDtypeStruct((t, D_MODEL), F32),
        compiler_params=pltpu.CompilerParams(dimension_semantics=("arbitrary",),
                                             vmem_limit_bytes=VMEM_LIMIT),
        name="combine",
    )(pstart, cbase_t, ntile_t, off_t, slots, gates, x1, ys_rows)


def kernel(x, mem, g_mix, w_in, g_q_diff, g_k_diff, lambda_q1, lambda_k1, lambda_q2, lambda_k2,
           g_subln, conv_w, g_mem, w_mem_kv, g_q_mem, g_k_mem, w_branch, w_out, g_ffn, w_router,
           b_router, w_gate_up, b_gate_up, w_down, b_down):
    batch, seq, d = x.shape
    t = batch * seq
    depth = g_mix.shape[0]
    assert depth == 1 and d == D_MODEL and seq % TQ == 0 and t % TM_PROJ == 0

    pos = jnp.arange(seq, dtype=F32)
    inv_freq = 1.0 / (ROPE_THETA ** (jnp.arange(0, DIFF_HEAD_DIM, 2, dtype=F32) / DIFF_HEAD_DIM))
    ang = pos[:, None] * inv_freq[None, :]
    cos_t = jnp.tile(jnp.cos(ang), (1, 4))
    sin_t = jnp.tile(jnp.concatenate([-jnp.sin(ang), jnp.sin(ang)], axis=1), (1, 2))
    grp = jnp.arange(SEC) // DIFF_HEAD_DIM
    bd = jnp.where(grp[:, None] == grp[None, :], 1.0 / DIFF_HEAD_DIM, 0.0).astype(BF16)
    tok = jnp.arange(TM_POST)
    tri = (tok[:, None] < tok[None, :]).astype(BF16)

    x2d = x.reshape(t, d)
    l = 0
    lam_init = 0.8 - 0.6 * math.exp(-0.3 * l)
    km, vm = _mem_kv(mem.reshape(batch * N_MEM, d), g_mem[l][None, :], w_mem_kv[l].astype(BF16),
                     g_k_mem[l][None, :])
    qn, kn, vt, cb, u, mqn, sig = _in_proj(
        x2d, g_mix[l][None, :], w_in[l].astype(BF16), cos_t, sin_t, bd,
        jnp.tile(g_q_diff[l], SEC // DIFF_HEAD_DIM)[None, :],
        jnp.tile(g_k_diff[l], SEC // DIFF_HEAD_DIM)[None, :],
        g_q_mem[l][None, :], seq)
    lam_rows = jnp.stack([lambda_q1[l], lambda_k1[l], lambda_q2[l], lambda_k2[l]]).astype(F32)
    y_diff = _diff_attn(qn, kn, vt, lam_rows, g_subln[l][None, :], batch, seq, lam_init)
    ex = jnp.arange(N_EXPERTS)
    ltri = (ex[None, :] < ex[:, None]).astype(F32)
    x1, rows, slots, gates, cbase, ntile, off, counts = _post(
        x2d, y_diff, cb, u, mqn, sig, km, vm, conv_w[l], w_branch[l].astype(BF16),
        w_out[l].astype(BF16), g_ffn[l][None, :], w_router[l].T, b_router[l][:, None], tri, ltri, seq)

    n_tiles = t // TM_POST
    n_assign = t * TOP_K
    n_rows = -(-(n_assign + N_EXPERTS * (MOE_BLOCK - 1)) // MOE_BLOCK) * MOE_BLOCK
    n_blocks = n_rows // MOE_BLOCK
    cnt = counts[:, 0]
    padded = (cnt + MOE_BLOCK - 1) // MOE_BLOCK * MOE_BLOCK
    pend = jnp.cumsum(padded)
    pstart = pend - padded
    block_row0 = jnp.arange(n_blocks, dtype=jnp.int32) * MOE_BLOCK
    block_expert = jnp.minimum(
        jnp.sum((pend[None, :] <= block_row0[:, None]).astype(jnp.int32), axis=1), N_EXPERTS - 1)
    n_active = (pend[-1:] // MOE_BLOCK).astype(jnp.int32)
    cbase_t = cbase[:, 0].reshape(n_tiles, N_EXPERTS)
    ntile_t = ntile[:, 0].reshape(n_tiles, N_EXPERTS)
    off_t = off[:, 0].reshape(n_tiles, N_EXPERTS)
    cend_b = (cbase_t + ntile_t).T[block_expert]
    r0 = block_row0 - pstart[block_expert]
    first_tile = jnp.sum((cend_b <= r0[:, None]).astype(jnp.int32), axis=1)
    first_tile = jnp.minimum(first_tile, n_tiles - 1).astype(jnp.int32)

    zero_rows = jnp.zeros((MOE_BLOCK * ROW_CHUNKS, LANES), F32)
    ys_rows = _experts(block_expert, first_tile, n_active, pstart, cnt,
                       cbase_t.T.reshape(-1), ntile_t.T.reshape(-1), off_t.T.reshape(-1),
                       rows, zero_rows, w_gate_up[l], b_gate_up[l][:, None, :],
                       w_down[l], b_down[l][:, None, :], n_tiles)
    out = _combine(pstart, cbase_t.reshape(-1), ntile_t.reshape(-1), off_t.reshape(-1), slots, gates,
                   x1, ys_rows)
    return out.reshape(batch, seq, d)
```

```python
import functools
import math

import jax
import jax.numpy as jnp
from jax import lax
from jax.experimental import pallas as pl
from jax.experimental.pallas import tpu as pltpu

F32 = jnp.float32
BF16 = jnp.bfloat16

D_MODEL = 1024
CHUNK = 64
EPS = 1e-6
ROPE_THETA = 10000.0
DIFF_HEADS = 4
DIFF_HEAD_DIM = 64
DIFF_V_DIM = 2 * DIFF_HEAD_DIM
N_MEM = 256
MEM_HEADS = 4
MEM_HEAD_DIM = 128
SEC = 512
N_SEC = 13
N_EXPERTS = 32
TOP_K = 4
SWIGLU_LIMIT = 7.0
SWIGLU_ALPHA = 1.702
MOE_BLOCK = 512
LANES = 128
ROW_CHUNKS = D_MODEL // LANES
LOG2E = 1.4426950408889634
NEG_BIG = -1e30

TM_PROJ = 512
TQ = 1024
ATTN_STRIP = 512
MAX_FIXED_SHIFT = 40.0
TM_POST = 256
FF_CHUNK = 256
VMEM_LIMIT = 48 * 1024 * 1024
VMEM_LIMIT_EXPERTS = 56 * 1024 * 1024


def _rms(x, eps=EPS):
    return x * lax.rsqrt(jnp.mean(x * x, axis=-1, keepdims=True) + eps)


def _mem_kv_kernel(mem_ref, g_ref, w_ref, gk_ref, k_out, v_out):
    h = (_rms(mem_ref[...]) * g_ref[...]).astype(BF16)
    kv = jnp.dot(h, w_ref[...], preferred_element_type=F32)
    for hd in range(MEM_HEADS):
        sl = slice(hd * MEM_HEAD_DIM, (hd + 1) * MEM_HEAD_DIM)
        k_out[:, sl] = (_rms(kv[:, sl]) * gk_ref[...]).astype(BF16)
    v_out[...] = kv[:, SEC:].astype(BF16)


def _mem_kv(mem2d, g_mem, w_kv, g_k):
    rows = mem2d.shape[0]
    return pl.pallas_call(
        _mem_kv_kernel,
        grid=(rows // N_MEM,),
        in_specs=[
            pl.BlockSpec((N_MEM, D_MODEL), lambda i: (i, 0)),
            pl.BlockSpec((1, D_MODEL), lambda i: (0, 0)),
            pl.BlockSpec((D_MODEL, 2 * SEC), lambda i: (0, 0)),
            pl.BlockSpec((1, MEM_HEAD_DIM), lambda i: (0, 0)),
        ],
        out_specs=[pl.BlockSpec((N_MEM, SEC), lambda i: (i, 0))] * 2,
        out_shape=[jax.ShapeDtypeStruct((rows, SEC), BF16)] * 2,
        compiler_params=pltpu.CompilerParams(vmem_limit_bytes=VMEM_LIMIT),
        name="mem_kv",
    )(mem2d, g_mem, w_kv, g_k)


def _in_proj_kernel(x_ref, g_ref, w_ref, cos_ref, sin_ref, bd_ref, gq_ref, gk_ref, gqm_ref,
                    q_out, k_out, v_out, cb_out, u_out, mq_out, sig_out):
    tm = x_ref.shape[0]
    h = (_rms(x_ref[...]) * g_ref[...]).astype(BF16)

    def proj(sec):
        return jnp.dot(h, w_ref[:, sec * SEC:(sec + 1) * SEC], preferred_element_type=F32)

    cos = jnp.concatenate([cos_ref[...]] * (SEC // LANES), axis=1)
    sin = jnp.concatenate([sin_ref[...]] * (SEC // LANES), axis=1)
    lane = lax.broadcasted_iota(jnp.int32, (tm, SEC), 1)
    first_half = (lane & (DIFF_HEAD_DIM // 2)) == 0

    def norm_rope(a, g, scale):
        ms = jnp.dot((a * a).astype(BF16), bd_ref[...], preferred_element_type=F32)
        y = a * lax.rsqrt(ms + EPS) * g
        partner = jnp.where(first_half,
                            pltpu.roll(y, SEC - DIFF_HEAD_DIM // 2, 1),
                            pltpu.roll(y, DIFF_HEAD_DIM // 2, 1))
        return (y * cos + partner * sin) * scale

    q_out[...] = norm_rope(proj(0), gq_ref[...], DIFF_HEAD_DIM ** -0.5 * LOG2E).astype(BF16)
    k_out[...] = norm_rope(proj(1), gk_ref[...], 1.0).astype(BF16)
    v = proj(2)
    for hd in range(DIFF_HEADS):
        v_out[0, hd, 0] = v[:, hd * DIFF_V_DIM:(hd + 1) * DIFF_V_DIM].T.astype(BF16)
    cb_out[...] = proj(3).astype(BF16)
    u_out[...] = proj(4) * proj(5)
    mq = proj(6)
    for hd in range(MEM_HEADS):
        sl = slice(hd * MEM_HEAD_DIM, (hd + 1) * MEM_HEAD_DIM)
        mq_out[:, sl] = (_rms(mq[:, sl]) * gqm_ref[...]
                         * (MEM_HEAD_DIM ** -0.5 * LOG2E)).astype(BF16)
    for s in range(7, N_SEC):
        a = proj(s)
        sig_out[:, (s - 7) * SEC:(s - 6) * SEC] = (1.0 / (1.0 + jnp.exp(-a))).astype(BF16)


def _in_proj(x2d, g_mix, w_in, cos_t, sin_t, bd, gq, gk, gqm, seq):
    t = x2d.shape[0]
    tm = TM_PROJ
    tiles_per_seq = seq // tm
    tiles_per_key = TQ // tm
    row = lambda i: (i, 0)
    const = lambda i: (0, 0)
    outs = [
        jax.ShapeDtypeStruct((t, SEC), BF16),
        jax.ShapeDtypeStruct((t, SEC), BF16),
        jax.ShapeDtypeStruct((t // seq, DIFF_HEADS, seq // TQ, DIFF_V_DIM, TQ), BF16),
        jax.ShapeDtypeStruct((t, SEC), BF16),
        jax.ShapeDtypeStruct((t, SEC), F32),
        jax.ShapeDtypeStruct((t, SEC), BF16),
        jax.ShapeDtypeStruct((t, 3 * D_MODEL), BF16),
    ]
    return pl.pallas_call(
        _in_proj_kernel,
        grid=(t // tm,),
        in_specs=[
            pl.BlockSpec((tm, D_MODEL), row),
            pl.BlockSpec((1, D_MODEL), const),
            pl.BlockSpec((D_MODEL, N_SEC * SEC), const),
            pl.BlockSpec((tm, LANES), lambda i: (i % tiles_per_seq, 0)),
            pl.BlockSpec((tm, LANES), lambda i: (i % tiles_per_seq, 0)),
            pl.BlockSpec((SEC, SEC), const),
            pl.BlockSpec((1, SEC), const),
            pl.BlockSpec((1, SEC), const),
            pl.BlockSpec((1, MEM_HEAD_DIM), const),
        ],
        out_specs=[
            pl.BlockSpec((1, DIFF_HEADS, 1, DIFF_V_DIM, tm),
                         lambda i: (i // tiles_per_seq, 0, (i % tiles_per_seq) // tiles_per_key,
                                    0, i % tiles_per_key))
            if o.ndim == 5 else pl.BlockSpec((tm, o.shape[1]), row) for o in outs],
        out_shape=outs,
        compiler_params=pltpu.CompilerParams(vmem_limit_bytes=VMEM_LIMIT),
        name="in_proj",
    )(x2d, g_mix, w_in, cos_t, sin_t, bd, gq, gk, gqm)


def _diff_attn_kernel(fixed_ref, q_ref, k_ref, vt_ref, lam_ref, gs_ref, bound_ref, o_ref,
                      acc_sc, s0_sc, l_sc, *, lam_init):
    i = pl.program_id(2)
    tq = q_ref.shape[0]
    q = q_ref[...]
    lane = lax.broadcasted_iota(jnp.int32, q.shape, 1)
    zero = jnp.zeros_like(q)
    qq = jnp.concatenate([jnp.where(lane < DIFF_HEAD_DIM, q, zero),
                          jnp.where(lane >= DIFF_HEAD_DIM, q, zero)], axis=0)
    acc_sc[...] = jnp.zeros(acc_sc.shape, F32)
    n_strips = 2 * tq // ATTN_STRIP

    def scores(j, c, nk):
        off = pl.multiple_of(j * tq, tq)
        return lax.dot_general(k_ref[pl.ds(off, nk), :], qq[c * ATTN_STRIP:(c + 1) * ATTN_STRIP, :],
                               (((1,), (1,)), ((), ())), preferred_element_type=F32)

    def step(j, ms, ls, masked, fixed):
        q_offs = [(c * ATTN_STRIP) % tq for c in range(n_strips)]
        nks = [min(tq, qo + ATTN_STRIP) if masked else tq for qo in q_offs]
        ms_new, ls_new = [], []
        s = s0_sc[0:nks[0], :]
        for c in range(n_strips):
            cols = slice(c * ATTN_STRIP, (c + 1) * ATTN_STRIP)
            if c + 1 < n_strips:
                s_next = scores(j, c + 1, nks[c + 1])
            elif not masked:
                s0_sc[...] = scores(j + 1, 0, tq)
            if masked:
                r = lax.broadcasted_iota(jnp.int32, s.shape, 0)
                col = lax.broadcasted_iota(jnp.int32, s.shape, 1)
                s = jnp.where((r // CHUNK) <= ((col + q_offs[c]) // CHUNK), s, NEG_BIG)
            if fixed:
                p = jnp.exp2(s - bound_ref[...])
                ls_new.append(ls[c] + jnp.sum(p, axis=0, keepdims=True))
                ms_new.append(ms[c])
                acc_sc[:, cols] = acc_sc[:, cols] + jnp.dot(
                    vt_ref[0, 0, j][:, :nks[c]], p.astype(BF16), preferred_element_type=F32)
            else:
                m_new = jnp.maximum(ms[c], jnp.max(s, axis=0, keepdims=True))
                p = jnp.exp2(s - m_new)
                alpha = jnp.exp2(ms[c] - m_new)
                ls_new.append(alpha * ls[c] + jnp.sum(p, axis=0, keepdims=True))
                ms_new.append(m_new)
                acc_sc[:, cols] = alpha * acc_sc[:, cols] + jnp.dot(
                    vt_ref[0, 0, j][:, :nks[c]], p.astype(BF16), preferred_element_type=F32)
            if c + 1 < n_strips:
                s = s_next
        return tuple(ms_new), tuple(ls_new)

    def run(fixed):
        s0_sc[...] = scores(0, 0, tq)
        m0 = tuple(jnp.full((1, ATTN_STRIP), NEG_BIG, F32) for _ in range(n_strips))
        l0 = tuple(jnp.zeros((1, ATTN_STRIP), F32) for _ in range(n_strips))
        m, l = lax.fori_loop(0, i, lambda j, c: step(j, c[0], c[1], False, fixed), (m0, l0))
        m, l = step(i, m, l, True, fixed)
        l_sc[...] = jnp.concatenate(l, axis=1)

    @pl.when(fixed_ref[0] == 1)
    def _():
        run(True)

    @pl.when(fixed_ref[0] != 1)
    def _():
        run(False)

    ot = acc_sc[...] / l_sc[...]
    lam = (jnp.exp(jnp.sum(lam_ref[0:1, :] * lam_ref[1:2, :], axis=-1, keepdims=True))
           - jnp.exp(jnp.sum(lam_ref[2:3, :] * lam_ref[3:4, :], axis=-1, keepdims=True))
           + lam_init)
    d = (ot[:, :tq] - lam * ot[:, tq:]).T
    o_ref[...] = (_rms(d) * gs_ref[...] * (1.0 - lam_init)).astype(BF16)


def _diff_attn(qn, kn, vt, lam_rows, g_subln, score_bound, batch, seq, lam_init):
    t = qn.shape[0]
    nq = seq // TQ
    use_fixed = (score_bound <= MAX_FIXED_SHIFT).astype(jnp.int32).reshape(1)
    grid_spec = pltpu.PrefetchScalarGridSpec(
        num_scalar_prefetch=1,
        grid=(batch, DIFF_HEADS, nq),
        in_specs=[
            pl.BlockSpec((TQ, DIFF_V_DIM), lambda b, h, i, *_: (b * nq + i, h)),
            pl.BlockSpec((seq, DIFF_V_DIM), lambda b, h, i, *_: (b, h)),
            pl.BlockSpec((1, 1, nq, DIFF_V_DIM, TQ), lambda b, h, i, *_: (b, h, 0, 0, 0)),
            pl.BlockSpec((4, DIFF_HEAD_DIM), lambda b, h, i, *_: (0, 0)),
            pl.BlockSpec((1, DIFF_V_DIM), lambda b, h, i, *_: (0, 0)),
            pl.BlockSpec((1, 1), lambda b, h, i, *_: (0, 0)),
        ],
        out_specs=pl.BlockSpec((TQ, DIFF_V_DIM), lambda b, h, i, *_: (b * nq + i, h)),
        scratch_shapes=[pltpu.VMEM((DIFF_V_DIM, 2 * TQ), F32), pltpu.VMEM((TQ, ATTN_STRIP), F32),
                        pltpu.VMEM((1, 2 * TQ), F32)],
    )
    return pl.pallas_call(
        functools.partial(_diff_attn_kernel, lam_init=lam_init),
        grid_spec=grid_spec,
        out_shape=jax.ShapeDtypeStruct((t, SEC), BF16),
        compiler_params=pltpu.CompilerParams(vmem_limit_bytes=VMEM_LIMIT),
        name="diff_attn",
    )(use_fixed, qn, kn, vt, lam_rows, g_subln, score_bound.reshape(1, 1))


def _row_span(ref, row, n):
    return ref.at[pl.ds(pl.multiple_of(row * ROW_CHUNKS, ROW_CHUNKS), n * ROW_CHUNKS), :]


def _chunk_rows(row0, n, c):
    return pl.ds(row0 * ROW_CHUNKS + c, n, stride=ROW_CHUNKS)


def _load_rows(ref, row0, n):
    return jnp.concatenate([ref[_chunk_rows(row0, n, c), :] for c in range(ROW_CHUNKS)], axis=1)


def _store_rows(ref, n, val):
    for c in range(ROW_CHUNKS):
        ref[_chunk_rows(0, n, c), :] = val[:, c * LANES:(c + 1) * LANES]


def _copy_run(src_hbm, buf, sem, src_row, dst_row, n, max_n):
    piece = 1 << (max_n.bit_length() - 1)
    while piece >= 1:
        take = (n & piece) != 0

        @pl.when(take)
        def _(piece=piece, src_row=src_row, dst_row=dst_row):
            pltpu.make_async_copy(_row_span(src_hbm, src_row, piece), _row_span(buf, dst_row, piece),
                                  sem).start()

        step = jnp.where(take, piece, 0)
        src_row = src_row + step
        dst_row = dst_row + step
        piece //= 2


def _wait_rows(n, src_hbm, buf, sem, base):
    pltpu.make_async_copy(_row_span(src_hbm, 0, n), _row_span(buf, base, n), sem).wait()


def _post_kernel(x_ref, yd_ref, cb_ref, u_ref, up_ref, mq_ref, sig_ref, km_ref, vm_ref,
                 cw_ref, wb_ref, wo_ref, gf_ref, wr_ref, br_ref, tri_ref, ltri_ref,
                 x1_out, rows_out, slot_out, gate_out, cbase_out, ntile_out, off_out, cnt_out, carry_sc,
                 *, seq):
    i = pl.program_id(0)
    tm = x_ref.shape[0]

    @pl.when(i == 0)
    def _():
        carry_sc[...] = jnp.zeros(carry_sc.shape, F32)

    u = u_ref[...]
    seq_start = (i * tm) % seq == 0
    up = jnp.where(seq_start, 0.0, up_ref[...])
    r = lax.broadcasted_iota(jnp.int32, u.shape, 0)
    u1 = jnp.where(r == 0, up[7:8, :], pltpu.roll(u, 1, 0))
    u2 = jnp.where(r == 0, up[6:7, :], jnp.where(r == 1, up[7:8, :], pltpu.roll(u, 2, 0)))
    y_conv = cb_ref[...].astype(F32) * (cw_ref[0:1, :] * u2 + cw_ref[1:2, :] * u1 + cw_ref[2:3, :] * u)

    def mem_scores(hd):
        sl = slice(hd * MEM_HEAD_DIM, (hd + 1) * MEM_HEAD_DIM)
        return lax.dot_general(mq_ref[:, sl], km_ref[:, sl], (((1,), (1,)), ((), ())),
                               preferred_element_type=F32)

    y_mem = []
    s_next = mem_scores(0)
    for hd in range(MEM_HEADS):
        sl = slice(hd * MEM_HEAD_DIM, (hd + 1) * MEM_HEAD_DIM)
        s = s_next
        if hd + 1 < MEM_HEADS:
            s_next = mem_scores(hd + 1)
        if hd == 0:
            merged = (sig_ref[:, 0:D_MODEL].astype(F32)
                      * jnp.dot(yd_ref[...], wb_ref[0], preferred_element_type=F32))
        elif hd == 1:
            merged += (sig_ref[:, D_MODEL:2 * D_MODEL].astype(F32)
                       * jnp.dot(y_conv.astype(BF16), wb_ref[1], preferred_element_type=F32))
        p = jnp.exp2(s - jnp.max(s, axis=-1, keepdims=True))
        o = jnp.dot(p.astype(BF16), vm_ref[:, sl], preferred_element_type=F32)
        y_mem.append(o / jnp.sum(p, axis=-1, keepdims=True))

    merged += (sig_ref[:, 2 * D_MODEL:3 * D_MODEL].astype(F32)
               * jnp.dot(jnp.concatenate(y_mem, axis=1).astype(BF16), wb_ref[2],
                         preferred_element_type=F32))
    x1 = x_ref[...] + jnp.dot(merged.astype(BF16), wo_ref[...], preferred_element_type=F32)
    x1_out[...] = x1

    h2 = _rms(x1) * gf_ref[...]

    nt = (((1,), (1,)), ((), ()))
    w = wr_ref[...]
    w_hi = w.astype(BF16)
    w_lo = (w - w_hi.astype(F32)).astype(BF16)
    h_hi = h2.astype(BF16)
    h_lo = (h2 - h_hi.astype(F32)).astype(BF16)
    part = lax.dot_general(jnp.concatenate([w_hi, w_lo], axis=0), h_hi, nt,
                           preferred_element_type=F32)
    logits = (part[:N_EXPERTS] + part[N_EXPERTS:]
              + lax.dot_general(w_hi, h_lo, nt, preferred_element_type=F32) + br_ref[...])
    eio = lax.broadcasted_iota(jnp.int32, logits.shape, 0)
    work = logits
    vals, hots = [], []
    for k in range(TOP_K):
        mk = jnp.max(work, axis=0, keepdims=True)
        ik = jnp.min(jnp.where(work == mk, eio, N_EXPERTS), axis=0, keepdims=True)
        hot = eio == ik
        work = jnp.where(hot, -jnp.inf, work)
        vals.append(mk)
        hots.append(hot)
    ex = [jnp.exp(v - vals[0]) for v in vals]
    den = ex[0] + ex[1] + ex[2] + ex[3]
    gates = [e / den for e in ex]

    assign = jnp.zeros(logits.shape, F32)
    for hot in hots:
        assign = jnp.where(hot, 1.0, assign)
    earlier = jnp.dot(assign.astype(BF16), tri_ref[...], preferred_element_type=F32)
    n_col = jnp.sum(assign, axis=1, keepdims=True)
    off_col = jnp.dot(ltri_ref[...], jnp.broadcast_to(n_col, (N_EXPERTS, LANES)),
                      precision=lax.Precision.HIGHEST, preferred_element_type=F32)[:, 0:1]
    slots = [jnp.sum(jnp.where(hot, earlier + off_col, 0.0), axis=0, keepdims=True).astype(jnp.int32)
             for hot in hots]
    for k in range(TOP_K):
        slot_out[k:k + 1, :] = slots[k]
        gate_out[k:k + 1, :] = gates[k]

    n_slots = TOP_K * tm
    jdx = lax.broadcasted_iota(jnp.int32, (n_slots, tm), 0)
    pick = jnp.where(jdx == slots[0], 1.0, jnp.where(jdx == slots[1], 1.0, jnp.where(
        jdx == slots[2], 1.0, jnp.where(jdx == slots[3], 1.0, 0.0)))).astype(BF16)
    rows = jnp.dot(pick, h_hi, preferred_element_type=F32)
    _store_rows(rows_out, n_slots, rows)

    cbase_out[...] = jnp.broadcast_to(carry_sc[...], cbase_out.shape).astype(jnp.int32)
    ntile_out[...] = jnp.broadcast_to(n_col, ntile_out.shape).astype(jnp.int32)
    off_out[...] = jnp.broadcast_to(off_col, off_out.shape).astype(jnp.int32)
    carry_sc[...] = carry_sc[...] + n_col
    cnt_out[...] = jnp.broadcast_to(carry_sc[...], cnt_out.shape).astype(jnp.int32)


def _post(x2d, y_diff, cb, u, mqn, sig, km, vm, conv_w, w_branch, w_out, g_ffn, w_rt, b_r, tri, ltri,
          seq):
    t = x2d.shape[0]
    tm = TM_POST
    n_tiles = t // tm
    row = lambda i: (i, 0)
    const = lambda i: (0, 0)
    table = jax.ShapeDtypeStruct((n_tiles * N_EXPERTS, LANES), jnp.int32)
    outs = [
        jax.ShapeDtypeStruct((t, D_MODEL), F32),
        jax.ShapeDtypeStruct((TOP_K * t * ROW_CHUNKS, LANES), F32),
        jax.ShapeDtypeStruct((TOP_K, t), jnp.int32),
        jax.ShapeDtypeStruct((TOP_K, t), F32),
        table, table, table,
        jax.ShapeDtypeStruct((N_EXPERTS, LANES), jnp.int32),
    ]
    return pl.pallas_call(
        functools.partial(_post_kernel, seq=seq),
        grid=(t // tm,),
        in_specs=[
            pl.BlockSpec((tm, D_MODEL), row),
            pl.BlockSpec((tm, SEC), row),
            pl.BlockSpec((tm, SEC), row),
            pl.BlockSpec((tm, SEC), row),
            pl.BlockSpec((8, SEC), lambda i: (jnp.maximum(i * (tm // 8) - 1, 0), 0)),
            pl.BlockSpec((tm, SEC), row),
            pl.BlockSpec((tm, 3 * D_MODEL), row),
            pl.BlockSpec((N_MEM, SEC), lambda i: ((i * tm) // seq, 0)),
            pl.BlockSpec((N_MEM, SEC), lambda i: ((i * tm) // seq, 0)),
            pl.BlockSpec((3, SEC), const),
            pl.BlockSpec((3, SEC, D_MODEL), lambda i: (0, 0, 0)),
            pl.BlockSpec((D_MODEL, D_MODEL), const),
            pl.BlockSpec((1, D_MODEL), const),
            pl.BlockSpec((N_EXPERTS, D_MODEL), const),
            pl.BlockSpec((N_EXPERTS, 1), const),
            pl.BlockSpec((tm, tm), const),
            pl.BlockSpec((N_EXPERTS, N_EXPERTS), const),
        ],
        out_specs=[
            pl.BlockSpec((tm, D_MODEL), row),
            pl.BlockSpec((TOP_K * tm * ROW_CHUNKS, LANES), row),
            pl.BlockSpec((TOP_K, tm), lambda i: (0, i)),
            pl.BlockSpec((TOP_K, tm), lambda i: (0, i)),
            pl.BlockSpec((N_EXPERTS, LANES), row),
            pl.BlockSpec((N_EXPERTS, LANES), row),
            pl.BlockSpec((N_EXPERTS, LANES), row),
            pl.BlockSpec((N_EXPERTS, LANES), const),
        ],
        out_shape=outs,
        scratch_shapes=[pltpu.VMEM((N_EXPERTS, 1), F32)],
        compiler_params=pltpu.CompilerParams(dimension_semantics=("arbitrary",),
                                             vmem_limit_bytes=VMEM_LIMIT),
        name="post",
    )(x2d, y_diff, cb, u, u, mqn, sig, km, vm, conv_w, w_branch, w_out, g_ffn, w_rt, b_r, tri, ltri)


def _experts_kernel(be_ref, t0_ref, na_ref, ps_ref, cnt_ref, cb_ref, nt_ref, of_ref,
                    rows_hbm, zeros_hbm, wgu_f32, bgu_ref, wd_f32, bd_ref,
                    ys_out, xbuf, sem, wgu_ref, wd_ref, *, n_tiles, slots_per_tile):
    b = pl.program_id(0)
    n_active = na_ref[0]
    slot = b % 2

    @pl.when(jnp.logical_and(b < n_active,
                             jnp.logical_or(b == 0, be_ref[b] != be_ref[jnp.maximum(b - 1, 0)])))
    def _():
        wgu_ref[...] = wgu_f32[0].astype(BF16)
        wd_ref[...] = wd_f32[0].astype(BF16)

    def issue(blk, buf_slot):
        e = be_ref[blk]
        r0 = blk * MOE_BLOCK - ps_ref[e]
        r1 = r0 + MOE_BLOCK
        base = buf_slot * MOE_BLOCK
        sm = sem.at[buf_slot]

        def cond(i):
            return jnp.logical_and(i < n_tiles, cb_ref[e * n_tiles + jnp.minimum(i, n_tiles - 1)] < r1)

        def body(i):
            c = cb_ref[e * n_tiles + i]
            lo = jnp.maximum(c, r0)
            hi = jnp.minimum(c + nt_ref[e * n_tiles + i], r1)
            src = i * slots_per_tile + of_ref[e * n_tiles + i] + (lo - c)
            _copy_run(rows_hbm, xbuf, sm, src, base + (lo - r0), jnp.maximum(hi - lo, 0),
                      min(TM_POST, MOE_BLOCK))
            return i + 1

        lax.while_loop(cond, body, t0_ref[blk])
        valid = jnp.clip(cnt_ref[e] - r0, 0, MOE_BLOCK)
        _copy_run(zeros_hbm, xbuf, sm, 0, base + valid, MOE_BLOCK - valid, MOE_BLOCK)

    @pl.when(b == 0)
    def _():
        issue(b, 0)

    @pl.when(b + 1 < n_active)
    def _():
        issue(b + 1, 1 - slot)

    def compute(n_rows):
        base = pl.multiple_of(slot * MOE_BLOCK, MOE_BLOCK)
        _wait_rows(MOE_BLOCK, rows_hbm, xbuf, sem.at[slot], base)
        x = _load_rows(xbuf, base, n_rows).astype(BF16)
        d_ff = wd_ref.shape[0]

        def gate_up(j):
            gs = slice(j * FF_CHUNK, (j + 1) * FF_CHUNK)
            us = slice(d_ff + j * FF_CHUNK, d_ff + (j + 1) * FF_CHUNK)
            return (jnp.dot(x, wgu_ref[:, gs], preferred_element_type=F32) + bgu_ref[0, :, gs],
                    jnp.dot(x, wgu_ref[:, us], preferred_element_type=F32) + bgu_ref[0, :, us])

        acts = []
        nxt = gate_up(0)
        for j in range(d_ff // FF_CHUNK):
            g, u = nxt
            if (j + 1) * FF_CHUNK < d_ff:
                nxt = gate_up(j + 1)
            g = jnp.minimum(g, SWIGLU_LIMIT)
            u = jnp.clip(u, -SWIGLU_LIMIT, SWIGLU_LIMIT)
            acts.append(((u + 1.0) * (g * (1.0 / (1.0 + jnp.exp(-SWIGLU_ALPHA * g))))).astype(BF16))
        act = jnp.concatenate(acts, axis=1)
        for n in range(D_MODEL // FF_CHUNK):
            cols = slice(n * FF_CHUNK, (n + 1) * FF_CHUNK)
            yn = jnp.dot(act, wd_ref[:, cols], preferred_element_type=F32) + bd_ref[0, :, cols]
            for c in range(FF_CHUNK // LANES):
                ys_out[_chunk_rows(0, n_rows, n * (FF_CHUNK // LANES) + c), :] = (
                    yn[:, c * LANES:(c + 1) * LANES])
        if n_rows < MOE_BLOCK:
            ys_out[n_rows * ROW_CHUNKS:, :] = jnp.zeros(((MOE_BLOCK - n_rows) * ROW_CHUNKS, LANES), F32)

    @pl.when(b < n_active)
    def _():
        compute(MOE_BLOCK)

    @pl.when(b >= n_active)
    def _():
        ys_out[...] = jnp.zeros(ys_out.shape, F32)


def _experts(block_expert, first_tile, n_active, pstart, counts, cbase_e, ntile_e, off_e,
             rows, zero_rows, wgu, bgu, wd, bd, n_tiles):
    n_blocks = block_expert.shape[0]
    ff2 = wgu.shape[2]
    by_expert = lambda b, be, *_: (be[b], 0, 0)
    grid_spec = pltpu.PrefetchScalarGridSpec(
        num_scalar_prefetch=8,
        grid=(n_blocks,),
        in_specs=[
            pl.BlockSpec(memory_space=pl.ANY),
            pl.BlockSpec(memory_space=pl.ANY),
            pl.BlockSpec((1, D_MODEL, ff2), by_expert),
            pl.BlockSpec((1, 1, ff2), by_expert),
            pl.BlockSpec((1, ff2 // 2, D_MODEL), by_expert),
            pl.BlockSpec((1, 1, D_MODEL), by_expert),
        ],
        out_specs=pl.BlockSpec((MOE_BLOCK * ROW_CHUNKS, LANES), lambda b, *_: (b, 0)),
        scratch_shapes=[
            pltpu.VMEM((2 * MOE_BLOCK * ROW_CHUNKS, LANES), F32),
            pltpu.SemaphoreType.DMA((2,)),
            pltpu.VMEM((D_MODEL, ff2), BF16),
            pltpu.VMEM((ff2 // 2, D_MODEL), BF16),
        ],
    )
    return pl.pallas_call(
        functools.partial(_experts_kernel, n_tiles=n_tiles, slots_per_tile=TOP_K * TM_POST),
        grid_spec=grid_spec,
        out_shape=jax.ShapeDtypeStruct((n_blocks * MOE_BLOCK * ROW_CHUNKS, LANES), F32),
        compiler_params=pltpu.CompilerParams(dimension_semantics=("arbitrary",),
                                             vmem_limit_bytes=VMEM_LIMIT_EXPERTS),
        name="experts",
    )(block_expert, first_tile, n_active, pstart, counts, cbase_e, ntile_e, off_e,
      rows, zero_rows, wgu, bgu, wd, bd)


def _combine_kernel(ps_ref, cb_ref, nt_ref, of_ref, slot_ref, gate_ref, x1_ref, ys_hbm, o_ref, buf,
                    sem):
    i = pl.program_id(0)
    n = pl.num_programs(0)
    tm = x1_ref.shape[0]
    n_slots = TOP_K * tm
    cur = i % 2

    def issue(tile, buf_slot):
        def body(e, carry):
            k = tile * N_EXPERTS + e
            _copy_run(ys_hbm, buf, sem.at[buf_slot], ps_ref[e] + cb_ref[k],
                      buf_slot * n_slots + of_ref[k], nt_ref[k], tm)
            return carry
        lax.fori_loop(0, N_EXPERTS, body, 0)

    @pl.when(i == 0)
    def _():
        issue(i, 0)

    @pl.when(i + 1 < n)
    def _():
        issue(i + 1, 1 - cur)

    base = pl.multiple_of(cur * n_slots, n_slots)
    _wait_rows(n_slots, ys_hbm, buf, sem.at[cur], base)
    y = _load_rows(buf, base, n_slots).astype(BF16)
    jdx = lax.broadcasted_iota(jnp.int32, (n_slots, tm), 0)
    pick = jnp.where(jdx == slot_ref[0:1, :], gate_ref[0:1, :], jnp.where(
        jdx == slot_ref[1:2, :], gate_ref[1:2, :], jnp.where(
            jdx == slot_ref[2:3, :], gate_ref[2:3, :], jnp.where(
                jdx == slot_ref[3:4, :], gate_ref[3:4, :], 0.0)))).astype(BF16)
    o_ref[...] = x1_ref[...] + lax.dot_general(pick, y, (((0,), (0,)), ((), ())),
                                               preferred_element_type=F32)


def _combine(pstart, cbase_t, ntile_t, off_t, slots, gates, x1, ys_rows):
    t = x1.shape[0]
    tm = TM_POST
    n_slots = TOP_K * tm
    grid_spec = pltpu.PrefetchScalarGridSpec(
        num_scalar_prefetch=4,
        grid=(t // tm,),
        in_specs=[
            pl.BlockSpec((TOP_K, tm), lambda i, *_: (0, i)),
            pl.BlockSpec((TOP_K, tm), lambda i, *_: (0, i)),
            pl.BlockSpec((tm, D_MODEL), lambda i, *_: (i, 0)),
            pl.BlockSpec(memory_space=pl.ANY),
        ],
        out_specs=pl.BlockSpec((tm, D_MODEL), lambda i, *_: (i, 0)),
        scratch_shapes=[
            pltpu.VMEM((2 * n_slots * ROW_CHUNKS, LANES), F32),
            pltpu.SemaphoreType.DMA((2,)),
        ],
    )
    return pl.pallas_call(
        _combine_kernel,
        grid_spec=grid_spec,
        out_shape=jax.ShapeDtypeStruct((t, D_MODEL), F32),
        compiler_params=pltpu.CompilerParams(dimension_semantics=("arbitrary",),
                                             vmem_limit_bytes=VMEM_LIMIT),
        name="combine",
    )(pstart, cbase_t, ntile_t, off_t, slots, gates, x1, ys_rows)


def kernel(x, mem, g_mix, w_in, g_q_diff, g_k_diff, lambda_q1, lambda_k1, lambda_q2, lambda_k2,
           g_subln, conv_w, g_mem, w_mem_kv, g_q_mem, g_k_mem, w_branch, w_out, g_ffn, w_router,
           b_router, w_gate_up, b_gate_up, w_down, b_down):
    batch, seq, d = x.shape
    t = batch * seq
    depth = g_mix.shape[0]
    assert depth == 1 and d == D_MODEL and seq % TQ == 0 and t % TM_PROJ == 0

    pos = jnp.arange(seq, dtype=F32)
    inv_freq = 1.0 / (ROPE_THETA ** (jnp.arange(0, DIFF_HEAD_DIM, 2, dtype=F32) / DIFF_HEAD_DIM))
    ang = pos[:, None] * inv_freq[None, :]
    cos_t = jnp.tile(jnp.cos(ang), (1, 4))
    sin_t = jnp.tile(jnp.concatenate([-jnp.sin(ang), jnp.sin(ang)], axis=1), (1, 2))
    grp = jnp.arange(SEC) // DIFF_HEAD_DIM
    bd = jnp.where(grp[:, None] == grp[None, :], 1.0 / DIFF_HEAD_DIM, 0.0).astype(BF16)
    tok = jnp.arange(TM_POST)
    tri = (tok[:, None] < tok[None, :]).astype(BF16)

    x2d = x.reshape(t, d)
    l = 0
    lam_init = 0.8 - 0.6 * math.exp(-0.3 * l)
    km, vm = _mem_kv(mem.reshape(batch * N_MEM, d), g_mem[l][None, :], w_mem_kv[l].astype(BF16),
                     g_k_mem[l][None, :])
    qn, kn, vt, cb, u, mqn, sig = _in_proj(
        x2d, g_mix[l][None, :], w_in[l].astype(BF16), cos_t, sin_t, bd,
        jnp.tile(g_q_diff[l], SEC // DIFF_HEAD_DIM)[None, :],
        jnp.tile(g_k_diff[l], SEC // DIFF_HEAD_DIM)[None, :],
        g_q_mem[l][None, :], seq)
    lam_rows = jnp.stack([lambda_q1[l], lambda_k1[l], lambda_q2[l], lambda_k2[l]]).astype(F32)
    score_bound = (DIFF_HEAD_DIM * (DIFF_HEAD_DIM ** -0.5 * LOG2E)
                   * jnp.max(jnp.abs(g_q_diff[l])) * jnp.max(jnp.abs(g_k_diff[l]))).astype(F32)
    y_diff = _diff_attn(qn, kn, vt, lam_rows, g_subln[l][None, :], score_bound, batch, seq, lam_init)
    ex = jnp.arange(N_EXPERTS)
    ltri = (ex[None, :] < ex[:, None]).astype(F32)
    x1, rows, slots, gates, cbase, ntile, off, counts = _post(
        x2d, y_diff, cb, u, mqn, sig, km, vm, conv_w[l], w_branch[l].astype(BF16),
        w_out[l].astype(BF16), g_ffn[l][None, :], w_router[l].T, b_router[l][:, None], tri, ltri, seq)

    n_tiles = t // TM_POST
    n_assign = t * TOP_K
    n_rows = -(-(n_assign + N_EXPERTS * (MOE_BLOCK - 1)) // MOE_BLOCK) * MOE_BLOCK
    n_blocks = n_rows // MOE_BLOCK
    cnt = counts[:, 0]
    padded = (cnt + MOE_BLOCK - 1) // MOE_BLOCK * MOE_BLOCK
    pend = jnp.cumsum(padded)
    pstart = pend - padded
    block_row0 = jnp.arange(n_blocks, dtype=jnp.int32) * MOE_BLOCK
    block_expert = jnp.minimum(
        jnp.sum((pend[None, :] <= block_row0[:, None]).astype(jnp.int32), axis=1), N_EXPERTS - 1)
    n_active = (pend[-1:] // MOE_BLOCK).astype(jnp.int32)
    cbase_t = cbase[:, 0].reshape(n_tiles, N_EXPERTS)
    ntile_t = ntile[:, 0].reshape(n_tiles, N_EXPERTS)
    off_t = off[:, 0].reshape(n_tiles, N_EXPERTS)
    cend_b = (cbase_t + ntile_t).T[block_expert]
    r0 = block_row0 - pstart[block_expert]
    first_tile = jnp.sum((cend_b <= r0[:, None]).astype(jnp.int32), axis=1)
    first_tile = jnp.minimum(first_tile, n_tiles - 1).astype(jnp.int32)

    zero_rows = jnp.zeros((MOE_BLOCK * ROW_CHUNKS, LANES), F32)
    ys_rows = _experts(block_expert, first_tile, n_active, pstart, cnt,
                       cbase_t.T.reshape(-1), ntile_t.T.reshape(-1), off_t.T.reshape(-1),
                       rows, zero_rows, w_gate_up[l], b_gate_up[l][:, None, :],
                       w_down[l], b_down[l][:, None, :], n_tiles)
    out = _combine(pstart, cbase_t.reshape(-1), ntile_t.reshape(-1), off_t.reshape(-1), slots, gates,
                   x1, ys_rows)
    return out.reshape(batch, seq, d)
```

```python
import functools
import math

import jax
import jax.numpy as jnp
from jax import lax
from jax.experimental import pallas as pl
from jax.experimental.pallas import tpu as pltpu

F32 = jnp.float32
BF16 = jnp.bfloat16

D_MODEL = 1024
CHUNK = 64
EPS = 1e-6
ROPE_THETA = 10000.0
DIFF_HEADS = 4
DIFF_HEAD_DIM = 64
DIFF_V_DIM = 2 * DIFF_HEAD_DIM
N_MEM = 256
MEM_HEADS = 4
MEM_HEAD_DIM = 128
SEC = 512
N_SEC = 13
N_EXPERTS = 32
TOP_K = 4
SWIGLU_LIMIT = 7.0
SWIGLU_ALPHA = 1.702
MOE_BLOCK = 512
LANES = 128
ROW_CHUNKS = D_MODEL // LANES
LOG2E = 1.4426950408889634
NEG_BIG = -1e30

TM_PROJ = 512
TQ = 1024
ATTN_STRIP = 512
MAX_FIXED_SHIFT = 40.0
TM_POST = 256
FF_CHUNK = 256
VMEM_LIMIT = 48 * 1024 * 1024
VMEM_LIMIT_EXPERTS = 56 * 1024 * 1024


def _rms(x, eps=EPS):
    return x * lax.rsqrt(jnp.mean(x * x, axis=-1, keepdims=True) + eps)


def _mem_kv_kernel(mem_ref, g_ref, w_ref, gk_ref, k_out, v_out):
    h = (_rms(mem_ref[...]) * g_ref[...]).astype(BF16)
    kv = jnp.dot(h, w_ref[...], preferred_element_type=F32)
    for hd in range(MEM_HEADS):
        sl = slice(hd * MEM_HEAD_DIM, (hd + 1) * MEM_HEAD_DIM)
        k_out[sl, :] = (_rms(kv[:, sl]) * gk_ref[...]).T.astype(BF16)
    v_out[...] = kv[:, SEC:].astype(BF16)


def _mem_kv(mem2d, g_mem, w_kv, g_k):
    rows = mem2d.shape[0]
    return pl.pallas_call(
        _mem_kv_kernel,
        grid=(rows // N_MEM,),
        in_specs=[
            pl.BlockSpec((N_MEM, D_MODEL), lambda i: (i, 0)),
            pl.BlockSpec((1, D_MODEL), lambda i: (0, 0)),
            pl.BlockSpec((D_MODEL, 2 * SEC), lambda i: (0, 0)),
            pl.BlockSpec((1, MEM_HEAD_DIM), lambda i: (0, 0)),
        ],
        out_specs=[pl.BlockSpec((SEC, N_MEM), lambda i: (i, 0)),
                   pl.BlockSpec((N_MEM, SEC), lambda i: (i, 0))],
        out_shape=[jax.ShapeDtypeStruct((rows // N_MEM * SEC, N_MEM), BF16),
                   jax.ShapeDtypeStruct((rows, SEC), BF16)],
        compiler_params=pltpu.CompilerParams(vmem_limit_bytes=VMEM_LIMIT),
        name="mem_kv",
    )(mem2d, g_mem, w_kv, g_k)


def _in_proj_kernel(x_ref, g_ref, w_ref, cos_ref, sin_ref, bd_ref, gq_ref, gk_ref, gqm_ref,
                    q_out, k_out, v_out, cb_out, u_out, mq_out, sig_out):
    tm = x_ref.shape[0]
    h = (_rms(x_ref[...]) * g_ref[...]).astype(BF16)

    def proj(sec):
        return jnp.dot(h, w_ref[:, sec * SEC:(sec + 1) * SEC], preferred_element_type=F32)

    cos = jnp.concatenate([cos_ref[...]] * (SEC // LANES), axis=1)
    sin = jnp.concatenate([sin_ref[...]] * (SEC // LANES), axis=1)
    lane = lax.broadcasted_iota(jnp.int32, (tm, SEC), 1)
    first_half = (lane & (DIFF_HEAD_DIM // 2)) == 0

    def norm_rope(a, g, scale):
        ms = jnp.dot((a * a).astype(BF16), bd_ref[...], preferred_element_type=F32)
        y = a * lax.rsqrt(ms + EPS) * g
        partner = jnp.where(first_half,
                            pltpu.roll(y, SEC - DIFF_HEAD_DIM // 2, 1),
                            pltpu.roll(y, DIFF_HEAD_DIM // 2, 1))
        return (y * cos + partner * sin) * scale

    q_out[...] = norm_rope(proj(0), gq_ref[...], DIFF_HEAD_DIM ** -0.5 * LOG2E).astype(BF16)
    k_out[...] = norm_rope(proj(1), gk_ref[...], 1.0).astype(BF16)
    v = proj(2)
    for hd in range(DIFF_HEADS):
        v_out[0, hd, 0] = v[:, hd * DIFF_V_DIM:(hd + 1) * DIFF_V_DIM].T.astype(BF16)
    cb_out[...] = proj(3).astype(BF16)
    u_out[...] = proj(4) * proj(5)
    mq = proj(6)
    for hd in range(MEM_HEADS):
        sl = slice(hd * MEM_HEAD_DIM, (hd + 1) * MEM_HEAD_DIM)
        mq_out[:, sl] = (_rms(mq[:, sl]) * gqm_ref[...]
                         * (MEM_HEAD_DIM ** -0.5 * LOG2E)).astype(BF16)
    for s in range(7, N_SEC):
        a = proj(s)
        sig_out[:, (s - 7) * SEC:(s - 6) * SEC] = (1.0 / (1.0 + jnp.exp(-a))).astype(BF16)


def _in_proj(x2d, g_mix, w_in, cos_t, sin_t, bd, gq, gk, gqm, seq):
    t = x2d.shape[0]
    tm = TM_PROJ
    tiles_per_seq = seq // tm
    tiles_per_key = TQ // tm
    row = lambda i: (i, 0)
    const = lambda i: (0, 0)
    outs = [
        jax.ShapeDtypeStruct((t, SEC), BF16),
        jax.ShapeDtypeStruct((t, SEC), BF16),
        jax.ShapeDtypeStruct((t // seq, DIFF_HEADS, seq // TQ, DIFF_V_DIM, TQ), BF16),
        jax.ShapeDtypeStruct((t, SEC), BF16),
        jax.ShapeDtypeStruct((t, SEC), F32),
        jax.ShapeDtypeStruct((t, SEC), BF16),
        jax.ShapeDtypeStruct((t, 3 * D_MODEL), BF16),
    ]
    return pl.pallas_call(
        _in_proj_kernel,
        grid=(t // tm,),
        in_specs=[
            pl.BlockSpec((tm, D_MODEL), row),
            pl.BlockSpec((1, D_MODEL), const),
            pl.BlockSpec((D_MODEL, N_SEC * SEC), const),
            pl.BlockSpec((tm, LANES), lambda i: (i % tiles_per_seq, 0)),
            pl.BlockSpec((tm, LANES), lambda i: (i % tiles_per_seq, 0)),
            pl.BlockSpec((SEC, SEC), const),
            pl.BlockSpec((1, SEC), const),
            pl.BlockSpec((1, SEC), const),
            pl.BlockSpec((1, MEM_HEAD_DIM), const),
        ],
        out_specs=[
            pl.BlockSpec((1, DIFF_HEADS, 1, DIFF_V_DIM, tm),
                         lambda i: (i // tiles_per_seq, 0, (i % tiles_per_seq) // tiles_per_key,
                                    0, i % tiles_per_key))
            if o.ndim == 5 else pl.BlockSpec((tm, o.shape[1]), row) for o in outs],
        out_shape=outs,
        compiler_params=pltpu.CompilerParams(vmem_limit_bytes=VMEM_LIMIT),
        name="in_proj",
    )(x2d, g_mix, w_in, cos_t, sin_t, bd, gq, gk, gqm)


def _diff_attn_kernel(fixed_ref, q_ref, k_ref, vt_ref, lam_ref, gs_ref, bound_ref, o_ref,
                      acc_sc, s0_sc, l_sc, *, lam_init):
    i = pl.program_id(2)
    tq = q_ref.shape[0]
    qt = q_ref[...].astype(F32).T
    dim = lax.broadcasted_iota(jnp.int32, qt.shape, 0)
    qq = jnp.concatenate([jnp.where(dim < DIFF_HEAD_DIM, qt, 0.0),
                          jnp.where(dim >= DIFF_HEAD_DIM, qt, 0.0)], axis=1).astype(BF16)
    acc_sc[...] = jnp.zeros(acc_sc.shape, F32)
    n_strips = 2 * tq // ATTN_STRIP

    def scores(j, c, nk):
        off = pl.multiple_of(j * tq, tq)
        return jnp.dot(k_ref[pl.ds(off, nk), :], qq[:, c * ATTN_STRIP:(c + 1) * ATTN_STRIP],
                       preferred_element_type=F32)

    def step(j, ms, ls, masked, fixed):
        q_offs = [(c * ATTN_STRIP) % tq for c in range(n_strips)]
        nks = [min(tq, qo + ATTN_STRIP) if masked else tq for qo in q_offs]
        ms_new, ls_new = [], []
        s = s0_sc[0:nks[0], :]
        for c in range(n_strips):
            cols = slice(c * ATTN_STRIP, (c + 1) * ATTN_STRIP)
            if c + 1 < n_strips:
                s_next = scores(j, c + 1, nks[c + 1])
            elif not masked:
                s0_sc[...] = scores(j + 1, 0, tq)
            if masked:
                r = lax.broadcasted_iota(jnp.int32, s.shape, 0)
                col = lax.broadcasted_iota(jnp.int32, s.shape, 1)
                s = jnp.where((r // CHUNK) <= ((col + q_offs[c]) // CHUNK), s, NEG_BIG)
            if fixed:
                p = jnp.exp2(s - bound_ref[...])
                ls_new.append(ls[c] + jnp.sum(p, axis=0, keepdims=True))
                ms_new.append(ms[c])
                acc_sc[:, cols] = acc_sc[:, cols] + jnp.dot(
                    vt_ref[0, 0, j][:, :nks[c]], p.astype(BF16), preferred_element_type=F32)
            else:
                m_new = jnp.maximum(ms[c], jnp.max(s, axis=0, keepdims=True))
                p = jnp.exp2(s - m_new)
                alpha = jnp.exp2(ms[c] - m_new)
                ls_new.append(alpha * ls[c] + jnp.sum(p, axis=0, keepdims=True))
                ms_new.append(m_new)
                acc_sc[:, cols] = alpha * acc_sc[:, cols] + jnp.dot(
                    vt_ref[0, 0, j][:, :nks[c]], p.astype(BF16), preferred_element_type=F32)
            if c + 1 < n_strips:
                s = s_next
        return tuple(ms_new), tuple(ls_new)

    def run(fixed):
        s0_sc[...] = scores(0, 0, tq)
        m0 = tuple(jnp.full((1, ATTN_STRIP), NEG_BIG, F32) for _ in range(n_strips))
        l0 = tuple(jnp.zeros((1, ATTN_STRIP), F32) for _ in range(n_strips))
        m, l = lax.fori_loop(0, i, lambda j, c: step(j, c[0], c[1], False, fixed), (m0, l0))
        m, l = step(i, m, l, True, fixed)
        l_sc[...] = jnp.concatenate(l, axis=1)

    @pl.when(fixed_ref[0] == 1)
    def _():
        run(True)

    @pl.when(fixed_ref[0] != 1)
    def _():
        run(False)

    ot = acc_sc[...] / l_sc[...]
    lam = (jnp.exp(jnp.sum(lam_ref[0:1, :] * lam_ref[1:2, :], axis=-1, keepdims=True))
           - jnp.exp(jnp.sum(lam_ref[2:3, :] * lam_ref[3:4, :], axis=-1, keepdims=True))
           + lam_init)
    d = (ot[:, :tq] - lam * ot[:, tq:]).T
    o_ref[...] = (_rms(d) * gs_ref[...] * (1.0 - lam_init)).astype(BF16)


def _diff_attn(qn, kn, vt, lam_rows, g_subln, score_bound, batch, seq, lam_init):
    t = qn.shape[0]
    nq = seq // TQ
    use_fixed = (score_bound <= MAX_FIXED_SHIFT).astype(jnp.int32).reshape(1)
    grid_spec = pltpu.PrefetchScalarGridSpec(
        num_scalar_prefetch=1,
        grid=(batch, DIFF_HEADS, nq),
        in_specs=[
            pl.BlockSpec((TQ, DIFF_V_DIM), lambda b, h, i, *_: (b * nq + i, h)),
            pl.BlockSpec((seq, DIFF_V_DIM), lambda b, h, i, *_: (b, h)),
            pl.BlockSpec((1, 1, nq, DIFF_V_DIM, TQ), lambda b, h, i, *_: (b, h, 0, 0, 0)),
            pl.BlockSpec((4, DIFF_HEAD_DIM), lambda b, h, i, *_: (0, 0)),
            pl.BlockSpec((1, DIFF_V_DIM), lambda b, h, i, *_: (0, 0)),
            pl.BlockSpec((1, 1), lambda b, h, i, *_: (0, 0)),
        ],
        out_specs=pl.BlockSpec((TQ, DIFF_V_DIM), lambda b, h, i, *_: (b * nq + i, h)),
        scratch_shapes=[pltpu.VMEM((DIFF_V_DIM, 2 * TQ), F32), pltpu.VMEM((TQ, ATTN_STRIP), F32),
                        pltpu.VMEM((1, 2 * TQ), F32)],
    )
    return pl.pallas_call(
        functools.partial(_diff_attn_kernel, lam_init=lam_init),
        grid_spec=grid_spec,
        out_shape=jax.ShapeDtypeStruct((t, SEC), BF16),
        compiler_params=pltpu.CompilerParams(vmem_limit_bytes=VMEM_LIMIT),
        name="diff_attn",
    )(use_fixed, qn, kn, vt, lam_rows, g_subln, score_bound.reshape(1, 1))


def _row_span(ref, row, n):
    return ref.at[pl.ds(pl.multiple_of(row * ROW_CHUNKS, ROW_CHUNKS), n * ROW_CHUNKS), :]


def _chunk_rows(row0, n, c):
    return pl.ds(row0 * ROW_CHUNKS + c, n, stride=ROW_CHUNKS)


def _load_rows(ref, row0, n):
    return jnp.concatenate([ref[_chunk_rows(row0, n, c), :] for c in range(ROW_CHUNKS)], axis=1)


def _store_rows(ref, n, val):
    for c in range(ROW_CHUNKS):
        ref[_chunk_rows(0, n, c), :] = val[:, c * LANES:(c + 1) * LANES]


def _copy_run(src_hbm, buf, sem, src_row, dst_row, n, max_n):
    piece = 1 << (max_n.bit_length() - 1)
    while piece >= 1:
        take = (n & piece) != 0

        @pl.when(take)
        def _(piece=piece, src_row=src_row, dst_row=dst_row):
            pltpu.make_async_copy(_row_span(src_hbm, src_row, piece), _row_span(buf, dst_row, piece),
                                  sem).start()

        step = jnp.where(take, piece, 0)
        src_row = src_row + step
        dst_row = dst_row + step
        piece //= 2


def _wait_rows(n, src_hbm, buf, sem, base):
    pltpu.make_async_copy(_row_span(src_hbm, 0, n), _row_span(buf, base, n), sem).wait()


def _post_kernel(x_ref, yd_ref, cb_ref, u_ref, up_ref, mq_ref, sig_ref, km_ref, vm_ref,
                 cw_ref, wb_ref, wo_ref, gf_ref, wr_ref, br_ref, tri_ref,
                 x1_out, rows_out, slot_out, gate_out, cbase_out, ntile_out, off_out, cnt_out, carry_sc,
                 *, seq):
    i = pl.program_id(0)
    tm = x_ref.shape[0]

    @pl.when(i == 0)
    def _():
        carry_sc[...] = jnp.zeros(carry_sc.shape, F32)

    u = u_ref[...]
    seq_start = (i * tm) % seq == 0
    up = jnp.where(seq_start, 0.0, up_ref[...])
    r = lax.broadcasted_iota(jnp.int32, u.shape, 0)
    u1 = jnp.where(r == 0, up[7:8, :], pltpu.roll(u, 1, 0))
    u2 = jnp.where(r == 0, up[6:7, :], jnp.where(r == 1, up[7:8, :], pltpu.roll(u, 2, 0)))
    y_conv = cb_ref[...].astype(F32) * (cw_ref[0:1, :] * u2 + cw_ref[1:2, :] * u1 + cw_ref[2:3, :] * u)

    def mem_scores(hd):
        sl = slice(hd * MEM_HEAD_DIM, (hd + 1) * MEM_HEAD_DIM)
        return jnp.dot(mq_ref[:, sl], km_ref[sl, :], preferred_element_type=F32)

    y_mem = []
    s_next = mem_scores(0)
    for hd in range(MEM_HEADS):
        sl = slice(hd * MEM_HEAD_DIM, (hd + 1) * MEM_HEAD_DIM)
        s = s_next
        if hd + 1 < MEM_HEADS:
            s_next = mem_scores(hd + 1)
        if hd == 0:
            merged = (sig_ref[:, 0:D_MODEL].astype(F32)
                      * jnp.dot(yd_ref[...], wb_ref[0], preferred_element_type=F32))
        elif hd == 1:
            merged += (sig_ref[:, D_MODEL:2 * D_MODEL].astype(F32)
                       * jnp.dot(y_conv.astype(BF16), wb_ref[1], preferred_element_type=F32))
        p = jnp.exp2(s - jnp.max(s, axis=-1, keepdims=True))
        o = jnp.dot(p.astype(BF16), vm_ref[:, sl], preferred_element_type=F32)
        y_mem.append(o / jnp.sum(p, axis=-1, keepdims=True))

    merged += (sig_ref[:, 2 * D_MODEL:3 * D_MODEL].astype(F32)
               * jnp.dot(jnp.concatenate(y_mem, axis=1).astype(BF16), wb_ref[2],
                         preferred_element_type=F32))
    x1 = x_ref[...] + jnp.dot(merged.astype(BF16), wo_ref[...], preferred_element_type=F32)
    x1_out[...] = x1

    h2 = _rms(x1) * gf_ref[...]

    w = wr_ref[...]
    w_hi = w.astype(BF16)
    w_lo = (w - w_hi.astype(F32)).astype(BF16)
    h_hi = h2.astype(BF16)
    h_lo = (h2 - h_hi.astype(F32)).astype(BF16)
    part = jnp.dot(h_hi, jnp.concatenate([w_hi, w_lo], axis=1), preferred_element_type=F32)
    by_token = (part[:, :N_EXPERTS] + part[:, N_EXPERTS:]
                + jnp.dot(h_lo, w_hi, preferred_element_type=F32))
    padded = jnp.concatenate([by_token, jnp.zeros((tm, LANES - N_EXPERTS), F32)], axis=1)
    logits = padded.T[:N_EXPERTS, :] + br_ref[...]
    eio = lax.broadcasted_iota(jnp.int32, logits.shape, 0)
    work = logits
    vals, hots = [], []
    for k in range(TOP_K):
        mk = jnp.max(work, axis=0, keepdims=True)
        ik = jnp.min(jnp.where(work == mk, eio, N_EXPERTS), axis=0, keepdims=True)
        hot = eio == ik
        work = jnp.where(hot, -jnp.inf, work)
        vals.append(mk)
        hots.append(hot)
    ex = [jnp.exp(v - vals[0]) for v in vals]
    den = ex[0] + ex[1] + ex[2] + ex[3]
    gates = [e / den for e in ex]

    assign = jnp.zeros(logits.shape, F32)
    for hot in hots:
        assign = jnp.where(hot, 1.0, assign)
    earlier = jnp.dot(assign.astype(BF16), tri_ref[...], preferred_element_type=F32)
    n_col = jnp.sum(assign, axis=1, keepdims=True)
    e_row = lax.broadcasted_iota(jnp.int32, (N_EXPERTS, LANES), 0)
    run = jnp.broadcast_to(n_col, (N_EXPERTS, LANES))
    shift = 1
    while shift < N_EXPERTS:
        run = run + jnp.where(e_row >= shift, pltpu.roll(run, shift, 0), 0.0)
        shift *= 2
    off_col = run[:, 0:1] - n_col
    slots = [jnp.sum(jnp.where(hot, earlier + off_col, 0.0), axis=0, keepdims=True).astype(jnp.int32)
             for hot in hots]
    for k in range(TOP_K):
        slot_out[k:k + 1, :] = slots[k]
        gate_out[k:k + 1, :] = gates[k]

    n_slots = TOP_K * tm
    jdx = lax.broadcasted_iota(jnp.int32, (n_slots, tm), 0)
    pick = jnp.where(jdx == slots[0], 1.0, jnp.where(jdx == slots[1], 1.0, jnp.where(
        jdx == slots[2], 1.0, jnp.where(jdx == slots[3], 1.0, 0.0)))).astype(BF16)
    rows = jnp.dot(pick, h_hi, preferred_element_type=F32)
    _store_rows(rows_out, n_slots, rows)

    cbase_out[...] = jnp.broadcast_to(carry_sc[...], cbase_out.shape).astype(jnp.int32)
    ntile_out[...] = jnp.broadcast_to(n_col, ntile_out.shape).astype(jnp.int32)
    off_out[...] = jnp.broadcast_to(off_col, off_out.shape).astype(jnp.int32)
    carry_sc[...] = carry_sc[...] + n_col
    cnt_out[...] = jnp.broadcast_to(carry_sc[...], cnt_out.shape).astype(jnp.int32)


def _post(x2d, y_diff, cb, u, mqn, sig, km, vm, conv_w, w_branch, w_out, g_ffn, w_rt, b_r, tri, seq):
    t = x2d.shape[0]
    tm = TM_POST
    n_tiles = t // tm
    row = lambda i: (i, 0)
    const = lambda i: (0, 0)
    table = jax.ShapeDtypeStruct((n_tiles * N_EXPERTS, LANES), jnp.int32)
    outs = [
        jax.ShapeDtypeStruct((t, D_MODEL), F32),
        jax.ShapeDtypeStruct((TOP_K * t * ROW_CHUNKS, LANES), F32),
        jax.ShapeDtypeStruct((TOP_K, t), jnp.int32),
        jax.ShapeDtypeStruct((TOP_K, t), F32),
        table, table, table,
        jax.ShapeDtypeStruct((N_EXPERTS, LANES), jnp.int32),
    ]
    return pl.pallas_call(
        functools.partial(_post_kernel, seq=seq),
        grid=(t // tm,),
        in_specs=[
            pl.BlockSpec((tm, D_MODEL), row),
            pl.BlockSpec((tm, SEC), row),
            pl.BlockSpec((tm, SEC), row),
            pl.BlockSpec((tm, SEC), row),
            pl.BlockSpec((8, SEC), lambda i: (jnp.maximum(i * (tm // 8) - 1, 0), 0)),
            pl.BlockSpec((tm, SEC), row),
            pl.BlockSpec((tm, 3 * D_MODEL), row),
            pl.BlockSpec((SEC, N_MEM), lambda i: ((i * tm) // seq, 0)),
            pl.BlockSpec((N_MEM, SEC), lambda i: ((i * tm) // seq, 0)),
            pl.BlockSpec((3, SEC), const),
            pl.BlockSpec((3, SEC, D_MODEL), lambda i: (0, 0, 0)),
            pl.BlockSpec((D_MODEL, D_MODEL), const),
            pl.BlockSpec((1, D_MODEL), const),
            pl.BlockSpec((D_MODEL, N_EXPERTS), const),
            pl.BlockSpec((N_EXPERTS, 1), const),
            pl.BlockSpec((tm, tm), const),
        ],
        out_specs=[
            pl.BlockSpec((tm, D_MODEL), row),
            pl.BlockSpec((TOP_K * tm * ROW_CHUNKS, LANES), row),
            pl.BlockSpec((TOP_K, tm), lambda i: (0, i)),
            pl.BlockSpec((TOP_K, tm), lambda i: (0, i)),
            pl.BlockSpec((N_EXPERTS, LANES), row),
            pl.BlockSpec((N_EXPERTS, LANES), row),
            pl.BlockSpec((N_EXPERTS, LANES), row),
            pl.BlockSpec((N_EXPERTS, LANES), const),
        ],
        out_shape=outs,
        scratch_shapes=[pltpu.VMEM((N_EXPERTS, 1), F32)],
        compiler_params=pltpu.CompilerParams(dimension_semantics=("arbitrary",),
                                             vmem_limit_bytes=VMEM_LIMIT),
        name="post",
    )(x2d, y_diff, cb, u, u, mqn, sig, km, vm, conv_w, w_branch, w_out, g_ffn, w_rt, b_r, tri)


def _experts_kernel(be_ref, t0_ref, na_ref, ps_ref, cnt_ref, cb_ref, nt_ref, of_ref,
                    rows_hbm, zeros_hbm, wgu_f32, bgu_ref, wd_f32, bd_ref,
                    ys_out, xbuf, sem, wgu_ref, wd_ref, *, n_tiles, slots_per_tile):
    b = pl.program_id(0)
    n_active = na_ref[0]
    slot = b % 2

    @pl.when(jnp.logical_and(b < n_active,
                             jnp.logical_or(b == 0, be_ref[b] != be_ref[jnp.maximum(b - 1, 0)])))
    def _():
        wgu_ref[...] = wgu_f32[0].astype(BF16)
        wd_ref[...] = wd_f32[0].astype(BF16)

    def issue(blk, buf_slot):
        e = be_ref[blk]
        r0 = blk * MOE_BLOCK - ps_ref[e]
        r1 = r0 + MOE_BLOCK
        base = buf_slot * MOE_BLOCK
        sm = sem.at[buf_slot]

        def cond(i):
            return jnp.logical_and(i < n_tiles, cb_ref[e * n_tiles + jnp.minimum(i, n_tiles - 1)] < r1)

        def body(i):
            c = cb_ref[e * n_tiles + i]
            lo = jnp.maximum(c, r0)
            hi = jnp.minimum(c + nt_ref[e * n_tiles + i], r1)
            src = i * slots_per_tile + of_ref[e * n_tiles + i] + (lo - c)
            _copy_run(rows_hbm, xbuf, sm, src, base + (lo - r0), jnp.maximum(hi - lo, 0),
                      min(TM_POST, MOE_BLOCK))
            return i + 1

        lax.while_loop(cond, body, t0_ref[blk])
        valid = jnp.clip(cnt_ref[e] - r0, 0, MOE_BLOCK)
        _copy_run(zeros_hbm, xbuf, sm, 0, base + valid, MOE_BLOCK - valid, MOE_BLOCK)

    @pl.when(b == 0)
    def _():
        issue(b, 0)

    @pl.when(b + 1 < n_active)
    def _():
        issue(b + 1, 1 - slot)

    def compute(n_rows):
        base = pl.multiple_of(slot * MOE_BLOCK, MOE_BLOCK)
        _wait_rows(MOE_BLOCK, rows_hbm, xbuf, sem.at[slot], base)
        x = _load_rows(xbuf, base, n_rows).astype(BF16)
        d_ff = wd_ref.shape[0]

        def gate_up(j):
            gs = slice(j * FF_CHUNK, (j + 1) * FF_CHUNK)
            us = slice(d_ff + j * FF_CHUNK, d_ff + (j + 1) * FF_CHUNK)
            return (jnp.dot(x, wgu_ref[:, gs], preferred_element_type=F32) + bgu_ref[0, :, gs],
                    jnp.dot(x, wgu_ref[:, us], preferred_element_type=F32) + bgu_ref[0, :, us])

        acts = []
        nxt = gate_up(0)
        for j in range(d_ff // FF_CHUNK):
            g, u = nxt
            if (j + 1) * FF_CHUNK < d_ff:
                nxt = gate_up(j + 1)
            g = jnp.minimum(g, SWIGLU_LIMIT)
            u = jnp.clip(u, -SWIGLU_LIMIT, SWIGLU_LIMIT)
            acts.append(((u + 1.0) * (g * (1.0 / (1.0 + jnp.exp(-SWIGLU_ALPHA * g))))).astype(BF16))
        act = jnp.concatenate(acts, axis=1)
        for n in range(D_MODEL // FF_CHUNK):
            cols = slice(n * FF_CHUNK, (n + 1) * FF_CHUNK)
            yn = jnp.dot(act, wd_ref[:, cols], preferred_element_type=F32) + bd_ref[0, :, cols]
            for c in range(FF_CHUNK // LANES):
                ys_out[_chunk_rows(0, n_rows, n * (FF_CHUNK // LANES) + c), :] = (
                    yn[:, c * LANES:(c + 1) * LANES])
        if n_rows < MOE_BLOCK:
            ys_out[n_rows * ROW_CHUNKS:, :] = jnp.zeros(((MOE_BLOCK - n_rows) * ROW_CHUNKS, LANES), F32)

    @pl.when(b < n_active)
    def _():
        compute(MOE_BLOCK)

    @pl.when(b >= n_active)
    def _():
        ys_out[...] = jnp.zeros(ys_out.shape, F32)


def _experts(block_expert, first_tile, n_active, pstart, counts, cbase_e, ntile_e, off_e,
             rows, zero_rows, wgu, bgu, wd, bd, n_tiles):
    n_blocks = block_expert.shape[0]
    ff2 = wgu.shape[2]
    by_expert = lambda b, be, *_: (be[b], 0, 0)
    grid_spec = pltpu.PrefetchScalarGridSpec(
        num_scalar_prefetch=8,
        grid=(n_blocks,),
        in_specs=[
            pl.BlockSpec(memory_space=pl.ANY),
            pl.BlockSpec(memory_space=pl.ANY),
            pl.BlockSpec((1, D_MODEL, ff2), by_expert),
            pl.BlockSpec((1, 1, ff2), by_expert),
            pl.BlockSpec((1, ff2 // 2, D_MODEL), by_expert),
            pl.BlockSpec((1, 1, D_MODEL), by_expert),
        ],
        out_specs=pl.BlockSpec((MOE_BLOCK * ROW_CHUNKS, LANES), lambda b, *_: (b, 0)),
        scratch_shapes=[
            pltpu.VMEM((2 * MOE_BLOCK * ROW_CHUNKS, LANES), F32),
            pltpu.SemaphoreType.DMA((2,)),
            pltpu.VMEM((D_MODEL, ff2), BF16),
            pltpu.VMEM((ff2 // 2, D_MODEL), BF16),
        ],
    )
    return pl.pallas_call(
        functools.partial(_experts_kernel, n_tiles=n_tiles, slots_per_tile=TOP_K * TM_POST),
        grid_spec=grid_spec,
        out_shape=jax.ShapeDtypeStruct((n_blocks * MOE_BLOCK * ROW_CHUNKS, LANES), F32),
        compiler_params=pltpu.CompilerParams(dimension_semantics=("arbitrary",),
                                             vmem_limit_bytes=VMEM_LIMIT_EXPERTS),
        name="experts",
    )(block_expert, first_tile, n_active, pstart, counts, cbase_e, ntile_e, off_e,
      rows, zero_rows, wgu, bgu, wd, bd)


def _combine_kernel(ps_ref, cb_ref, nt_ref, of_ref, slot_ref, gate_ref, x1_ref, ys_hbm, o_ref, buf,
                    sem):
    i = pl.program_id(0)
    n = pl.num_programs(0)
    tm = x1_ref.shape[0]
    n_slots = TOP_K * tm
    cur = i % 2

    def issue(tile, buf_slot):
        def body(e, carry):
            k = tile * N_EXPERTS + e
            _copy_run(ys_hbm, buf, sem.at[buf_slot], ps_ref[e] + cb_ref[k],
                      buf_slot * n_slots + of_ref[k], nt_ref[k], tm)
            return carry
        lax.fori_loop(0, N_EXPERTS, body, 0)

    @pl.when(i == 0)
    def _():
        issue(i, 0)

    @pl.when(i + 1 < n)
    def _():
        issue(i + 1, 1 - cur)

    base = pl.multiple_of(cur * n_slots, n_slots)
    _wait_rows(n_slots, ys_hbm, buf, sem.at[cur], base)
    y = _load_rows(buf, base, n_slots).astype(BF16)
    jdx = lax.broadcasted_iota(jnp.int32, (n_slots, tm), 0)
    pick = jnp.where(jdx == slot_ref[0:1, :], gate_ref[0:1, :], jnp.where(
        jdx == slot_ref[1:2, :], gate_ref[1:2, :], jnp.where(
            jdx == slot_ref[2:3, :], gate_ref[2:3, :], jnp.where(
                jdx == slot_ref[3:4, :], gate_ref[3:4, :], 0.0)))).astype(BF16)
    o_ref[...] = x1_ref[...] + lax.dot_general(pick, y, (((0,), (0,)), ((), ())),
                                               preferred_element_type=F32)


def _combine(pstart, cbase_t, ntile_t, off_t, slots, gates, x1, ys_rows):
    t = x1.shape[0]
    tm = TM_POST
    n_slots = TOP_K * tm
    grid_spec = pltpu.PrefetchScalarGridSpec(
        num_scalar_prefetch=4,
        grid=(t // tm,),
        in_specs=[
            pl.BlockSpec((TOP_K, tm), lambda i, *_: (0, i)),
            pl.BlockSpec((TOP_K, tm), lambda i, *_: (0, i)),
            pl.BlockSpec((tm, D_MODEL), lambda i, *_: (i, 0)),
            pl.BlockSpec(memory_space=pl.ANY),
        ],
        out_specs=pl.BlockSpec((tm, D_MODEL), lambda i, *_: (i, 0)),
        scratch_shapes=[
            pltpu.VMEM((2 * n_slots * ROW_CHUNKS, LANES), F32),
            pltpu.SemaphoreType.DMA((2,)),
        ],
    )
    return pl.pallas_call(
        _combine_kernel,
        grid_spec=grid_spec,
        out_shape=jax.ShapeDtypeStruct((t, D_MODEL), F32),
        compiler_params=pltpu.CompilerParams(dimension_semantics=("arbitrary",),
                                             vmem_limit_bytes=VMEM_LIMIT),
        name="combine",
    )(pstart, cbase_t, ntile_t, off_t, slots, gates, x1, ys_rows)


def kernel(x, mem, g_mix, w_in, g_q_diff, g_k_diff, lambda_q1, lambda_k1, lambda_q2, lambda_k2,
           g_subln, conv_w, g_mem, w_mem_kv, g_q_mem, g_k_mem, w_branch, w_out, g_ffn, w_router,
           b_router, w_gate_up, b_gate_up, w_down, b_down):
    batch, seq, d = x.shape
    t = batch * seq
    depth = g_mix.shape[0]
    assert depth == 1 and d == D_MODEL and seq % TQ == 0 and t % TM_PROJ == 0

    pos = jnp.arange(seq, dtype=F32)
    inv_freq = 1.0 / (ROPE_THETA ** (jnp.arange(0, DIFF_HEAD_DIM, 2, dtype=F32) / DIFF_HEAD_DIM))
    ang = pos[:, None] * inv_freq[None, :]
    cos_t = jnp.tile(jnp.cos(ang), (1, 4))
    sin_t = jnp.tile(jnp.concatenate([-jnp.sin(ang), jnp.sin(ang)], axis=1), (1, 2))
    grp = jnp.arange(SEC) // DIFF_HEAD_DIM
    bd = jnp.where(grp[:, None] == grp[None, :], 1.0 / DIFF_HEAD_DIM, 0.0).astype(BF16)
    tok = jnp.arange(TM_POST)
    tri = (tok[:, None] < tok[None, :]).astype(BF16)

    x2d = x.reshape(t, d)
    l = 0
    lam_init = 0.8 - 0.6 * math.exp(-0.3 * l)
    km, vm = _mem_kv(mem.reshape(batch * N_MEM, d), g_mem[l][None, :], w_mem_kv[l].astype(BF16),
                     g_k_mem[l][None, :])
    qn, kn, vt, cb, u, mqn, sig = _in_proj(
        x2d, g_mix[l][None, :], w_in[l].astype(BF16), cos_t, sin_t, bd,
        jnp.tile(g_q_diff[l], SEC // DIFF_HEAD_DIM)[None, :],
        jnp.tile(g_k_diff[l], SEC // DIFF_HEAD_DIM)[None, :],
        g_q_mem[l][None, :], seq)
    lam_rows = jnp.stack([lambda_q1[l], lambda_k1[l], lambda_q2[l], lambda_k2[l]]).astype(F32)
    score_bound = (DIFF_HEAD_DIM * (DIFF_HEAD_DIM ** -0.5 * LOG2E)
                   * jnp.max(jnp.abs(g_q_diff[l])) * jnp.max(jnp.abs(g_k_diff[l]))).astype(F32)
    y_diff = _diff_attn(qn, kn, vt, lam_rows, g_subln[l][None, :], score_bound, batch, seq, lam_init)
    x1, rows, slots, gates, cbase, ntile, off, counts = _post(
        x2d, y_diff, cb, u, mqn, sig, km, vm, conv_w[l], w_branch[l].astype(BF16),
        w_out[l].astype(BF16), g_ffn[l][None, :], w_router[l], b_router[l][:, None], tri, seq)

    n_tiles = t // TM_POST
    n_assign = t * TOP_K
    n_rows = -(-(n_assign + N_EXPERTS * (MOE_BLOCK - 1)) // MOE_BLOCK) * MOE_BLOCK
    n_blocks = n_rows // MOE_BLOCK
    cnt = counts[:, 0]
    padded = (cnt + MOE_BLOCK - 1) // MOE_BLOCK * MOE_BLOCK
    pend = jnp.cumsum(padded)
    pstart = pend - padded
    block_row0 = jnp.arange(n_blocks, dtype=jnp.int32) * MOE_BLOCK
    block_expert = jnp.minimum(
        jnp.sum((pend[None, :] <= block_row0[:, None]).astype(jnp.int32), axis=1), N_EXPERTS - 1)
    n_active = (pend[-1:] // MOE_BLOCK).astype(jnp.int32)
    cbase_t = cbase[:, 0].reshape(n_tiles, N_EXPERTS)
    ntile_t = ntile[:, 0].reshape(n_tiles, N_EXPERTS)
    off_t = off[:, 0].reshape(n_tiles, N_EXPERTS)
    cend_b = (cbase_t + ntile_t).T[block_expert]
    r0 = block_row0 - pstart[block_expert]
    first_tile = jnp.sum((cend_b <= r0[:, None]).astype(jnp.int32), axis=1)
    first_tile = jnp.minimum(first_tile, n_tiles - 1).astype(jnp.int32)

    zero_rows = jnp.zeros((MOE_BLOCK * ROW_CHUNKS, LANES), F32)
    ys_rows = _experts(block_expert, first_tile, n_active, pstart, cnt,
                       cbase_t.T.reshape(-1), ntile_t.T.reshape(-1), off_t.T.reshape(-1),
                       rows, zero_rows, w_gate_up[l], b_gate_up[l][:, None, :],
                       w_down[l], b_down[l][:, None, :], n_tiles)
    out = _combine(pstart, cbase_t.reshape(-1), ntile_t.reshape(-1), off_t.reshape(-1), slots, gates,
                   x1, ys_rows)
    return out.reshape(batch, seq, d)
```

```python
import functools
import math

import jax
import jax.numpy as jnp
from jax import lax
from jax.experimental import pallas as pl
from jax.experimental.pallas import tpu as pltpu

F32 = jnp.float32
BF16 = jnp.bfloat16

D_MODEL = 1024
CHUNK = 64
EPS = 1e-6
ROPE_THETA = 10000.0
DIFF_HEADS = 4
DIFF_HEAD_DIM = 64
DIFF_V_DIM = 2 * DIFF_HEAD_DIM
N_MEM = 256
MEM_HEADS = 4
MEM_HEAD_DIM = 128
SEC = 512
N_SEC = 13
N_EXPERTS = 32
TOP_K = 4
SWIGLU_LIMIT = 7.0
SWIGLU_ALPHA = 1.702
MOE_BLOCK = 512
LANES = 128
ROW_CHUNKS = D_MODEL // LANES
LOG2E = 1.4426950408889634
NEG_BIG = -1e30

TM_PROJ = 512
TQ = 1024
ATTN_STRIP = 512
MAX_FIXED_SHIFT = 40.0
TM_POST = 256
FF_CHUNK = 256
VMEM_LIMIT = 48 * 1024 * 1024
VMEM_LIMIT_EXPERTS = 56 * 1024 * 1024


def _rms(x, eps=EPS):
    return x * lax.rsqrt(jnp.mean(x * x, axis=-1, keepdims=True) + eps)


def _mem_kv_kernel(mem_ref, g_ref, w_ref, gk_ref, k_out, v_out):
    h = (_rms(mem_ref[...]) * g_ref[...]).astype(BF16)
    kv = jnp.dot(h, w_ref[...], preferred_element_type=F32)
    for hd in range(MEM_HEADS):
        sl = slice(hd * MEM_HEAD_DIM, (hd + 1) * MEM_HEAD_DIM)
        k_out[sl, :] = (_rms(kv[:, sl]) * gk_ref[...]).T.astype(BF16)
    v_out[...] = kv[:, SEC:].astype(BF16)


def _mem_kv(mem2d, g_mem, w_kv, g_k):
    rows = mem2d.shape[0]
    return pl.pallas_call(
        _mem_kv_kernel,
        grid=(rows // N_MEM,),
        in_specs=[
            pl.BlockSpec((N_MEM, D_MODEL), lambda i: (i, 0)),
            pl.BlockSpec((1, D_MODEL), lambda i: (0, 0)),
            pl.BlockSpec((D_MODEL, 2 * SEC), lambda i: (0, 0)),
            pl.BlockSpec((1, MEM_HEAD_DIM), lambda i: (0, 0)),
        ],
        out_specs=[pl.BlockSpec((SEC, N_MEM), lambda i: (i, 0)),
                   pl.BlockSpec((N_MEM, SEC), lambda i: (i, 0))],
        out_shape=[jax.ShapeDtypeStruct((rows // N_MEM * SEC, N_MEM), BF16),
                   jax.ShapeDtypeStruct((rows, SEC), BF16)],
        compiler_params=pltpu.CompilerParams(vmem_limit_bytes=VMEM_LIMIT),
        name="mem_kv",
    )(mem2d, g_mem, w_kv, g_k)


def _in_proj_kernel(x_ref, g_ref, w_ref, cos_ref, sin_ref, bd_ref, gq_ref, gk_ref, gqm_ref,
                    q_out, k_out, v_out, cb_out, u_out, mq_out, sig_out):
    tm = x_ref.shape[0]
    h = (_rms(x_ref[...]) * g_ref[...]).astype(BF16)

    def proj(sec):
        return jnp.dot(h, w_ref[:, sec * SEC:(sec + 1) * SEC], preferred_element_type=F32)

    cos = jnp.concatenate([cos_ref[...]] * (SEC // LANES), axis=1)
    sin = jnp.concatenate([sin_ref[...]] * (SEC // LANES), axis=1)
    lane = lax.broadcasted_iota(jnp.int32, (tm, SEC), 1)
    first_half = (lane & (DIFF_HEAD_DIM // 2)) == 0

    def norm_rope(a, g, scale):
        ms = jnp.dot((a * a).astype(BF16), bd_ref[...], preferred_element_type=F32)
        y = a * lax.rsqrt(ms + EPS) * g
        partner = jnp.where(first_half,
                            pltpu.roll(y, SEC - DIFF_HEAD_DIM // 2, 1),
                            pltpu.roll(y, DIFF_HEAD_DIM // 2, 1))
        return (y * cos + partner * sin) * scale

    q_out[...] = norm_rope(proj(0), gq_ref[...], DIFF_HEAD_DIM ** -0.5 * LOG2E).astype(BF16)
    k_out[...] = norm_rope(proj(1), gk_ref[...], 1.0).astype(BF16)
    v = proj(2)
    for hd in range(DIFF_HEADS):
        v_out[0, hd, 0] = v[:, hd * DIFF_V_DIM:(hd + 1) * DIFF_V_DIM].T.astype(BF16)
    cb_out[...] = proj(3).astype(BF16)
    u_out[...] = proj(4) * proj(5)
    mq = proj(6)
    for hd in range(MEM_HEADS):
        sl = slice(hd * MEM_HEAD_DIM, (hd + 1) * MEM_HEAD_DIM)
        mq_out[:, sl] = (_rms(mq[:, sl]) * gqm_ref[...]
                         * (MEM_HEAD_DIM ** -0.5 * LOG2E)).astype(BF16)
    for s in range(7, N_SEC):
        a = proj(s)
        sig_out[:, (s - 7) * SEC:(s - 6) * SEC] = (1.0 / (1.0 + jnp.exp(-a))).astype(BF16)


def _in_proj(x2d, g_mix, w_in, cos_t, sin_t, bd, gq, gk, gqm, seq):
    t = x2d.shape[0]
    tm = TM_PROJ
    tiles_per_seq = seq // tm
    tiles_per_key = TQ // tm
    row = lambda i: (i, 0)
    const = lambda i: (0, 0)
    outs = [
        jax.ShapeDtypeStruct((t, SEC), BF16),
        jax.ShapeDtypeStruct((t, SEC), BF16),
        jax.ShapeDtypeStruct((t // seq, DIFF_HEADS, seq // TQ, DIFF_V_DIM, TQ), BF16),
        jax.ShapeDtypeStruct((t, SEC), BF16),
        jax.ShapeDtypeStruct((t, SEC), F32),
        jax.ShapeDtypeStruct((t, SEC), BF16),
        jax.ShapeDtypeStruct((t, 3 * D_MODEL), BF16),
    ]
    return pl.pallas_call(
        _in_proj_kernel,
        grid=(t // tm,),
        in_specs=[
            pl.BlockSpec((tm, D_MODEL), row),
            pl.BlockSpec((1, D_MODEL), const),
            pl.BlockSpec((D_MODEL, N_SEC * SEC), const),
            pl.BlockSpec((tm, LANES), lambda i: (i % tiles_per_seq, 0)),
            pl.BlockSpec((tm, LANES), lambda i: (i % tiles_per_seq, 0)),
            pl.BlockSpec((SEC, SEC), const),
            pl.BlockSpec((1, SEC), const),
            pl.BlockSpec((1, SEC), const),
            pl.BlockSpec((1, MEM_HEAD_DIM), const),
        ],
        out_specs=[
            pl.BlockSpec((1, DIFF_HEADS, 1, DIFF_V_DIM, tm),
                         lambda i: (i // tiles_per_seq, 0, (i % tiles_per_seq) // tiles_per_key,
                                    0, i % tiles_per_key))
            if o.ndim == 5 else pl.BlockSpec((tm, o.shape[1]), row) for o in outs],
        out_shape=outs,
        compiler_params=pltpu.CompilerParams(vmem_limit_bytes=VMEM_LIMIT),
        name="in_proj",
    )(x2d, g_mix, w_in, cos_t, sin_t, bd, gq, gk, gqm)


def _diff_attn_kernel(fixed_ref, q_ref, qn_ref, k_ref, vt_ref, lam_ref, gs_ref, bound_ref, o_ref,
                      acc_sc, s0_sc, l_sc, *, lam_init):
    i = pl.program_id(2)
    tq = q_ref.shape[0]
    def stacked_queries(ref):
        qt = ref[...].astype(F32).T
        dim = lax.broadcasted_iota(jnp.int32, qt.shape, 0)
        return jnp.concatenate([jnp.where(dim < DIFF_HEAD_DIM, qt, 0.0),
                                jnp.where(dim >= DIFF_HEAD_DIM, qt, 0.0)], axis=1).astype(BF16)

    qq = stacked_queries(q_ref)
    acc_sc[...] = jnp.zeros(acc_sc.shape, F32)
    n_strips = 2 * tq // ATTN_STRIP

    @pl.when(i == 0)
    def _():
        s0_sc[...] = jnp.dot(k_ref[0:tq, :], qq[:, 0:ATTN_STRIP], preferred_element_type=F32)

    def scores(j, c, nk):
        off = pl.multiple_of(j * tq, tq)
        return jnp.dot(k_ref[pl.ds(off, nk), :], qq[:, c * ATTN_STRIP:(c + 1) * ATTN_STRIP],
                       preferred_element_type=F32)

    def step(j, ms, ls, masked, fixed):
        q_offs = [(c * ATTN_STRIP) % tq for c in range(n_strips)]
        nks = [min(tq, qo + ATTN_STRIP) if masked else tq for qo in q_offs]
        ms_new, ls_new = [], []
        s = s0_sc[0:nks[0], :]
        for c in range(n_strips):
            cols = slice(c * ATTN_STRIP, (c + 1) * ATTN_STRIP)
            if c + 1 < n_strips:
                s_next = scores(j, c + 1, nks[c + 1])
            elif not masked:
                s0_sc[...] = scores(j + 1, 0, tq)
            if masked:
                r = lax.broadcasted_iota(jnp.int32, s.shape, 0)
                col = lax.broadcasted_iota(jnp.int32, s.shape, 1)
                s = jnp.where((r // CHUNK) <= ((col + q_offs[c]) // CHUNK), s, NEG_BIG)
            if fixed:
                p = jnp.exp2(s - bound_ref[...])
                ls_new.append(ls[c] + jnp.sum(p, axis=0, keepdims=True))
                ms_new.append(ms[c])
                acc_sc[:, cols] = acc_sc[:, cols] + jnp.dot(
                    vt_ref[0, 0, j][:, :nks[c]], p.astype(BF16), preferred_element_type=F32)
            else:
                m_new = jnp.maximum(ms[c], jnp.max(s, axis=0, keepdims=True))
                p = jnp.exp2(s - m_new)
                alpha = jnp.exp2(ms[c] - m_new)
                ls_new.append(alpha * ls[c] + jnp.sum(p, axis=0, keepdims=True))
                ms_new.append(m_new)
                acc_sc[:, cols] = alpha * acc_sc[:, cols] + jnp.dot(
                    vt_ref[0, 0, j][:, :nks[c]], p.astype(BF16), preferred_element_type=F32)
            if c + 1 < n_strips:
                s = s_next
        return tuple(ms_new), tuple(ls_new)

    def run(fixed):
        m0 = tuple(jnp.full((1, ATTN_STRIP), NEG_BIG, F32) for _ in range(n_strips))
        l0 = tuple(jnp.zeros((1, ATTN_STRIP), F32) for _ in range(n_strips))
        m, l = lax.fori_loop(0, i, lambda j, c: step(j, c[0], c[1], False, fixed), (m0, l0))
        m, l = step(i, m, l, True, fixed)
        l_sc[...] = jnp.concatenate(l, axis=1)

    @pl.when(fixed_ref[0] == 1)
    def _():
        run(True)

    @pl.when(fixed_ref[0] != 1)
    def _():
        run(False)

    qt_next = qn_ref[0:ATTN_STRIP, :].astype(F32).T
    dim_next = lax.broadcasted_iota(jnp.int32, qt_next.shape, 0)
    s0_sc[...] = jnp.dot(k_ref[0:tq, :], jnp.where(dim_next < DIFF_HEAD_DIM, qt_next, 0.0).astype(BF16),
                         preferred_element_type=F32)

    ot = acc_sc[...] / l_sc[...]
    lam = (jnp.exp(jnp.sum(lam_ref[0:1, :] * lam_ref[1:2, :], axis=-1, keepdims=True))
           - jnp.exp(jnp.sum(lam_ref[2:3, :] * lam_ref[3:4, :], axis=-1, keepdims=True))
           + lam_init)
    d = (ot[:, :tq] - lam * ot[:, tq:]).T
    o_ref[...] = (_rms(d) * gs_ref[...] * (1.0 - lam_init)).astype(BF16)


def _diff_attn(qn, kn, vt, lam_rows, g_subln, score_bound, batch, seq, lam_init):
    t = qn.shape[0]
    nq = seq // TQ
    use_fixed = (score_bound <= MAX_FIXED_SHIFT).astype(jnp.int32).reshape(1)
    grid_spec = pltpu.PrefetchScalarGridSpec(
        num_scalar_prefetch=1,
        grid=(batch, DIFF_HEADS, nq),
        in_specs=[
            pl.BlockSpec((TQ, DIFF_V_DIM), lambda b, h, i, *_: (b * nq + i, h)),
            pl.BlockSpec((TQ, DIFF_V_DIM), lambda b, h, i, *_: (b * nq + jnp.minimum(i + 1, nq - 1), h)),
            pl.BlockSpec((seq, DIFF_V_DIM), lambda b, h, i, *_: (b, h)),
            pl.BlockSpec((1, 1, nq, DIFF_V_DIM, TQ), lambda b, h, i, *_: (b, h, 0, 0, 0)),
            pl.BlockSpec((4, DIFF_HEAD_DIM), lambda b, h, i, *_: (0, 0)),
            pl.BlockSpec((1, DIFF_V_DIM), lambda b, h, i, *_: (0, 0)),
            pl.BlockSpec((1, 1), lambda b, h, i, *_: (0, 0)),
        ],
        out_specs=pl.BlockSpec((TQ, DIFF_V_DIM), lambda b, h, i, *_: (b * nq + i, h)),
        scratch_shapes=[pltpu.VMEM((DIFF_V_DIM, 2 * TQ), F32), pltpu.VMEM((TQ, ATTN_STRIP), F32),
                        pltpu.VMEM((1, 2 * TQ), F32)],
    )
    return pl.pallas_call(
        functools.partial(_diff_attn_kernel, lam_init=lam_init),
        grid_spec=grid_spec,
        out_shape=jax.ShapeDtypeStruct((t, SEC), BF16),
        compiler_params=pltpu.CompilerParams(
            dimension_semantics=("arbitrary", "arbitrary", "arbitrary"),
            vmem_limit_bytes=VMEM_LIMIT),
        name="diff_attn",
    )(use_fixed, qn, qn, kn, vt, lam_rows, g_subln, score_bound.reshape(1, 1))


def _row_span(ref, row, n):
    return ref.at[pl.ds(pl.multiple_of(row * ROW_CHUNKS, ROW_CHUNKS), n * ROW_CHUNKS), :]


def _chunk_rows(row0, n, c):
    return pl.ds(row0 * ROW_CHUNKS + c, n, stride=ROW_CHUNKS)


def _load_rows(ref, row0, n):
    return jnp.concatenate([ref[_chunk_rows(row0, n, c), :] for c in range(ROW_CHUNKS)], axis=1)


def _store_rows(ref, n, val):
    for c in range(ROW_CHUNKS):
        ref[_chunk_rows(0, n, c), :] = val[:, c * LANES:(c + 1) * LANES]


def _copy_run(src_hbm, buf, sem, src_row, dst_row, n, max_n):
    piece = 1 << (max_n.bit_length() - 1)
    while piece >= 1:
        take = (n & piece) != 0

        @pl.when(take)
        def _(piece=piece, src_row=src_row, dst_row=dst_row):
            pltpu.make_async_copy(_row_span(src_hbm, src_row, piece), _row_span(buf, dst_row, piece),
                                  sem).start()

        step = jnp.where(take, piece, 0)
        src_row = src_row + step
        dst_row = dst_row + step
        piece //= 2


def _wait_rows(n, src_hbm, buf, sem, base):
    pltpu.make_async_copy(_row_span(src_hbm, 0, n), _row_span(buf, base, n), sem).wait()


def _post_kernel(x_ref, yd_ref, cb_ref, u_ref, up_ref, mq_ref, sig_ref, km_ref, vm_ref,
                 cw_ref, wb_ref, wo_ref, gf_ref, wr_ref, br_ref, tri_ref,
                 x1_out, rows_out, slot_out, gate_out, cbase_out, ntile_out, off_out, cnt_out, carry_sc,
                 *, seq):
    i = pl.program_id(0)
    tm = x_ref.shape[0]

    @pl.when(i == 0)
    def _():
        carry_sc[...] = jnp.zeros(carry_sc.shape, F32)

    u = u_ref[...]
    seq_start = (i * tm) % seq == 0
    up = jnp.where(seq_start, 0.0, up_ref[...])
    r = lax.broadcasted_iota(jnp.int32, u.shape, 0)
    u1 = jnp.where(r == 0, up[7:8, :], pltpu.roll(u, 1, 0))
    u2 = jnp.where(r == 0, up[6:7, :], jnp.where(r == 1, up[7:8, :], pltpu.roll(u, 2, 0)))
    y_conv = cb_ref[...].astype(F32) * (cw_ref[0:1, :] * u2 + cw_ref[1:2, :] * u1 + cw_ref[2:3, :] * u)

    def mem_scores(hd):
        sl = slice(hd * MEM_HEAD_DIM, (hd + 1) * MEM_HEAD_DIM)
        return jnp.dot(mq_ref[:, sl], km_ref[sl, :], preferred_element_type=F32)

    y_mem = []
    s_next = mem_scores(0)
    for hd in range(MEM_HEADS):
        sl = slice(hd * MEM_HEAD_DIM, (hd + 1) * MEM_HEAD_DIM)
        s = s_next
        if hd + 1 < MEM_HEADS:
            s_next = mem_scores(hd + 1)
        if hd == 0:
            merged = (sig_ref[:, 0:D_MODEL].astype(F32)
                      * jnp.dot(yd_ref[...], wb_ref[0], preferred_element_type=F32))
        elif hd == 1:
            merged += (sig_ref[:, D_MODEL:2 * D_MODEL].astype(F32)
                       * jnp.dot(y_conv.astype(BF16), wb_ref[1], preferred_element_type=F32))
        p = jnp.exp2(s - jnp.max(s, axis=-1, keepdims=True))
        o = jnp.dot(p.astype(BF16), vm_ref[:, sl], preferred_element_type=F32)
        y_mem.append(o / jnp.sum(p, axis=-1, keepdims=True))

    merged += (sig_ref[:, 2 * D_MODEL:3 * D_MODEL].astype(F32)
               * jnp.dot(jnp.concatenate(y_mem, axis=1).astype(BF16), wb_ref[2],
                         preferred_element_type=F32))
    x1 = x_ref[...] + jnp.dot(merged.astype(BF16), wo_ref[...], preferred_element_type=F32)
    x1_out[...] = x1

    h2 = _rms(x1) * gf_ref[...]

    w = wr_ref[...]
    w_hi = w.astype(BF16)
    w_lo = (w - w_hi.astype(F32)).astype(BF16)
    h_hi = h2.astype(BF16)
    h_lo = (h2 - h_hi.astype(F32)).astype(BF16)
    part = jnp.dot(h_hi, jnp.concatenate([w_hi, w_lo], axis=1), preferred_element_type=F32)
    by_token = (part[:, :N_EXPERTS] + part[:, N_EXPERTS:]
                + jnp.dot(h_lo, w_hi, preferred_element_type=F32))
    padded = jnp.concatenate([by_token, jnp.zeros((tm, LANES - N_EXPERTS), F32)], axis=1)
    logits = padded.T[:N_EXPERTS, :] + br_ref[...]
    eio = lax.broadcasted_iota(jnp.int32, logits.shape, 0)
    work = logits
    vals, hots = [], []
    for k in range(TOP_K):
        mk = jnp.max(work, axis=0, keepdims=True)
        ik = jnp.min(jnp.where(work == mk, eio, N_EXPERTS), axis=0, keepdims=True)
        hot = eio == ik
        work = jnp.where(hot, -jnp.inf, work)
        vals.append(mk)
        hots.append(hot)
    ex = [jnp.exp(v - vals[0]) for v in vals]
    den = ex[0] + ex[1] + ex[2] + ex[3]
    gates = [e / den for e in ex]

    assign = jnp.zeros(logits.shape, F32)
    for hot in hots:
        assign = jnp.where(hot, 1.0, assign)
    earlier = jnp.dot(assign.astype(BF16), tri_ref[...], preferred_element_type=F32)
    n_col = jnp.sum(assign, axis=1, keepdims=True)
    e_row = lax.broadcasted_iota(jnp.int32, (N_EXPERTS, LANES), 0)
    run = jnp.broadcast_to(n_col, (N_EXPERTS, LANES))
    shift = 1
    while shift < N_EXPERTS:
        run = run + jnp.where(e_row >= shift, pltpu.roll(run, shift, 0), 0.0)
        shift *= 2
    off_col = run[:, 0:1] - n_col
    slots = [jnp.sum(jnp.where(hot, earlier + off_col, 0.0), axis=0, keepdims=True).astype(jnp.int32)
             for hot in hots]
    for k in range(TOP_K):
        slot_out[k:k + 1, :] = slots[k]
        gate_out[k:k + 1, :] = gates[k]

    n_slots = TOP_K * tm
    jdx = lax.broadcasted_iota(jnp.int32, (n_slots, tm), 0)
    pick = jnp.where(jdx == slots[0], 1.0, jnp.where(jdx == slots[1], 1.0, jnp.where(
        jdx == slots[2], 1.0, jnp.where(jdx == slots[3], 1.0, 0.0)))).astype(BF16)
    rows = jnp.dot(pick, h_hi, preferred_element_type=F32)
    _store_rows(rows_out, n_slots, rows)

    cbase_out[...] = jnp.broadcast_to(carry_sc[...], cbase_out.shape).astype(jnp.int32)
    ntile_out[...] = jnp.broadcast_to(n_col, ntile_out.shape).astype(jnp.int32)
    off_out[...] = jnp.broadcast_to(off_col, off_out.shape).astype(jnp.int32)
    carry_sc[...] = carry_sc[...] + n_col
    cnt_out[...] = jnp.broadcast_to(carry_sc[...], cnt_out.shape).astype(jnp.int32)


def _post(x2d, y_diff, cb, u, mqn, sig, km, vm, conv_w, w_branch, w_out, g_ffn, w_rt, b_r, tri, seq):
    t = x2d.shape[0]
    tm = TM_POST
    n_tiles = t // tm
    row = lambda i: (i, 0)
    const = lambda i: (0, 0)
    table = jax.ShapeDtypeStruct((n_tiles * N_EXPERTS, LANES), jnp.int32)
    outs = [
        jax.ShapeDtypeStruct((t, D_MODEL), F32),
        jax.ShapeDtypeStruct((TOP_K * t * ROW_CHUNKS, LANES), F32),
        jax.ShapeDtypeStruct((TOP_K, t), jnp.int32),
        jax.ShapeDtypeStruct((TOP_K, t), F32),
        table, table, table,
        jax.ShapeDtypeStruct((N_EXPERTS, LANES), jnp.int32),
    ]
    return pl.pallas_call(
        functools.partial(_post_kernel, seq=seq),
        grid=(t // tm,),
        in_specs=[
            pl.BlockSpec((tm, D_MODEL), row),
            pl.BlockSpec((tm, SEC), row),
            pl.BlockSpec((tm, SEC), row),
            pl.BlockSpec((tm, SEC), row),
            pl.BlockSpec((8, SEC), lambda i: (jnp.maximum(i * (tm // 8) - 1, 0), 0)),
            pl.BlockSpec((tm, SEC), row),
            pl.BlockSpec((tm, 3 * D_MODEL), row),
            pl.BlockSpec((SEC, N_MEM), lambda i: ((i * tm) // seq, 0)),
            pl.BlockSpec((N_MEM, SEC), lambda i: ((i * tm) // seq, 0)),
            pl.BlockSpec((3, SEC), const),
            pl.BlockSpec((3, SEC, D_MODEL), lambda i: (0, 0, 0)),
            pl.BlockSpec((D_MODEL, D_MODEL), const),
            pl.BlockSpec((1, D_MODEL), const),
            pl.BlockSpec((D_MODEL, N_EXPERTS), const),
            pl.BlockSpec((N_EXPERTS, 1), const),
            pl.BlockSpec((tm, tm), const),
        ],
        out_specs=[
            pl.BlockSpec((tm, D_MODEL), row),
            pl.BlockSpec((TOP_K * tm * ROW_CHUNKS, LANES), row),
            pl.BlockSpec((TOP_K, tm), lambda i: (0, i)),
            pl.BlockSpec((TOP_K, tm), lambda i: (0, i)),
            pl.BlockSpec((N_EXPERTS, LANES), row),
            pl.BlockSpec((N_EXPERTS, LANES), row),
            pl.BlockSpec((N_EXPERTS, LANES), row),
            pl.BlockSpec((N_EXPERTS, LANES), const),
        ],
        out_shape=outs,
        scratch_shapes=[pltpu.VMEM((N_EXPERTS, 1), F32)],
        compiler_params=pltpu.CompilerParams(dimension_semantics=("arbitrary",),
                                             vmem_limit_bytes=VMEM_LIMIT),
        name="post",
    )(x2d, y_diff, cb, u, u, mqn, sig, km, vm, conv_w, w_branch, w_out, g_ffn, w_rt, b_r, tri)


def _experts_kernel(be_ref, t0_ref, na_ref, ps_ref, cnt_ref, cb_ref, nt_ref, of_ref,
                    rows_hbm, zeros_hbm, wgu_f32, bgu_ref, wd_f32, bd_ref,
                    ys_out, xbuf, sem, wgu_ref, wd_ref, *, n_tiles, slots_per_tile):
    b = pl.program_id(0)
    n_active = na_ref[0]
    slot = b % 2

    @pl.when(jnp.logical_and(b < n_active,
                             jnp.logical_or(b == 0, be_ref[b] != be_ref[jnp.maximum(b - 1, 0)])))
    def _():
        wgu_ref[...] = wgu_f32[0].astype(BF16)
        wd_ref[...] = wd_f32[0].astype(BF16)

    def issue(blk, buf_slot):
        e = be_ref[blk]
        r0 = blk * MOE_BLOCK - ps_ref[e]
        r1 = r0 + MOE_BLOCK
        base = buf_slot * MOE_BLOCK
        sm = sem.at[buf_slot]

        def cond(i):
            return jnp.logical_and(i < n_tiles, cb_ref[e * n_tiles + jnp.minimum(i, n_tiles - 1)] < r1)

        def body(i):
            c = cb_ref[e * n_tiles + i]
            lo = jnp.maximum(c, r0)
            hi = jnp.minimum(c + nt_ref[e * n_tiles + i], r1)
            src = i * slots_per_tile + of_ref[e * n_tiles + i] + (lo - c)
            _copy_run(rows_hbm, xbuf, sm, src, base + (lo - r0), jnp.maximum(hi - lo, 0),
                      min(TM_POST, MOE_BLOCK))
            return i + 1

        lax.while_loop(cond, body, t0_ref[blk])
        valid = jnp.clip(cnt_ref[e] - r0, 0, MOE_BLOCK)
        _copy_run(zeros_hbm, xbuf, sm, 0, base + valid, MOE_BLOCK - valid, MOE_BLOCK)

    @pl.when(b == 0)
    def _():
        issue(b, 0)

    @pl.when(b + 1 < n_active)
    def _():
        issue(b + 1, 1 - slot)

    def compute(n_rows):
        base = pl.multiple_of(slot * MOE_BLOCK, MOE_BLOCK)
        _wait_rows(MOE_BLOCK, rows_hbm, xbuf, sem.at[slot], base)
        x = _load_rows(xbuf, base, n_rows).astype(BF16)
        d_ff = wd_ref.shape[0]

        def gate_up(j):
            gs = slice(j * FF_CHUNK, (j + 1) * FF_CHUNK)
            us = slice(d_ff + j * FF_CHUNK, d_ff + (j + 1) * FF_CHUNK)
            return (jnp.dot(x, wgu_ref[:, gs], preferred_element_type=F32) + bgu_ref[0, :, gs],
                    jnp.dot(x, wgu_ref[:, us], preferred_element_type=F32) + bgu_ref[0, :, us])

        acts = []
        nxt = gate_up(0)
        for j in range(d_ff // FF_CHUNK):
            g, u = nxt
            if (j + 1) * FF_CHUNK < d_ff:
                nxt = gate_up(j + 1)
            g = jnp.minimum(g, SWIGLU_LIMIT)
            u = jnp.clip(u, -SWIGLU_LIMIT, SWIGLU_LIMIT)
            acts.append(((u + 1.0) * (g * (1.0 / (1.0 + jnp.exp(-SWIGLU_ALPHA * g))))).astype(BF16))
        act = jnp.concatenate(acts, axis=1)
        for n in range(D_MODEL // FF_CHUNK):
            cols = slice(n * FF_CHUNK, (n + 1) * FF_CHUNK)
            yn = jnp.dot(act, wd_ref[:, cols], preferred_element_type=F32) + bd_ref[0, :, cols]
            for c in range(FF_CHUNK // LANES):
                ys_out[_chunk_rows(0, n_rows, n * (FF_CHUNK // LANES) + c), :] = (
                    yn[:, c * LANES:(c + 1) * LANES])
        if n_rows < MOE_BLOCK:
            ys_out[n_rows * ROW_CHUNKS:, :] = jnp.zeros(((MOE_BLOCK - n_rows) * ROW_CHUNKS, LANES), F32)

    @pl.when(b < n_active)
    def _():
        compute(MOE_BLOCK)

    @pl.when(b >= n_active)
    def _():
        ys_out[...] = jnp.zeros(ys_out.shape, F32)


def _experts(block_expert, first_tile, n_active, pstart, counts, cbase_e, ntile_e, off_e,
             rows, zero_rows, wgu, bgu, wd, bd, n_tiles):
    n_blocks = block_expert.shape[0]
    ff2 = wgu.shape[2]
    by_expert = lambda b, be, *_: (be[b], 0, 0)
    grid_spec = pltpu.PrefetchScalarGridSpec(
        num_scalar_prefetch=8,
        grid=(n_blocks,),
        in_specs=[
            pl.BlockSpec(memory_space=pl.ANY),
            pl.BlockSpec(memory_space=pl.ANY),
            pl.BlockSpec((1, D_MODEL, ff2), by_expert),
            pl.BlockSpec((1, 1, ff2), by_expert),
            pl.BlockSpec((1, ff2 // 2, D_MODEL), by_expert),
            pl.BlockSpec((1, 1, D_MODEL), by_expert),
        ],
        out_specs=pl.BlockSpec((MOE_BLOCK * ROW_CHUNKS, LANES), lambda b, *_: (b, 0)),
        scratch_shapes=[
            pltpu.VMEM((2 * MOE_BLOCK * ROW_CHUNKS, LANES), F32),
            pltpu.SemaphoreType.DMA((2,)),
            pltpu.VMEM((D_MODEL, ff2), BF16),
            pltpu.VMEM((ff2 // 2, D_MODEL), BF16),
        ],
    )
    return pl.pallas_call(
        functools.partial(_experts_kernel, n_tiles=n_tiles, slots_per_tile=TOP_K * TM_POST),
        grid_spec=grid_spec,
        out_shape=jax.ShapeDtypeStruct((n_blocks * MOE_BLOCK * ROW_CHUNKS, LANES), F32),
        compiler_params=pltpu.CompilerParams(dimension_semantics=("arbitrary",),
                                             vmem_limit_bytes=VMEM_LIMIT_EXPERTS),
        name="experts",
    )(block_expert, first_tile, n_active, pstart, counts, cbase_e, ntile_e, off_e,
      rows, zero_rows, wgu, bgu, wd, bd)


def _combine_kernel(ps_ref, cb_ref, nt_ref, of_ref, slot_ref, gate_ref, x1_ref, ys_hbm, o_ref, buf,
                    sem):
    i = pl.program_id(0)
    n = pl.num_programs(0)
    tm = x1_ref.shape[0]
    n_slots = TOP_K * tm
    cur = i % 2

    def issue(tile, buf_slot):
        def body(e, carry):
            k = tile * N_EXPERTS + e
            _copy_run(ys_hbm, buf, sem.at[buf_slot], ps_ref[e] + cb_ref[k],
                      buf_slot * n_slots + of_ref[k], nt_ref[k], tm)
            return carry
        lax.fori_loop(0, N_EXPERTS, body, 0)

    @pl.when(i == 0)
    def _():
        issue(i, 0)

    @pl.when(i + 1 < n)
    def _():
        issue(i + 1, 1 - cur)

    base = pl.multiple_of(cur * n_slots, n_slots)
    _wait_rows(n_slots, ys_hbm, buf, sem.at[cur], base)
    y = _load_rows(buf, base, n_slots).astype(BF16)
    jdx = lax.broadcasted_iota(jnp.int32, (n_slots, tm), 0)
    pick = jnp.where(jdx == slot_ref[0:1, :], gate_ref[0:1, :], jnp.where(
        jdx == slot_ref[1:2, :], gate_ref[1:2, :], jnp.where(
            jdx == slot_ref[2:3, :], gate_ref[2:3, :], jnp.where(
                jdx == slot_ref[3:4, :], gate_ref[3:4, :], 0.0)))).astype(BF16)
    o_ref[...] = x1_ref[...] + lax.dot_general(pick, y, (((0,), (0,)), ((), ())),
                                               preferred_element_type=F32)


def _combine(pstart, cbase_t, ntile_t, off_t, slots, gates, x1, ys_rows):
    t = x1.shape[0]
    tm = TM_POST
    n_slots = TOP_K * tm
    grid_spec = pltpu.PrefetchScalarGridSpec(
        num_scalar_prefetch=4,
        grid=(t // tm,),
        in_specs=[
            pl.BlockSpec((TOP_K, tm), lambda i, *_: (0, i)),
            pl.BlockSpec((TOP_K, tm), lambda i, *_: (0, i)),
            pl.BlockSpec((tm, D_MODEL), lambda i, *_: (i, 0)),
            pl.BlockSpec(memory_space=pl.ANY),
        ],
        out_specs=pl.BlockSpec((tm, D_MODEL), lambda i, *_: (i, 0)),
        scratch_shapes=[
            pltpu.VMEM((2 * n_slots * ROW_CHUNKS, LANES), F32),
            pltpu.SemaphoreType.DMA((2,)),
        ],
    )
    return pl.pallas_call(
        _combine_kernel,
        grid_spec=grid_spec,
        out_shape=jax.ShapeDtypeStruct((t, D_MODEL), F32),
        compiler_params=pltpu.CompilerParams(dimension_semantics=("arbitrary",),
                                             vmem_limit_bytes=VMEM_LIMIT),
        name="combine",
    )(pstart, cbase_t, ntile_t, off_t, slots, gates, x1, ys_rows)


def kernel(x, mem, g_mix, w_in, g_q_diff, g_k_diff, lambda_q1, lambda_k1, lambda_q2, lambda_k2,
           g_subln, conv_w, g_mem, w_mem_kv, g_q_mem, g_k_mem, w_branch, w_out, g_ffn, w_router,
           b_router, w_gate_up, b_gate_up, w_down, b_down):
    batch, seq, d = x.shape
    t = batch * seq
    depth = g_mix.shape[0]
    assert depth == 1 and d == D_MODEL and seq % TQ == 0 and t % TM_PROJ == 0

    pos = jnp.arange(seq, dtype=F32)
    inv_freq = 1.0 / (ROPE_THETA ** (jnp.arange(0, DIFF_HEAD_DIM, 2, dtype=F32) / DIFF_HEAD_DIM))
    ang = pos[:, None] * inv_freq[None, :]
    cos_t = jnp.tile(jnp.cos(ang), (1, 4))
    sin_t = jnp.tile(jnp.concatenate([-jnp.sin(ang), jnp.sin(ang)], axis=1), (1, 2))
    grp = jnp.arange(SEC) // DIFF_HEAD_DIM
    bd = jnp.where(grp[:, None] == grp[None, :], 1.0 / DIFF_HEAD_DIM, 0.0).astype(BF16)
    tok = jnp.arange(TM_POST)
    tri = (tok[:, None] < tok[None, :]).astype(BF16)

    x2d = x.reshape(t, d)
    l = 0
    lam_init = 0.8 - 0.6 * math.exp(-0.3 * l)
    km, vm = _mem_kv(mem.reshape(batch * N_MEM, d), g_mem[l][None, :], w_mem_kv[l].astype(BF16),
                     g_k_mem[l][None, :])
    qn, kn, vt, cb, u, mqn, sig = _in_proj(
        x2d, g_mix[l][None, :], w_in[l].astype(BF16), cos_t, sin_t, bd,
        jnp.tile(g_q_diff[l], SEC // DIFF_HEAD_DIM)[None, :],
        jnp.tile(g_k_diff[l], SEC // DIFF_HEAD_DIM)[None, :],
        g_q_mem[l][None, :], seq)
    lam_rows = jnp.stack([lambda_q1[l], lambda_k1[l], lambda_q2[l], lambda_k2[l]]).astype(F32)
    score_bound = (DIFF_HEAD_DIM * (DIFF_HEAD_DIM ** -0.5 * LOG2E)
                   * jnp.max(jnp.abs(g_q_diff[l])) * jnp.max(jnp.abs(g_k_diff[l]))).astype(F32)
    y_diff = _diff_attn(qn, kn, vt, lam_rows, g_subln[l][None, :], score_bound, batch, seq, lam_init)
    x1, rows, slots, gates, cbase, ntile, off, counts = _post(
        x2d, y_diff, cb, u, mqn, sig, km, vm, conv_w[l], w_branch[l].astype(BF16),
        w_out[l].astype(BF16), g_ffn[l][None, :], w_router[l], b_router[l][:, None], tri, seq)

    n_tiles = t // TM_POST
    n_assign = t * TOP_K
    n_rows = -(-(n_assign + N_EXPERTS * (MOE_BLOCK - 1)) // MOE_BLOCK) * MOE_BLOCK
    n_blocks = n_rows // MOE_BLOCK
    cnt = counts[:, 0]
    padded = (cnt + MOE_BLOCK - 1) // MOE_BLOCK * MOE_BLOCK
    pend = jnp.cumsum(padded)
    pstart = pend - padded
    block_row0 = jnp.arange(n_blocks, dtype=jnp.int32) * MOE_BLOCK
    block_expert = jnp.minimum(
        jnp.sum((pend[None, :] <= block_row0[:, None]).astype(jnp.int32), axis=1), N_EXPERTS - 1)
    n_active = (pend[-1:] // MOE_BLOCK).astype(jnp.int32)
    cbase_t = cbase[:, 0].reshape(n_tiles, N_EXPERTS)
    ntile_t = ntile[:, 0].reshape(n_tiles, N_EXPERTS)
    off_t = off[:, 0].reshape(n_tiles, N_EXPERTS)
    cend_b = (cbase_t + ntile_t).T[block_expert]
    r0 = block_row0 - pstart[block_expert]
    first_tile = jnp.sum((cend_b <= r0[:, None]).astype(jnp.int32), axis=1)
    first_tile = jnp.minimum(first_tile, n_tiles - 1).astype(jnp.int32)

    zero_rows = jnp.zeros((MOE_BLOCK * ROW_CHUNKS, LANES), F32)
    ys_rows = _experts(block_expert, first_tile, n_active, pstart, cnt,
                       cbase_t.T.reshape(-1), ntile_t.T.reshape(-1), off_t.T.reshape(-1),
                       rows, zero_rows, w_gate_up[l], b_gate_up[l][:, None, :],
                       w_down[l], b_down[l][:, None, :], n_tiles)
    out = _combine(pstart, cbase_t.reshape(-1), ntile_t.reshape(-1), off_t.reshape(-1), slots, gates,
                   x1, ys_rows)
    return out.reshape(batch, seq, d)
```

```python
import functools
import math

import jax
import jax.numpy as jnp
from jax import lax
from jax.experimental import pallas as pl
from jax.experimental.pallas import tpu as pltpu

F32 = jnp.float32
BF16 = jnp.bfloat16

D_MODEL = 1024
CHUNK = 64
EPS = 1e-6
ROPE_THETA = 10000.0
DIFF_HEADS = 4
DIFF_HEAD_DIM = 64
DIFF_V_DIM = 2 * DIFF_HEAD_DIM
N_MEM = 256
MEM_HEADS = 4
MEM_HEAD_DIM = 128
SEC = 512
N_SEC = 13
N_EXPERTS = 32
TOP_K = 4
SWIGLU_LIMIT = 7.0
SWIGLU_ALPHA = 1.702
MOE_BLOCK = 512
LANES = 128
ROW_CHUNKS = D_MODEL // LANES
LOG2E = 1.4426950408889634
NEG_BIG = -1e30

TM_PROJ = 512
TQ = 1024
ATTN_STRIP = 512
MAX_FIXED_SHIFT = 40.0
TM_POST = 256
FF_CHUNK = 256
VMEM_LIMIT = 48 * 1024 * 1024
VMEM_LIMIT_EXPERTS = 56 * 1024 * 1024


def _rms(x, eps=EPS):
    return x * lax.rsqrt(jnp.mean(x * x, axis=-1, keepdims=True) + eps)


def _mem_kv_kernel(mem_ref, g_ref, w_ref, gk_ref, k_out, v_out):
    h = (_rms(mem_ref[...]) * g_ref[...]).astype(BF16)
    kv = jnp.dot(h, w_ref[...], preferred_element_type=F32)
    for hd in range(MEM_HEADS):
        sl = slice(hd * MEM_HEAD_DIM, (hd + 1) * MEM_HEAD_DIM)
        k_out[sl, :] = (_rms(kv[:, sl]) * gk_ref[...]).T.astype(BF16)
    v_out[...] = kv[:, SEC:].astype(BF16)


def _mem_kv(mem2d, g_mem, w_kv, g_k):
    rows = mem2d.shape[0]
    return pl.pallas_call(
        _mem_kv_kernel,
        grid=(rows // N_MEM,),
        in_specs=[
            pl.BlockSpec((N_MEM, D_MODEL), lambda i: (i, 0)),
            pl.BlockSpec((1, D_MODEL), lambda i: (0, 0)),
            pl.BlockSpec((D_MODEL, 2 * SEC), lambda i: (0, 0)),
            pl.BlockSpec((1, MEM_HEAD_DIM), lambda i: (0, 0)),
        ],
        out_specs=[pl.BlockSpec((SEC, N_MEM), lambda i: (i, 0)),
                   pl.BlockSpec((N_MEM, SEC), lambda i: (i, 0))],
        out_shape=[jax.ShapeDtypeStruct((rows // N_MEM * SEC, N_MEM), BF16),
                   jax.ShapeDtypeStruct((rows, SEC), BF16)],
        compiler_params=pltpu.CompilerParams(vmem_limit_bytes=VMEM_LIMIT),
        name="mem_kv",
    )(mem2d, g_mem, w_kv, g_k)


def _in_proj_kernel(x_ref, g_ref, w_ref, freq_ref, sign_ref, cosr_ref, sinr_ref, bd_ref, gq_ref, gk_ref,
                    gqm_ref, q_out, k_out, v_out, cb_out, u_out, mq_out, sig_out, *, tiles_per_seq):
    tm = x_ref.shape[0]
    h = (_rms(x_ref[...]) * g_ref[...]).astype(BF16)

    def proj(sec):
        return jnp.dot(h, w_ref[:, sec * SEC:(sec + 1) * SEC], preferred_element_type=F32)

    base = ((pl.program_id(0) % tiles_per_seq) * tm).astype(F32) * freq_ref[...]
    cos_b, sin_b = jnp.cos(base), jnp.sin(base)
    cos128 = cos_b * cosr_ref[...] - sin_b * sinr_ref[...]
    sin128 = (sin_b * cosr_ref[...] + cos_b * sinr_ref[...]) * sign_ref[...]
    cos = jnp.concatenate([cos128] * (SEC // LANES), axis=1)
    sin = jnp.concatenate([sin128] * (SEC // LANES), axis=1)
    lane = lax.broadcasted_iota(jnp.int32, (tm, SEC), 1)
    first_half = (lane & (DIFF_HEAD_DIM // 2)) == 0

    def norm_rope(a, g, scale):
        ms = jnp.dot((a * a).astype(BF16), bd_ref[...], preferred_element_type=F32)
        y = a * lax.rsqrt(ms + EPS) * g
        partner = jnp.where(first_half,
                            pltpu.roll(y, SEC - DIFF_HEAD_DIM // 2, 1),
                            pltpu.roll(y, DIFF_HEAD_DIM // 2, 1))
        return (y * cos + partner * sin) * scale

    q_out[...] = norm_rope(proj(0), gq_ref[...], DIFF_HEAD_DIM ** -0.5 * LOG2E).astype(BF16)
    k_out[...] = norm_rope(proj(1), gk_ref[...], 1.0).astype(BF16)
    v = proj(2)
    for hd in range(DIFF_HEADS):
        v_out[0, hd, 0] = v[:, hd * DIFF_V_DIM:(hd + 1) * DIFF_V_DIM].T.astype(BF16)
    cb_out[...] = proj(3).astype(BF16)
    u_out[...] = proj(4) * proj(5)
    mq = proj(6)
    for hd in range(MEM_HEADS):
        sl = slice(hd * MEM_HEAD_DIM, (hd + 1) * MEM_HEAD_DIM)
        mq_out[:, sl] = (_rms(mq[:, sl]) * gqm_ref[...]
                         * (MEM_HEAD_DIM ** -0.5 * LOG2E)).astype(BF16)
    for s in range(7, N_SEC):
        a = proj(s)
        sig_out[:, (s - 7) * SEC:(s - 6) * SEC] = (1.0 / (1.0 + jnp.exp(-a))).astype(BF16)


def _in_proj(x2d, g_mix, w_in, freq, sign, cos_r, sin_r, bd, gq, gk, gqm, seq):
    t = x2d.shape[0]
    tm = TM_PROJ
    tiles_per_seq = seq // tm
    tiles_per_key = TQ // tm
    row = lambda i: (i, 0)
    const = lambda i: (0, 0)
    outs = [
        jax.ShapeDtypeStruct((t, SEC), BF16),
        jax.ShapeDtypeStruct((t, SEC), BF16),
        jax.ShapeDtypeStruct((t // seq, DIFF_HEADS, seq // TQ, DIFF_V_DIM, TQ), BF16),
        jax.ShapeDtypeStruct((t, SEC), BF16),
        jax.ShapeDtypeStruct((t, SEC), F32),
        jax.ShapeDtypeStruct((t, SEC), BF16),
        jax.ShapeDtypeStruct((t, 3 * D_MODEL), BF16),
    ]
    return pl.pallas_call(
        functools.partial(_in_proj_kernel, tiles_per_seq=tiles_per_seq),
        grid=(t // tm,),
        in_specs=[
            pl.BlockSpec((tm, D_MODEL), row),
            pl.BlockSpec((1, D_MODEL), const),
            pl.BlockSpec((D_MODEL, N_SEC * SEC), const),
            pl.BlockSpec((1, LANES), const),
            pl.BlockSpec((1, LANES), const),
            pl.BlockSpec((tm, LANES), const),
            pl.BlockSpec((tm, LANES), const),
            pl.BlockSpec((SEC, SEC), const),
            pl.BlockSpec((1, SEC), const),
            pl.BlockSpec((1, SEC), const),
            pl.BlockSpec((1, MEM_HEAD_DIM), const),
        ],
        out_specs=[
            pl.BlockSpec((1, DIFF_HEADS, 1, DIFF_V_DIM, tm),
                         lambda i: (i // tiles_per_seq, 0, (i % tiles_per_seq) // tiles_per_key,
                                    0, i % tiles_per_key))
            if o.ndim == 5 else pl.BlockSpec((tm, o.shape[1]), row) for o in outs],
        out_shape=outs,
        compiler_params=pltpu.CompilerParams(vmem_limit_bytes=VMEM_LIMIT),
        name="in_proj",
    )(x2d, g_mix, w_in, freq, sign, cos_r, sin_r, bd, gq, gk, gqm)


def _diff_attn_kernel(fixed_ref, q_ref, qn_ref, k_ref, vt_ref, lam_ref, gs_ref, bound_ref, o_ref,
                      acc_sc, s0_sc, l_sc, *, lam_init):
    i = pl.program_id(2)
    tq = q_ref.shape[0]
    def stacked_queries(ref):
        qt = ref[...].astype(F32).T
        dim = lax.broadcasted_iota(jnp.int32, qt.shape, 0)
        return jnp.concatenate([jnp.where(dim < DIFF_HEAD_DIM, qt, 0.0),
                                jnp.where(dim >= DIFF_HEAD_DIM, qt, 0.0)], axis=1).astype(BF16)

    qq = stacked_queries(q_ref)
    acc_sc[...] = jnp.zeros(acc_sc.shape, F32)
    n_strips = 2 * tq // ATTN_STRIP

    @pl.when(i == 0)
    def _():
        s0_sc[...] = jnp.dot(k_ref[0:tq, :], qq[:, 0:ATTN_STRIP], preferred_element_type=F32)

    def scores(j, c, nk):
        off = pl.multiple_of(j * tq, tq)
        return jnp.dot(k_ref[pl.ds(off, nk), :], qq[:, c * ATTN_STRIP:(c + 1) * ATTN_STRIP],
                       preferred_element_type=F32)

    def step(j, ms, ls, masked, fixed):
        q_offs = [(c * ATTN_STRIP) % tq for c in range(n_strips)]
        nks = [min(tq, qo + ATTN_STRIP) if masked else tq for qo in q_offs]
        ms_new, ls_new = [], []
        s = s0_sc[0:nks[0], :]
        for c in range(n_strips):
            cols = slice(c * ATTN_STRIP, (c + 1) * ATTN_STRIP)
            if c + 1 < n_strips:
                s_next = scores(j, c + 1, nks[c + 1])
            elif not masked:
                s0_sc[...] = scores(j + 1, 0, tq)
            if masked:
                r = lax.broadcasted_iota(jnp.int32, s.shape, 0)
                col = lax.broadcasted_iota(jnp.int32, s.shape, 1)
                s = jnp.where((r // CHUNK) <= ((col + q_offs[c]) // CHUNK), s, NEG_BIG)
            if fixed:
                p = jnp.exp2(s - bound_ref[...])
                ls_new.append(ls[c] + jnp.sum(p, axis=0, keepdims=True))
                ms_new.append(ms[c])
                acc_sc[:, cols] = acc_sc[:, cols] + jnp.dot(
                    vt_ref[0, 0, j][:, :nks[c]], p.astype(BF16), preferred_element_type=F32)
            else:
                m_new = jnp.maximum(ms[c], jnp.max(s, axis=0, keepdims=True))
                p = jnp.exp2(s - m_new)
                alpha = jnp.exp2(ms[c] - m_new)
                ls_new.append(alpha * ls[c] + jnp.sum(p, axis=0, keepdims=True))
                ms_new.append(m_new)
                acc_sc[:, cols] = alpha * acc_sc[:, cols] + jnp.dot(
                    vt_ref[0, 0, j][:, :nks[c]], p.astype(BF16), preferred_element_type=F32)
            if c + 1 < n_strips:
                s = s_next
        return tuple(ms_new), tuple(ls_new)

    def run(fixed):
        m0 = tuple(jnp.full((1, ATTN_STRIP), NEG_BIG, F32) for _ in range(n_strips))
        l0 = tuple(jnp.zeros((1, ATTN_STRIP), F32) for _ in range(n_strips))
        m, l = lax.fori_loop(0, i, lambda j, c: step(j, c[0], c[1], False, fixed), (m0, l0))
        m, l = step(i, m, l, True, fixed)
        l_sc[...] = jnp.concatenate(l, axis=1)

    @pl.when(fixed_ref[0] == 1)
    def _():
        run(True)

    @pl.when(fixed_ref[0] != 1)
    def _():
        run(False)

    qt_next = qn_ref[0:ATTN_STRIP, :].astype(F32).T
    dim_next = lax.broadcasted_iota(jnp.int32, qt_next.shape, 0)
    s0_sc[...] = jnp.dot(k_ref[0:tq, :], jnp.where(dim_next < DIFF_HEAD_DIM, qt_next, 0.0).astype(BF16),
                         preferred_element_type=F32)

    ot = acc_sc[...] / l_sc[...]
    lam = (jnp.exp(jnp.sum(lam_ref[0:1, :] * lam_ref[1:2, :], axis=-1, keepdims=True))
           - jnp.exp(jnp.sum(lam_ref[2:3, :] * lam_ref[3:4, :], axis=-1, keepdims=True))
           + lam_init)
    d = (ot[:, :tq] - lam * ot[:, tq:]).T
    o_ref[...] = (_rms(d) * gs_ref[...] * (1.0 - lam_init)).astype(BF16)


def _diff_attn(qn, kn, vt, lam_rows, g_subln, score_bound, batch, seq, lam_init):
    t = qn.shape[0]
    nq = seq // TQ
    use_fixed = (score_bound <= MAX_FIXED_SHIFT).astype(jnp.int32).reshape(1)
    grid_spec = pltpu.PrefetchScalarGridSpec(
        num_scalar_prefetch=1,
        grid=(batch, DIFF_HEADS, nq),
        in_specs=[
            pl.BlockSpec((TQ, DIFF_V_DIM), lambda b, h, i, *_: (b * nq + i, h)),
            pl.BlockSpec((TQ, DIFF_V_DIM), lambda b, h, i, *_: (b * nq + jnp.minimum(i + 1, nq - 1), h)),
            pl.BlockSpec((seq, DIFF_V_DIM), lambda b, h, i, *_: (b, h)),
            pl.BlockSpec((1, 1, nq, DIFF_V_DIM, TQ), lambda b, h, i, *_: (b, h, 0, 0, 0)),
            pl.BlockSpec((4, DIFF_HEAD_DIM), lambda b, h, i, *_: (0, 0)),
            pl.BlockSpec((1, DIFF_V_DIM), lambda b, h, i, *_: (0, 0)),
            pl.BlockSpec((1, 1), lambda b, h, i, *_: (0, 0)),
        ],
        out_specs=pl.BlockSpec((TQ, DIFF_V_DIM), lambda b, h, i, *_: (b * nq + i, h)),
        scratch_shapes=[pltpu.VMEM((DIFF_V_DIM, 2 * TQ), F32), pltpu.VMEM((TQ, ATTN_STRIP), F32),
                        pltpu.VMEM((1, 2 * TQ), F32)],
    )
    return pl.pallas_call(
        functools.partial(_diff_attn_kernel, lam_init=lam_init),
        grid_spec=grid_spec,
        out_shape=jax.ShapeDtypeStruct((t, SEC), BF16),
        compiler_params=pltpu.CompilerParams(
            dimension_semantics=("arbitrary", "arbitrary", "arbitrary"),
            vmem_limit_bytes=VMEM_LIMIT),
        name="diff_attn",
    )(use_fixed, qn, qn, kn, vt, lam_rows, g_subln, score_bound.reshape(1, 1))


def _row_span(ref, row, n):
    return ref.at[pl.ds(pl.multiple_of(row * ROW_CHUNKS, ROW_CHUNKS), n * ROW_CHUNKS), :]


def _chunk_rows(row0, n, c):
    return pl.ds(row0 * ROW_CHUNKS + c, n, stride=ROW_CHUNKS)


def _load_rows(ref, row0, n):
    return jnp.concatenate([ref[_chunk_rows(row0, n, c), :] for c in range(ROW_CHUNKS)], axis=1)


def _store_rows(ref, n, val):
    for c in range(ROW_CHUNKS):
        ref[_chunk_rows(0, n, c), :] = val[:, c * LANES:(c + 1) * LANES]


def _copy_run(src_hbm, buf, sem, src_row, dst_row, n, max_n):
    piece = 1 << (max_n.bit_length() - 1)
    while piece >= 1:
        take = (n & piece) != 0

        @pl.when(take)
        def _(piece=piece, src_row=src_row, dst_row=dst_row):
            pltpu.make_async_copy(_row_span(src_hbm, src_row, piece), _row_span(buf, dst_row, piece),
                                  sem).start()

        step = jnp.where(take, piece, 0)
        src_row = src_row + step
        dst_row = dst_row + step
        piece //= 2


def _wait_rows(n, src_hbm, buf, sem, base):
    pltpu.make_async_copy(_row_span(src_hbm, 0, n), _row_span(buf, base, n), sem).wait()


def _post_kernel(x_ref, yd_ref, cb_ref, u_ref, up_ref, mq_ref, sig_ref, km_ref, vm_ref,
                 cw_ref, wb_ref, wo_ref, gf_ref, wr_ref, br_ref, tri_ref,
                 x1_out, rows_out, slot_out, gate_out, cbase_out, ntile_out, off_out, cnt_out, carry_sc,
                 *, seq):
    i = pl.program_id(0)
    tm = x_ref.shape[0]

    @pl.when(i == 0)
    def _():
        carry_sc[...] = jnp.zeros(carry_sc.shape, F32)

    u = u_ref[...]
    seq_start = (i * tm) % seq == 0
    up = jnp.where(seq_start, 0.0, up_ref[...])
    r = lax.broadcasted_iota(jnp.int32, u.shape, 0)
    u1 = jnp.where(r == 0, up[7:8, :], pltpu.roll(u, 1, 0))
    u2 = jnp.where(r == 0, up[6:7, :], jnp.where(r == 1, up[7:8, :], pltpu.roll(u, 2, 0)))
    y_conv = cb_ref[...].astype(F32) * (cw_ref[0:1, :] * u2 + cw_ref[1:2, :] * u1 + cw_ref[2:3, :] * u)

    def mem_scores(hd):
        sl = slice(hd * MEM_HEAD_DIM, (hd + 1) * MEM_HEAD_DIM)
        return jnp.dot(mq_ref[:, sl], km_ref[sl, :], preferred_element_type=F32)

    y_mem = []
    s_next = mem_scores(0)
    for hd in range(MEM_HEADS):
        sl = slice(hd * MEM_HEAD_DIM, (hd + 1) * MEM_HEAD_DIM)
        s = s_next
        if hd + 1 < MEM_HEADS:
            s_next = mem_scores(hd + 1)
        if hd == 0:
            merged = (sig_ref[:, 0:D_MODEL].astype(F32)
                      * jnp.dot(yd_ref[...], wb_ref[0], preferred_element_type=F32))
        elif hd == 1:
            merged += (sig_ref[:, D_MODEL:2 * D_MODEL].astype(F32)
                       * jnp.dot(y_conv.astype(BF16), wb_ref[1], preferred_element_type=F32))
        p = jnp.exp2(s - jnp.max(s, axis=-1, keepdims=True))
        o = jnp.dot(p.astype(BF16), vm_ref[:, sl], preferred_element_type=F32)
        y_mem.append(o / jnp.sum(p, axis=-1, keepdims=True))

    merged += (sig_ref[:, 2 * D_MODEL:3 * D_MODEL].astype(F32)
               * jnp.dot(jnp.concatenate(y_mem, axis=1).astype(BF16), wb_ref[2],
                         preferred_element_type=F32))
    x1 = x_ref[...] + jnp.dot(merged.astype(BF16), wo_ref[...], preferred_element_type=F32)
    x1_out[...] = x1

    h2 = _rms(x1) * gf_ref[...]

    w = wr_ref[...]
    w_hi = w.astype(BF16)
    w_lo = (w - w_hi.astype(F32)).astype(BF16)
    h_hi = h2.astype(BF16)
    h_lo = (h2 - h_hi.astype(F32)).astype(BF16)
    part = jnp.dot(h_hi, jnp.concatenate([w_hi, w_lo], axis=1), preferred_element_type=F32)
    by_token = (part[:, :N_EXPERTS] + part[:, N_EXPERTS:]
                + jnp.dot(h_lo, w_hi, preferred_element_type=F32))
    padded = jnp.concatenate([by_token, jnp.zeros((tm, LANES - N_EXPERTS), F32)], axis=1)
    logits = padded.T[:N_EXPERTS, :] + br_ref[...]
    eio = lax.broadcasted_iota(jnp.int32, logits.shape, 0)
    work = logits
    vals, hots = [], []
    for k in range(TOP_K):
        mk = jnp.max(work, axis=0, keepdims=True)
        ik = jnp.min(jnp.where(work == mk, eio, N_EXPERTS), axis=0, keepdims=True)
        hot = eio == ik
        work = jnp.where(hot, -jnp.inf, work)
        vals.append(mk)
        hots.append(hot)
    ex = [jnp.exp(v - vals[0]) for v in vals]
    den = ex[0] + ex[1] + ex[2] + ex[3]
    gates = [e / den for e in ex]

    assign = jnp.zeros(logits.shape, F32)
    for hot in hots:
        assign = jnp.where(hot, 1.0, assign)
    earlier = jnp.dot(assign.astype(BF16), tri_ref[...], preferred_element_type=F32)
    n_col = jnp.sum(assign, axis=1, keepdims=True)
    e_row = lax.broadcasted_iota(jnp.int32, (N_EXPERTS, LANES), 0)
    run = jnp.broadcast_to(n_col, (N_EXPERTS, LANES))
    shift = 1
    while shift < N_EXPERTS:
        run = run + jnp.where(e_row >= shift, pltpu.roll(run, shift, 0), 0.0)
        shift *= 2
    off_col = run[:, 0:1] - n_col
    slots = [jnp.sum(jnp.where(hot, earlier + off_col, 0.0), axis=0, keepdims=True).astype(jnp.int32)
             for hot in hots]
    for k in range(TOP_K):
        slot_out[k:k + 1, :] = slots[k]
        gate_out[k:k + 1, :] = gates[k]

    n_slots = TOP_K * tm
    jdx = lax.broadcasted_iota(jnp.int32, (n_slots, tm), 0)
    pick = jnp.where(jdx == slots[0], 1.0, jnp.where(jdx == slots[1], 1.0, jnp.where(
        jdx == slots[2], 1.0, jnp.where(jdx == slots[3], 1.0, 0.0)))).astype(BF16)
    rows = jnp.dot(pick, h_hi, preferred_element_type=F32)
    _store_rows(rows_out, n_slots, rows)

    cbase_out[...] = jnp.broadcast_to(carry_sc[...], cbase_out.shape).astype(jnp.int32)
    ntile_out[...] = jnp.broadcast_to(n_col, ntile_out.shape).astype(jnp.int32)
    off_out[...] = jnp.broadcast_to(off_col, off_out.shape).astype(jnp.int32)
    carry_sc[...] = carry_sc[...] + n_col
    cnt_out[...] = jnp.broadcast_to(carry_sc[...], cnt_out.shape).astype(jnp.int32)


def _post(x2d, y_diff, cb, u, mqn, sig, km, vm, conv_w, w_branch, w_out, g_ffn, w_rt, b_r, tri, seq):
    t = x2d.shape[0]
    tm = TM_POST
    n_tiles = t // tm
    row = lambda i: (i, 0)
    const = lambda i: (0, 0)
    table = jax.ShapeDtypeStruct((n_tiles * N_EXPERTS, LANES), jnp.int32)
    outs = [
        jax.ShapeDtypeStruct((t, D_MODEL), F32),
        jax.ShapeDtypeStruct((TOP_K * t * ROW_CHUNKS, LANES), F32),
        jax.ShapeDtypeStruct((TOP_K, t), jnp.int32),
        jax.ShapeDtypeStruct((TOP_K, t), F32),
        table, table, table,
        jax.ShapeDtypeStruct((N_EXPERTS, LANES), jnp.int32),
    ]
    return pl.pallas_call(
        functools.partial(_post_kernel, seq=seq),
        grid=(t // tm,),
        in_specs=[
            pl.BlockSpec((tm, D_MODEL), row),
            pl.BlockSpec((tm, SEC), row),
            pl.BlockSpec((tm, SEC), row),
            pl.BlockSpec((tm, SEC), row),
            pl.BlockSpec((8, SEC), lambda i: (jnp.maximum(i * (tm // 8) - 1, 0), 0)),
            pl.BlockSpec((tm, SEC), row),
            pl.BlockSpec((tm, 3 * D_MODEL), row),
            pl.BlockSpec((SEC, N_MEM), lambda i: ((i * tm) // seq, 0)),
            pl.BlockSpec((N_MEM, SEC), lambda i: ((i * tm) // seq, 0)),
            pl.BlockSpec((3, SEC), const),
            pl.BlockSpec((3, SEC, D_MODEL), lambda i: (0, 0, 0)),
            pl.BlockSpec((D_MODEL, D_MODEL), const),
            pl.BlockSpec((1, D_MODEL), const),
            pl.BlockSpec((D_MODEL, N_EXPERTS), const),
            pl.BlockSpec((N_EXPERTS, 1), const),
            pl.BlockSpec((tm, tm), const),
        ],
        out_specs=[
            pl.BlockSpec((tm, D_MODEL), row),
            pl.BlockSpec((TOP_K * tm * ROW_CHUNKS, LANES), row),
            pl.BlockSpec((TOP_K, tm), lambda i: (0, i)),
            pl.BlockSpec((TOP_K, tm), lambda i: (0, i)),
            pl.BlockSpec((N_EXPERTS, LANES), row),
            pl.BlockSpec((N_EXPERTS, LANES), row),
            pl.BlockSpec((N_EXPERTS, LANES), row),
            pl.BlockSpec((N_EXPERTS, LANES), const),
        ],
        out_shape=outs,
        scratch_shapes=[pltpu.VMEM((N_EXPERTS, 1), F32)],
        compiler_params=pltpu.CompilerParams(dimension_semantics=("arbitrary",),
                                             vmem_limit_bytes=VMEM_LIMIT),
        name="post",
    )(x2d, y_diff, cb, u, u, mqn, sig, km, vm, conv_w, w_branch, w_out, g_ffn, w_rt, b_r, tri)


def _experts_kernel(be_ref, t0_ref, na_ref, ps_ref, cnt_ref, cb_ref, nt_ref, of_ref,
                    rows_hbm, zeros_hbm, wgu_f32, bgu_ref, wd_f32, bd_ref,
                    ys_out, xbuf, sem, wgu_ref, wd_ref, *, n_tiles, slots_per_tile):
    b = pl.program_id(0)
    n_active = na_ref[0]
    slot = b % 2

    @pl.when(jnp.logical_and(b < n_active,
                             jnp.logical_or(b == 0, be_ref[b] != be_ref[jnp.maximum(b - 1, 0)])))
    def _():
        wgu_ref[...] = wgu_f32[0].astype(BF16)
        wd_ref[...] = wd_f32[0].astype(BF16)

    def issue(blk, buf_slot):
        e = be_ref[blk]
        r0 = blk * MOE_BLOCK - ps_ref[e]
        r1 = r0 + MOE_BLOCK
        base = buf_slot * MOE_BLOCK
        sm = sem.at[buf_slot]

        def cond(i):
            return jnp.logical_and(i < n_tiles, cb_ref[e * n_tiles + jnp.minimum(i, n_tiles - 1)] < r1)

        def body(i):
            c = cb_ref[e * n_tiles + i]
            lo = jnp.maximum(c, r0)
            hi = jnp.minimum(c + nt_ref[e * n_tiles + i], r1)
            src = i * slots_per_tile + of_ref[e * n_tiles + i] + (lo - c)
            _copy_run(rows_hbm, xbuf, sm, src, base + (lo - r0), jnp.maximum(hi - lo, 0),
                      min(TM_POST, MOE_BLOCK))
            return i + 1

        lax.while_loop(cond, body, t0_ref[blk])
        valid = jnp.clip(cnt_ref[e] - r0, 0, MOE_BLOCK)
        _copy_run(zeros_hbm, xbuf, sm, 0, base + valid, MOE_BLOCK - valid, MOE_BLOCK)

    @pl.when(b == 0)
    def _():
        issue(b, 0)

    @pl.when(b + 1 < n_active)
    def _():
        issue(b + 1, 1 - slot)

    def compute(n_rows):
        base = pl.multiple_of(slot * MOE_BLOCK, MOE_BLOCK)
        _wait_rows(MOE_BLOCK, rows_hbm, xbuf, sem.at[slot], base)
        x = _load_rows(xbuf, base, n_rows).astype(BF16)
        d_ff = wd_ref.shape[0]

        def gate_up(j):
            gs = slice(j * FF_CHUNK, (j + 1) * FF_CHUNK)
            us = slice(d_ff + j * FF_CHUNK, d_ff + (j + 1) * FF_CHUNK)
            return (jnp.dot(x, wgu_ref[:, gs], preferred_element_type=F32) + bgu_ref[0, :, gs],
                    jnp.dot(x, wgu_ref[:, us], preferred_element_type=F32) + bgu_ref[0, :, us])

        acts = []
        nxt = gate_up(0)
        for j in range(d_ff // FF_CHUNK):
            g, u = nxt
            if (j + 1) * FF_CHUNK < d_ff:
                nxt = gate_up(j + 1)
            g = jnp.minimum(g, SWIGLU_LIMIT)
            u = jnp.clip(u, -SWIGLU_LIMIT, SWIGLU_LIMIT)
            acts.append(((u + 1.0) * (g * (1.0 / (1.0 + jnp.exp(-SWIGLU_ALPHA * g))))).astype(BF16))
        act = jnp.concatenate(acts, axis=1)
        for n in range(D_MODEL // FF_CHUNK):
            cols = slice(n * FF_CHUNK, (n + 1) * FF_CHUNK)
            yn = jnp.dot(act, wd_ref[:, cols], preferred_element_type=F32) + bd_ref[0, :, cols]
            for c in range(FF_CHUNK // LANES):
                ys_out[_chunk_rows(0, n_rows, n * (FF_CHUNK // LANES) + c), :] = (
                    yn[:, c * LANES:(c + 1) * LANES])
        if n_rows < MOE_BLOCK:
            ys_out[n_rows * ROW_CHUNKS:, :] = jnp.zeros(((MOE_BLOCK - n_rows) * ROW_CHUNKS, LANES), F32)

    @pl.when(b < n_active)
    def _():
        compute(MOE_BLOCK)

    @pl.when(b >= n_active)
    def _():
        ys_out[...] = jnp.zeros(ys_out.shape, F32)


def _experts(block_expert, first_tile, n_active, pstart, counts, cbase_e, ntile_e, off_e,
             rows, zero_rows, wgu, bgu, wd, bd, n_tiles):
    n_blocks = block_expert.shape[0]
    ff2 = wgu.shape[2]
    by_expert = lambda b, be, *_: (be[b], 0, 0)
    grid_spec = pltpu.PrefetchScalarGridSpec(
        num_scalar_prefetch=8,
        grid=(n_blocks,),
        in_specs=[
            pl.BlockSpec(memory_space=pl.ANY),
            pl.BlockSpec(memory_space=pl.ANY),
            pl.BlockSpec((1, D_MODEL, ff2), by_expert),
            pl.BlockSpec((1, 1, ff2), by_expert),
            pl.BlockSpec((1, ff2 // 2, D_MODEL), by_expert),
            pl.BlockSpec((1, 1, D_MODEL), by_expert),
        ],
        out_specs=pl.BlockSpec((MOE_BLOCK * ROW_CHUNKS, LANES), lambda b, *_: (b, 0)),
        scratch_shapes=[
            pltpu.VMEM((2 * MOE_BLOCK * ROW_CHUNKS, LANES), F32),
            pltpu.SemaphoreType.DMA((2,)),
            pltpu.VMEM((D_MODEL, ff2), BF16),
            pltpu.VMEM((ff2 // 2, D_MODEL), BF16),
        ],
    )
    return pl.pallas_call(
        functools.partial(_experts_kernel, n_tiles=n_tiles, slots_per_tile=TOP_K * TM_POST),
        grid_spec=grid_spec,
        out_shape=jax.ShapeDtypeStruct((n_blocks * MOE_BLOCK * ROW_CHUNKS, LANES), F32),
        compiler_params=pltpu.CompilerParams(dimension_semantics=("arbitrary",),
                                             vmem_limit_bytes=VMEM_LIMIT_EXPERTS),
        name="experts",
    )(block_expert, first_tile, n_active, pstart, counts, cbase_e, ntile_e, off_e,
      rows, zero_rows, wgu, bgu, wd, bd)


def _combine_kernel(ps_ref, cb_ref, nt_ref, of_ref, slot_ref, gate_ref, x1_ref, ys_hbm, o_ref, buf,
                    sem):
    i = pl.program_id(0)
    n = pl.num_programs(0)
    tm = x1_ref.shape[0]
    n_slots = TOP_K * tm
    cur = i % 2

    def issue(tile, buf_slot):
        def body(e, carry):
            k = tile * N_EXPERTS + e
            _copy_run(ys_hbm, buf, sem.at[buf_slot], ps_ref[e] + cb_ref[k],
                      buf_slot * n_slots + of_ref[k], nt_ref[k], tm)
            return carry
        lax.fori_loop(0, N_EXPERTS, body, 0)

    @pl.when(i == 0)
    def _():
        issue(i, 0)

    @pl.when(i + 1 < n)
    def _():
        issue(i + 1, 1 - cur)

    base = pl.multiple_of(cur * n_slots, n_slots)
    _wait_rows(n_slots, ys_hbm, buf, sem.at[cur], base)
    y = _load_rows(buf, base, n_slots).astype(BF16)
    jdx = lax.broadcasted_iota(jnp.int32, (n_slots, tm), 0)
    pick = jnp.where(jdx == slot_ref[0:1, :], gate_ref[0:1, :], jnp.where(
        jdx == slot_ref[1:2, :], gate_ref[1:2, :], jnp.where(
            jdx == slot_ref[2:3, :], gate_ref[2:3, :], jnp.where(
                jdx == slot_ref[3:4, :], gate_ref[3:4, :], 0.0)))).astype(BF16)
    o_ref[...] = x1_ref[...] + lax.dot_general(pick, y, (((0,), (0,)), ((), ())),
                                               preferred_element_type=F32)


def _combine(pstart, cbase_t, ntile_t, off_t, slots, gates, x1, ys_rows):
    t = x1.shape[0]
    tm = TM_POST
    n_slots = TOP_K * tm
    grid_spec = pltpu.PrefetchScalarGridSpec(
        num_scalar_prefetch=4,
        grid=(t // tm,),
        in_specs=[
            pl.BlockSpec((TOP_K, tm), lambda i, *_: (0, i)),
            pl.BlockSpec((TOP_K, tm), lambda i, *_: (0, i)),
            pl.BlockSpec((tm, D_MODEL), lambda i, *_: (i, 0)),
            pl.BlockSpec(memory_space=pl.ANY),
        ],
        out_specs=pl.BlockSpec((tm, D_MODEL), lambda i, *_: (i, 0)),
        scratch_shapes=[
            pltpu.VMEM((2 * n_slots * ROW_CHUNKS, LANES), F32),
            pltpu.SemaphoreType.DMA((2,)),
        ],
    )
    return pl.pallas_call(
        _combine_kernel,
        grid_spec=grid_spec,
        out_shape=jax.ShapeDtypeStruct((t, D_MODEL), F32),
        compiler_params=pltpu.CompilerParams(dimension_semantics=("arbitrary",),
                                             vmem_limit_bytes=VMEM_LIMIT),
        name="combine",
    )(pstart, cbase_t, ntile_t, off_t, slots, gates, x1, ys_rows)


def kernel(x, mem, g_mix, w_in, g_q_diff, g_k_diff, lambda_q1, lambda_k1, lambda_q2, lambda_k2,
           g_subln, conv_w, g_mem, w_mem_kv, g_q_mem, g_k_mem, w_branch, w_out, g_ffn, w_router,
           b_router, w_gate_up, b_gate_up, w_down, b_down):
    batch, seq, d = x.shape
    t = batch * seq
    depth = g_mix.shape[0]
    assert depth == 1 and d == D_MODEL and seq % TQ == 0 and t % TM_PROJ == 0

    inv_freq = 1.0 / (ROPE_THETA ** (jnp.arange(0, DIFF_HEAD_DIM, 2, dtype=F32) / DIFF_HEAD_DIM))
    freq = jnp.tile(inv_freq, 4)[None, :]
    half = jnp.ones((DIFF_HEAD_DIM // 2,), F32)
    sign = jnp.tile(jnp.concatenate([-half, half]), 2)[None, :]
    ang_r = jnp.arange(TM_PROJ, dtype=F32)[:, None] * freq
    cos_r, sin_r = jnp.cos(ang_r), jnp.sin(ang_r)
    grp = jnp.arange(SEC) // DIFF_HEAD_DIM
    bd = jnp.where(grp[:, None] == grp[None, :], 1.0 / DIFF_HEAD_DIM, 0.0).astype(BF16)
    tok = jnp.arange(TM_POST)
    tri = (tok[:, None] < tok[None, :]).astype(BF16)

    x2d = x.reshape(t, d)
    l = 0
    lam_init = 0.8 - 0.6 * math.exp(-0.3 * l)
    km, vm = _mem_kv(mem.reshape(batch * N_MEM, d), g_mem[l][None, :], w_mem_kv[l].astype(BF16),
                     g_k_mem[l][None, :])
    qn, kn, vt, cb, u, mqn, sig = _in_proj(
        x2d, g_mix[l][None, :], w_in[l].astype(BF16), freq, sign, cos_r, sin_r, bd,
        jnp.tile(g_q_diff[l], SEC // DIFF_HEAD_DIM)[None, :],
        jnp.tile(g_k_diff[l], SEC // DIFF_HEAD_DIM)[None, :],
        g_q_mem[l][None, :], seq)
    lam_rows = jnp.stack([lambda_q1[l], lambda_k1[l], lambda_q2[l], lambda_k2[l]]).astype(F32)
    score_bound = (DIFF_HEAD_DIM * (DIFF_HEAD_DIM ** -0.5 * LOG2E)
                   * jnp.max(jnp.abs(g_q_diff[l])) * jnp.max(jnp.abs(g_k_diff[l]))).astype(F32)
    y_diff = _diff_attn(qn, kn, vt, lam_rows, g_subln[l][None, :], score_bound, batch, seq, lam_init)
    x1, rows, slots, gates, cbase, ntile, off, counts = _post(
        x2d, y_diff, cb, u, mqn, sig, km, vm, conv_w[l], w_branch[l].astype(BF16),
        w_out[l].astype(BF16), g_ffn[l][None, :], w_router[l], b_router[l][:, None], tri, seq)

    n_tiles = t // TM_POST
    n_assign = t * TOP_K
    n_rows = -(-(n_assign + N_EXPERTS * (MOE_BLOCK - 1)) // MOE_BLOCK) * MOE_BLOCK
    n_blocks = n_rows // MOE_BLOCK
    cnt = counts[:, 0]
    padded = (cnt + MOE_BLOCK - 1) // MOE_BLOCK * MOE_BLOCK
    pend = jnp.cumsum(padded)
    pstart = pend - padded
    block_row0 = jnp.arange(n_blocks, dtype=jnp.int32) * MOE_BLOCK
    block_expert = jnp.minimum(
        jnp.sum((pend[None, :] <= block_row0[:, None]).astype(jnp.int32), axis=1), N_EXPERTS - 1)
    n_active = (pend[-1:] // MOE_BLOCK).astype(jnp.int32)
    cbase_t = cbase[:, 0].reshape(n_tiles, N_EXPERTS)
    ntile_t = ntile[:, 0].reshape(n_tiles, N_EXPERTS)
    off_t = off[:, 0].reshape(n_tiles, N_EXPERTS)
    cend_b = (cbase_t + ntile_t).T[block_expert]
    r0 = block_row0 - pstart[block_expert]
    first_tile = jnp.sum((cend_b <= r0[:, None]).astype(jnp.int32), axis=1)
    first_tile = jnp.minimum(first_tile, n_tiles - 1).astype(jnp.int32)

    zero_rows = jnp.zeros((MOE_BLOCK * ROW_CHUNKS, LANES), F32)
    ys_rows = _experts(block_expert, first_tile, n_active, pstart, cnt,
                       cbase_t.T.reshape(-1), ntile_t.T.reshape(-1), off_t.T.reshape(-1),
                       rows, zero_rows, w_gate_up[l], b_gate_up[l][:, None, :],
                       w_down[l], b_down[l][:, None, :], n_tiles)
    out = _combine(pstart, cbase_t.reshape(-1), ntile_t.reshape(-1), off_t.reshape(-1), slots, gates,
                   x1, ys_rows)
    return out.reshape(batch, seq, d)
```

```python
import functools
import math

import jax
import jax.numpy as jnp
from jax import lax
from jax.experimental import pallas as pl
from jax.experimental.pallas import tpu as pltpu

F32 = jnp.float32
BF16 = jnp.bfloat16

D_MODEL = 1024
CHUNK = 64
EPS = 1e-6
ROPE_THETA = 10000.0
DIFF_HEADS = 4
DIFF_HEAD_DIM = 64
DIFF_V_DIM = 2 * DIFF_HEAD_DIM
N_MEM = 256
MEM_HEADS = 4
MEM_HEAD_DIM = 128
SEC = 512
N_SEC = 13
N_EXPERTS = 32
TOP_K = 4
SWIGLU_LIMIT = 7.0
SWIGLU_ALPHA = 1.702
MOE_BLOCK = 512
LANES = 128
ROW_CHUNKS = D_MODEL // LANES
LOG2E = 1.4426950408889634
NEG_BIG = -1e30

TM_PROJ = 512
TQ = 1024
ATTN_STRIP = 512
MAX_FIXED_SHIFT = 40.0
TM_POST = 256
FF_CHUNK = 256
VMEM_LIMIT = 48 * 1024 * 1024
VMEM_LIMIT_EXPERTS = 56 * 1024 * 1024


def _rms(x, eps=EPS):
    return x * lax.rsqrt(jnp.mean(x * x, axis=-1, keepdims=True) + eps)


def _mem_kv_kernel(mem_ref, g_ref, w_ref, gk_ref, k_out, v_out):
    h = (_rms(mem_ref[...]) * g_ref[...]).astype(BF16)
    kv = jnp.dot(h, w_ref[...], preferred_element_type=F32)
    for hd in range(MEM_HEADS):
        sl = slice(hd * MEM_HEAD_DIM, (hd + 1) * MEM_HEAD_DIM)
        k_out[sl, :] = (_rms(kv[:, sl]) * gk_ref[...]).T.astype(BF16)
    v_out[...] = kv[:, SEC:].astype(BF16)


def _mem_kv(mem2d, g_mem, w_kv, g_k):
    rows = mem2d.shape[0]
    return pl.pallas_call(
        _mem_kv_kernel,
        grid=(rows // N_MEM,),
        in_specs=[
            pl.BlockSpec((N_MEM, D_MODEL), lambda i: (i, 0)),
            pl.BlockSpec((1, D_MODEL), lambda i: (0, 0)),
            pl.BlockSpec((D_MODEL, 2 * SEC), lambda i: (0, 0)),
            pl.BlockSpec((1, MEM_HEAD_DIM), lambda i: (0, 0)),
        ],
        out_specs=[pl.BlockSpec((SEC, N_MEM), lambda i: (i, 0)),
                   pl.BlockSpec((N_MEM, SEC), lambda i: (i, 0))],
        out_shape=[jax.ShapeDtypeStruct((rows // N_MEM * SEC, N_MEM), BF16),
                   jax.ShapeDtypeStruct((rows, SEC), BF16)],
        compiler_params=pltpu.CompilerParams(vmem_limit_bytes=VMEM_LIMIT),
        name="mem_kv",
    )(mem2d, g_mem, w_kv, g_k)


def _in_proj_kernel(x_ref, g_ref, w_ref, freq_ref, sign_ref, cosr_ref, sinr_ref, bd_ref, gq_ref, gk_ref,
                    gqm_ref, q_out, k_out, v_out, cb_out, u_out, mq_out, sig_out, *, tiles_per_seq):
    tm = x_ref.shape[0]
    h = (_rms(x_ref[...]) * g_ref[...]).astype(BF16)

    def proj(sec):
        return jnp.dot(h, w_ref[:, sec * SEC:(sec + 1) * SEC], preferred_element_type=F32)

    base = ((pl.program_id(0) % tiles_per_seq) * tm).astype(F32) * freq_ref[...]
    cos_b, sin_b = jnp.cos(base), jnp.sin(base)
    cos128 = cos_b * cosr_ref[...] - sin_b * sinr_ref[...]
    sin128 = (sin_b * cosr_ref[...] + cos_b * sinr_ref[...]) * sign_ref[...]
    cos = jnp.concatenate([cos128] * (SEC // LANES), axis=1)
    sin = jnp.concatenate([sin128] * (SEC // LANES), axis=1)
    lane = lax.broadcasted_iota(jnp.int32, (tm, SEC), 1)
    first_half = (lane & (DIFF_HEAD_DIM // 2)) == 0

    def norm_rope(a, g, scale):
        ms = jnp.dot((a * a).astype(BF16), bd_ref[...], preferred_element_type=F32)
        y = a * lax.rsqrt(ms + EPS) * g
        partner = jnp.where(first_half,
                            pltpu.roll(y, SEC - DIFF_HEAD_DIM // 2, 1),
                            pltpu.roll(y, DIFF_HEAD_DIM // 2, 1))
        return (y * cos + partner * sin) * scale

    q_out[...] = norm_rope(proj(0), gq_ref[...], DIFF_HEAD_DIM ** -0.5 * LOG2E).astype(BF16)
    k_out[...] = norm_rope(proj(1), gk_ref[...], 1.0).astype(BF16)
    v = proj(2)
    for hd in range(DIFF_HEADS):
        v_out[0, hd, 0] = v[:, hd * DIFF_V_DIM:(hd + 1) * DIFF_V_DIM].T.astype(BF16)
    cb_out[...] = proj(3).astype(BF16)
    u_out[...] = proj(4) * proj(5)
    mq = proj(6)
    for hd in range(MEM_HEADS):
        sl = slice(hd * MEM_HEAD_DIM, (hd + 1) * MEM_HEAD_DIM)
        mq_out[:, sl] = (_rms(mq[:, sl]) * gqm_ref[...]
                         * (MEM_HEAD_DIM ** -0.5 * LOG2E)).astype(BF16)
    for s in range(7, N_SEC):
        a = proj(s)
        sig_out[:, (s - 7) * SEC:(s - 6) * SEC] = (1.0 / (1.0 + jnp.exp(-a))).astype(BF16)


def _in_proj(x2d, g_mix, w_in, freq, sign, cos_r, sin_r, bd, gq, gk, gqm, seq):
    t = x2d.shape[0]
    tm = TM_PROJ
    tiles_per_seq = seq // tm
    tiles_per_key = TQ // tm
    row = lambda i: (i, 0)
    const = lambda i: (0, 0)
    outs = [
        jax.ShapeDtypeStruct((t, SEC), BF16),
        jax.ShapeDtypeStruct((t, SEC), BF16),
        jax.ShapeDtypeStruct((t // seq, DIFF_HEADS, seq // TQ, DIFF_V_DIM, TQ), BF16),
        jax.ShapeDtypeStruct((t, SEC), BF16),
        jax.ShapeDtypeStruct((t, SEC), F32),
        jax.ShapeDtypeStruct((t, SEC), BF16),
        jax.ShapeDtypeStruct((t, 3 * D_MODEL), BF16),
    ]
    return pl.pallas_call(
        functools.partial(_in_proj_kernel, tiles_per_seq=tiles_per_seq),
        grid=(t // tm,),
        in_specs=[
            pl.BlockSpec((tm, D_MODEL), row),
            pl.BlockSpec((1, D_MODEL), const),
            pl.BlockSpec((D_MODEL, N_SEC * SEC), const),
            pl.BlockSpec((1, LANES), const),
            pl.BlockSpec((1, LANES), const),
            pl.BlockSpec((tm, LANES), const),
            pl.BlockSpec((tm, LANES), const),
            pl.BlockSpec((SEC, SEC), const),
            pl.BlockSpec((1, SEC), const),
            pl.BlockSpec((1, SEC), const),
            pl.BlockSpec((1, MEM_HEAD_DIM), const),
        ],
        out_specs=[
            pl.BlockSpec((1, DIFF_HEADS, 1, DIFF_V_DIM, tm),
                         lambda i: (i // tiles_per_seq, 0, (i % tiles_per_seq) // tiles_per_key,
                                    0, i % tiles_per_key))
            if o.ndim == 5 else pl.BlockSpec((tm, o.shape[1]), row) for o in outs],
        out_shape=outs,
        compiler_params=pltpu.CompilerParams(vmem_limit_bytes=VMEM_LIMIT),
        name="in_proj",
    )(x2d, g_mix, w_in, freq, sign, cos_r, sin_r, bd, gq, gk, gqm)


def _diff_attn_kernel(fixed_ref, q_ref, qn_ref, k_ref, vt_ref, lam_ref, gs_ref, bound_ref, o_ref,
                      acc_sc, s0_sc, l_sc, *, lam_init):
    i = pl.program_id(2)
    tq = q_ref.shape[0]
    def stacked_queries(ref):
        qt = ref[...].astype(F32).T
        dim = lax.broadcasted_iota(jnp.int32, qt.shape, 0)
        return jnp.concatenate([jnp.where(dim < DIFF_HEAD_DIM, qt, 0.0),
                                jnp.where(dim >= DIFF_HEAD_DIM, qt, 0.0)], axis=1).astype(BF16)

    qq = stacked_queries(q_ref)
    acc_sc[...] = jnp.zeros(acc_sc.shape, F32)
    n_strips = 2 * tq // ATTN_STRIP

    @pl.when(i == 0)
    def _():
        s0_sc[...] = jnp.dot(k_ref[0:tq, :], qq[:, 0:ATTN_STRIP], preferred_element_type=F32)

    def scores(j, c, nk):
        off = pl.multiple_of(j * tq, tq)
        return jnp.dot(k_ref[pl.ds(off, nk), :], qq[:, c * ATTN_STRIP:(c + 1) * ATTN_STRIP],
                       preferred_element_type=F32)

    def step(j, ms, ls, masked, fixed):
        q_offs = [(c * ATTN_STRIP) % tq for c in range(n_strips)]
        nks = [min(tq, qo + ATTN_STRIP) if masked else tq for qo in q_offs]
        ms_new, ls_new = [], []
        s = s0_sc[0:nks[0], :]
        for c in range(n_strips):
            cols = slice(c * ATTN_STRIP, (c + 1) * ATTN_STRIP)
            if c + 1 < n_strips:
                s_next = scores(j, c + 1, nks[c + 1])
            elif not masked:
                s0_sc[...] = scores(j + 1, 0, tq)
            if masked:
                r = lax.broadcasted_iota(jnp.int32, s.shape, 0)
                col = lax.broadcasted_iota(jnp.int32, s.shape, 1)
                s = jnp.where((r // CHUNK) <= ((col + q_offs[c]) // CHUNK), s, NEG_BIG)
            if fixed:
                p = jnp.exp2(s - bound_ref[...])
                ls_new.append(ls[c] + jnp.sum(p, axis=0, keepdims=True))
                ms_new.append(ms[c])
                acc_sc[:, cols] = acc_sc[:, cols] + jnp.dot(
                    vt_ref[0, 0, j][:, :nks[c]], p.astype(BF16), preferred_element_type=F32)
            else:
                m_new = jnp.maximum(ms[c], jnp.max(s, axis=0, keepdims=True))
                p = jnp.exp2(s - m_new)
                alpha = jnp.exp2(ms[c] - m_new)
                ls_new.append(alpha * ls[c] + jnp.sum(p, axis=0, keepdims=True))
                ms_new.append(m_new)
                acc_sc[:, cols] = alpha * acc_sc[:, cols] + jnp.dot(
                    vt_ref[0, 0, j][:, :nks[c]], p.astype(BF16), preferred_element_type=F32)
            if c + 1 < n_strips:
                s = s_next
        return tuple(ms_new), tuple(ls_new)

    def run(fixed):
        m0 = tuple(jnp.full((1, ATTN_STRIP), NEG_BIG, F32) for _ in range(n_strips))
        l0 = tuple(jnp.zeros((1, ATTN_STRIP), F32) for _ in range(n_strips))
        m, l = lax.fori_loop(0, i, lambda j, c: step(j, c[0], c[1], False, fixed), (m0, l0))
        m, l = step(i, m, l, True, fixed)
        l_sc[...] = jnp.concatenate(l, axis=1)

    @pl.when(fixed_ref[0] == 1)
    def _():
        run(True)

    @pl.when(fixed_ref[0] != 1)
    def _():
        run(False)

    qt_next = qn_ref[0:ATTN_STRIP, :].astype(F32).T
    dim_next = lax.broadcasted_iota(jnp.int32, qt_next.shape, 0)
    s0_sc[...] = jnp.dot(k_ref[0:tq, :], jnp.where(dim_next < DIFF_HEAD_DIM, qt_next, 0.0).astype(BF16),
                         preferred_element_type=F32)

    ot = acc_sc[...] / l_sc[...]
    lam = (jnp.exp(jnp.sum(lam_ref[0:1, :] * lam_ref[1:2, :], axis=-1, keepdims=True))
           - jnp.exp(jnp.sum(lam_ref[2:3, :] * lam_ref[3:4, :], axis=-1, keepdims=True))
           + lam_init)
    d = (ot[:, :tq] - lam * ot[:, tq:]).T
    o_ref[...] = (_rms(d) * gs_ref[...] * (1.0 - lam_init)).astype(BF16)


def _diff_attn(qn, kn, vt, lam_rows, g_subln, score_bound, batch, seq, lam_init):
    t = qn.shape[0]
    nq = seq // TQ
    use_fixed = (score_bound <= MAX_FIXED_SHIFT).astype(jnp.int32).reshape(1)
    grid_spec = pltpu.PrefetchScalarGridSpec(
        num_scalar_prefetch=1,
        grid=(batch, DIFF_HEADS, nq),
        in_specs=[
            pl.BlockSpec((TQ, DIFF_V_DIM), lambda b, h, i, *_: (b * nq + i, h)),
            pl.BlockSpec((TQ, DIFF_V_DIM), lambda b, h, i, *_: (b * nq + jnp.minimum(i + 1, nq - 1), h)),
            pl.BlockSpec((seq, DIFF_V_DIM), lambda b, h, i, *_: (b, h)),
            pl.BlockSpec((1, 1, nq, DIFF_V_DIM, TQ), lambda b, h, i, *_: (b, h, 0, 0, 0)),
            pl.BlockSpec((4, DIFF_HEAD_DIM), lambda b, h, i, *_: (0, 0)),
            pl.BlockSpec((1, DIFF_V_DIM), lambda b, h, i, *_: (0, 0)),
            pl.BlockSpec((1, 1), lambda b, h, i, *_: (0, 0)),
        ],
        out_specs=pl.BlockSpec((TQ, DIFF_V_DIM), lambda b, h, i, *_: (b * nq + i, h)),
        scratch_shapes=[pltpu.VMEM((DIFF_V_DIM, 2 * TQ), F32), pltpu.VMEM((TQ, ATTN_STRIP), F32),
                        pltpu.VMEM((1, 2 * TQ), F32)],
    )
    return pl.pallas_call(
        functools.partial(_diff_attn_kernel, lam_init=lam_init),
        grid_spec=grid_spec,
        out_shape=jax.ShapeDtypeStruct((t, SEC), BF16),
        compiler_params=pltpu.CompilerParams(
            dimension_semantics=("arbitrary", "arbitrary", "arbitrary"),
            vmem_limit_bytes=VMEM_LIMIT),
        name="diff_attn",
    )(use_fixed, qn, qn, kn, vt, lam_rows, g_subln, score_bound.reshape(1, 1))


def _row_span(ref, row, n):
    return ref.at[pl.ds(pl.multiple_of(row * ROW_CHUNKS, ROW_CHUNKS), n * ROW_CHUNKS), :]


def _chunk_rows(row0, n, c):
    return pl.ds(row0 * ROW_CHUNKS + c, n, stride=ROW_CHUNKS)


def _load_rows(ref, row0, n):
    return jnp.concatenate([ref[_chunk_rows(row0, n, c), :] for c in range(ROW_CHUNKS)], axis=1)


def _store_rows(ref, n, val):
    for c in range(ROW_CHUNKS):
        ref[_chunk_rows(0, n, c), :] = val[:, c * LANES:(c + 1) * LANES]


def _copy_run(src_hbm, buf, sem, src_row, dst_row, n, max_n):
    piece = 1 << (max_n.bit_length() - 1)
    while piece >= 1:
        take = (n & piece) != 0

        @pl.when(take)
        def _(piece=piece, src_row=src_row, dst_row=dst_row):
            pltpu.make_async_copy(_row_span(src_hbm, src_row, piece), _row_span(buf, dst_row, piece),
                                  sem).start()

        step = jnp.where(take, piece, 0)
        src_row = src_row + step
        dst_row = dst_row + step
        piece //= 2


def _wait_rows(n, src_hbm, buf, sem, base):
    pltpu.make_async_copy(_row_span(src_hbm, 0, n), _row_span(buf, base, n), sem).wait()


def _post_kernel(x_ref, yd_ref, cb_ref, u_ref, up_ref, mq_ref, sig_ref, km_ref, vm_ref,
                 cw_ref, wb_ref, wo_ref, gf_ref, wr_ref, br_ref, tri_ref,
                 x1_out, rows_out, slot_out, gate_out, cbase_out, ntile_out, off_out, cnt_out, carry_sc,
                 *, seq):
    i = pl.program_id(0)
    tm = x_ref.shape[0]

    @pl.when(i == 0)
    def _():
        carry_sc[...] = jnp.zeros(carry_sc.shape, F32)

    u = u_ref[...]
    seq_start = (i * tm) % seq == 0
    up = jnp.where(seq_start, 0.0, up_ref[...])
    r = lax.broadcasted_iota(jnp.int32, u.shape, 0)
    u1 = jnp.where(r == 0, up[7:8, :], pltpu.roll(u, 1, 0))
    u2 = jnp.where(r == 0, up[6:7, :], jnp.where(r == 1, up[7:8, :], pltpu.roll(u, 2, 0)))
    y_conv = cb_ref[...].astype(F32) * (cw_ref[0:1, :] * u2 + cw_ref[1:2, :] * u1 + cw_ref[2:3, :] * u)

    def mem_scores(hd):
        sl = slice(hd * MEM_HEAD_DIM, (hd + 1) * MEM_HEAD_DIM)
        return jnp.dot(mq_ref[:, sl], km_ref[sl, :], preferred_element_type=F32)

    y_mem = []
    s_next = mem_scores(0)
    for hd in range(MEM_HEADS):
        sl = slice(hd * MEM_HEAD_DIM, (hd + 1) * MEM_HEAD_DIM)
        s = s_next
        if hd + 1 < MEM_HEADS:
            s_next = mem_scores(hd + 1)
        if hd == 0:
            merged = (sig_ref[:, 0:D_MODEL].astype(F32)
                      * jnp.dot(yd_ref[...], wb_ref[0], preferred_element_type=F32))
        elif hd == 1:
            merged += (sig_ref[:, D_MODEL:2 * D_MODEL].astype(F32)
                       * jnp.dot(y_conv.astype(BF16), wb_ref[1], preferred_element_type=F32))
        p = jnp.exp2(s - jnp.max(s, axis=-1, keepdims=True))
        o = jnp.dot(p.astype(BF16), vm_ref[:, sl], preferred_element_type=F32)
        y_mem.append(o / jnp.sum(p, axis=-1, keepdims=True))

    merged += (sig_ref[:, 2 * D_MODEL:3 * D_MODEL].astype(F32)
               * jnp.dot(jnp.concatenate(y_mem, axis=1).astype(BF16), wb_ref[2],
                         preferred_element_type=F32))
    x1 = x_ref[...] + jnp.dot(merged.astype(BF16), wo_ref[...], preferred_element_type=F32)
    x1_out[...] = x1

    h2 = _rms(x1) * gf_ref[...]

    w = wr_ref[...]
    w_hi = w.astype(BF16)
    w_lo = (w - w_hi.astype(F32)).astype(BF16)
    h_hi = h2.astype(BF16)
    h_lo = (h2 - h_hi.astype(F32)).astype(BF16)
    part = jnp.dot(h_hi, jnp.concatenate([w_hi, w_lo], axis=1), preferred_element_type=F32)
    by_token = (part[:, :N_EXPERTS] + part[:, N_EXPERTS:]
                + jnp.dot(h_lo, w_hi, preferred_element_type=F32))
    padded = jnp.concatenate([by_token, jnp.zeros((tm, LANES - N_EXPERTS), F32)], axis=1)
    logits = padded.T[:N_EXPERTS, :] + br_ref[...]
    eio = lax.broadcasted_iota(jnp.int32, logits.shape, 0)
    work = logits
    vals, hots = [], []
    for k in range(TOP_K):
        mk = jnp.max(work, axis=0, keepdims=True)
        ik = jnp.min(jnp.where(work == mk, eio, N_EXPERTS), axis=0, keepdims=True)
        hot = eio == ik
        work = jnp.where(hot, -jnp.inf, work)
        vals.append(mk)
        hots.append(hot)
    ex = [jnp.exp(v - vals[0]) for v in vals]
    den = ex[0] + ex[1] + ex[2] + ex[3]
    gates = [e / den for e in ex]

    assign = jnp.zeros(logits.shape, F32)
    for hot in hots:
        assign = jnp.where(hot, 1.0, assign)
    earlier = jnp.dot(assign.astype(BF16), tri_ref[...], preferred_element_type=F32)
    n_col = jnp.sum(assign, axis=1, keepdims=True)
    e_row = lax.broadcasted_iota(jnp.int32, (N_EXPERTS, LANES), 0)
    run = jnp.broadcast_to(n_col, (N_EXPERTS, LANES))
    shift = 1
    while shift < N_EXPERTS:
        run = run + jnp.where(e_row >= shift, pltpu.roll(run, shift, 0), 0.0)
        shift *= 2
    off_col = run[:, 0:1] - n_col
    slots = [jnp.sum(jnp.where(hot, earlier + off_col, 0.0), axis=0, keepdims=True).astype(jnp.int32)
             for hot in hots]
    for k in range(TOP_K):
        slot_out[k:k + 1, :] = slots[k]
        gate_out[k:k + 1, :] = gates[k]

    n_slots = TOP_K * tm
    jdx = lax.broadcasted_iota(jnp.int32, (n_slots, tm), 0)
    pick = jnp.where(jdx == slots[0], 1.0, jnp.where(jdx == slots[1], 1.0, jnp.where(
        jdx == slots[2], 1.0, jnp.where(jdx == slots[3], 1.0, 0.0)))).astype(BF16)
    rows = jnp.dot(pick, h_hi, preferred_element_type=F32)
    _store_rows(rows_out, n_slots, rows)

    cbase_out[...] = jnp.broadcast_to(carry_sc[...], cbase_out.shape).astype(jnp.int32)
    ntile_out[...] = jnp.broadcast_to(n_col, ntile_out.shape).astype(jnp.int32)
    off_out[...] = jnp.broadcast_to(off_col, off_out.shape).astype(jnp.int32)
    carry_sc[...] = carry_sc[...] + n_col
    cnt_out[...] = jnp.broadcast_to(carry_sc[...], cnt_out.shape).astype(jnp.int32)


def _post(x2d, y_diff, cb, u, mqn, sig, km, vm, conv_w, w_branch, w_out, g_ffn, w_rt, b_r, tri, seq):
    t = x2d.shape[0]
    tm = TM_POST
    n_tiles = t // tm
    row = lambda i: (i, 0)
    const = lambda i: (0, 0)
    table = jax.ShapeDtypeStruct((n_tiles * N_EXPERTS, LANES), jnp.int32)
    outs = [
        jax.ShapeDtypeStruct((t, D_MODEL), F32),
        jax.ShapeDtypeStruct((TOP_K * t * ROW_CHUNKS, LANES), F32),
        jax.ShapeDtypeStruct((TOP_K, t), jnp.int32),
        jax.ShapeDtypeStruct((TOP_K, t), F32),
        table, table, table,
        jax.ShapeDtypeStruct((N_EXPERTS, LANES), jnp.int32),
    ]
    return pl.pallas_call(
        functools.partial(_post_kernel, seq=seq),
        grid=(t // tm,),
        in_specs=[
            pl.BlockSpec((tm, D_MODEL), row),
            pl.BlockSpec((tm, SEC), row),
            pl.BlockSpec((tm, SEC), row),
            pl.BlockSpec((tm, SEC), row),
            pl.BlockSpec((8, SEC), lambda i: (jnp.maximum(i * (tm // 8) - 1, 0), 0)),
            pl.BlockSpec((tm, SEC), row),
            pl.BlockSpec((tm, 3 * D_MODEL), row),
            pl.BlockSpec((SEC, N_MEM), lambda i: ((i * tm) // seq, 0)),
            pl.BlockSpec((N_MEM, SEC), lambda i: ((i * tm) // seq, 0)),
            pl.BlockSpec((3, SEC), const),
            pl.BlockSpec((3, SEC, D_MODEL), lambda i: (0, 0, 0)),
            pl.BlockSpec((D_MODEL, D_MODEL), const),
            pl.BlockSpec((1, D_MODEL), const),
            pl.BlockSpec((D_MODEL, N_EXPERTS), const),
            pl.BlockSpec((N_EXPERTS, 1), const),
            pl.BlockSpec((tm, tm), const),
        ],
        out_specs=[
            pl.BlockSpec((tm, D_MODEL), row),
            pl.BlockSpec((TOP_K * tm * ROW_CHUNKS, LANES), row),
            pl.BlockSpec((TOP_K, tm), lambda i: (0, i)),
            pl.BlockSpec((TOP_K, tm), lambda i: (0, i)),
            pl.BlockSpec((N_EXPERTS, LANES), row),
            pl.BlockSpec((N_EXPERTS, LANES), row),
            pl.BlockSpec((N_EXPERTS, LANES), row),
            pl.BlockSpec((N_EXPERTS, LANES), const),
        ],
        out_shape=outs,
        scratch_shapes=[pltpu.VMEM((N_EXPERTS, 1), F32)],
        compiler_params=pltpu.CompilerParams(dimension_semantics=("arbitrary",),
                                             vmem_limit_bytes=VMEM_LIMIT),
        name="post",
    )(x2d, y_diff, cb, u, u, mqn, sig, km, vm, conv_w, w_branch, w_out, g_ffn, w_rt, b_r, tri)


def _experts_kernel(be_ref, t0_ref, na_ref, ps_ref, cnt_ref, cb_ref, nt_ref, of_ref,
                    rows_hbm, zeros_hbm, wgu_f32, bgu_ref, wd_f32, bd_ref,
                    ys_out, xbuf, sem, wgu_ref, wd_ref, *, n_tiles, slots_per_tile):
    b = pl.program_id(0)
    n_active = na_ref[0]
    slot = b % 2

    @pl.when(jnp.logical_and(b < n_active,
                             jnp.logical_or(b == 0, be_ref[b] != be_ref[jnp.maximum(b - 1, 0)])))
    def _():
        wgu_ref[...] = wgu_f32[0].astype(BF16)
        wd_ref[...] = wd_f32[0].astype(BF16)

    def issue(blk, buf_slot):
        e = be_ref[blk]
        r0 = blk * MOE_BLOCK - ps_ref[e]
        r1 = r0 + MOE_BLOCK
        base = buf_slot * MOE_BLOCK
        sm = sem.at[buf_slot]

        def cond(i):
            return jnp.logical_and(i < n_tiles, cb_ref[e * n_tiles + jnp.minimum(i, n_tiles - 1)] < r1)

        def body(i):
            c = cb_ref[e * n_tiles + i]
            lo = jnp.maximum(c, r0)
            hi = jnp.minimum(c + nt_ref[e * n_tiles + i], r1)
            src = i * slots_per_tile + of_ref[e * n_tiles + i] + (lo - c)
            _copy_run(rows_hbm, xbuf, sm, src, base + (lo - r0), jnp.maximum(hi - lo, 0),
                      min(TM_POST, MOE_BLOCK))
            return i + 1

        lax.while_loop(cond, body, t0_ref[blk])
        valid = jnp.clip(cnt_ref[e] - r0, 0, MOE_BLOCK)
        _copy_run(zeros_hbm, xbuf, sm, 0, base + valid, MOE_BLOCK - valid, MOE_BLOCK)

    @pl.when(b == 0)
    def _():
        issue(b, 0)

    @pl.when(b + 1 < n_active)
    def _():
        issue(b + 1, 1 - slot)

    def compute(n_rows):
        base = pl.multiple_of(slot * MOE_BLOCK, MOE_BLOCK)
        _wait_rows(MOE_BLOCK, rows_hbm, xbuf, sem.at[slot], base)
        x = _load_rows(xbuf, base, n_rows).astype(BF16)
        d_ff = wd_ref.shape[0]

        def gate_up(j):
            gs = slice(j * FF_CHUNK, (j + 1) * FF_CHUNK)
            us = slice(d_ff + j * FF_CHUNK, d_ff + (j + 1) * FF_CHUNK)
            return (jnp.dot(x, wgu_ref[:, gs], preferred_element_type=F32) + bgu_ref[0, :, gs],
                    jnp.dot(x, wgu_ref[:, us], preferred_element_type=F32) + bgu_ref[0, :, us])

        acts = []
        nxt = gate_up(0)
        for j in range(d_ff // FF_CHUNK):
            g, u = nxt
            if (j + 1) * FF_CHUNK < d_ff:
                nxt = gate_up(j + 1)
            g = jnp.minimum(g, SWIGLU_LIMIT)
            u = jnp.clip(u, -SWIGLU_LIMIT, SWIGLU_LIMIT)
            acts.append(((u + 1.0) * (g * (1.0 / (1.0 + jnp.exp(-SWIGLU_ALPHA * g))))).astype(BF16))
        act = jnp.concatenate(acts, axis=1)
        for n in range(D_MODEL // FF_CHUNK):
            cols = slice(n * FF_CHUNK, (n + 1) * FF_CHUNK)
            yn = jnp.dot(act, wd_ref[:, cols], preferred_element_type=F32) + bd_ref[0, :, cols]
            for c in range(FF_CHUNK // LANES):
                ys_out[_chunk_rows(0, n_rows, n * (FF_CHUNK // LANES) + c), :] = (
                    yn[:, c * LANES:(c + 1) * LANES])
        if n_rows < MOE_BLOCK:
            ys_out[n_rows * ROW_CHUNKS:, :] = jnp.zeros(((MOE_BLOCK - n_rows) * ROW_CHUNKS, LANES), F32)

    @pl.when(b < n_active)
    def _():
        compute(MOE_BLOCK)

    @pl.when(b >= n_active)
    def _():
        ys_out[...] = jnp.zeros(ys_out.shape, F32)


def _experts(block_expert, first_tile, n_active, pstart, counts, cbase_e, ntile_e, off_e,
             rows, zero_rows, wgu, bgu, wd, bd, n_tiles):
    n_blocks = block_expert.shape[0]
    ff2 = wgu.shape[2]
    by_expert = lambda b, be, *_: (be[b], 0, 0)
    grid_spec = pltpu.PrefetchScalarGridSpec(
        num_scalar_prefetch=8,
        grid=(n_blocks,),
        in_specs=[
            pl.BlockSpec(memory_space=pl.ANY),
            pl.BlockSpec(memory_space=pl.ANY),
            pl.BlockSpec((1, D_MODEL, ff2), by_expert),
            pl.BlockSpec((1, 1, ff2), by_expert),
            pl.BlockSpec((1, ff2 // 2, D_MODEL), by_expert),
            pl.BlockSpec((1, 1, D_MODEL), by_expert),
        ],
        out_specs=pl.BlockSpec((MOE_BLOCK * ROW_CHUNKS, LANES), lambda b, *_: (b, 0)),
        scratch_shapes=[
            pltpu.VMEM((2 * MOE_BLOCK * ROW_CHUNKS, LANES), F32),
            pltpu.SemaphoreType.DMA((2,)),
            pltpu.VMEM((D_MODEL, ff2), BF16),
            pltpu.VMEM((ff2 // 2, D_MODEL), BF16),
        ],
    )
    return pl.pallas_call(
        functools.partial(_experts_kernel, n_tiles=n_tiles, slots_per_tile=TOP_K * TM_POST),
        grid_spec=grid_spec,
        out_shape=jax.ShapeDtypeStruct((n_blocks * MOE_BLOCK * ROW_CHUNKS, LANES), F32),
        compiler_params=pltpu.CompilerParams(dimension_semantics=("arbitrary",),
                                             vmem_limit_bytes=VMEM_LIMIT_EXPERTS),
        name="experts",
    )(block_expert, first_tile, n_active, pstart, counts, cbase_e, ntile_e, off_e,
      rows, zero_rows, wgu, bgu, wd, bd)


def _combine_kernel(ps_ref, cb_ref, nt_ref, of_ref, slot_ref, gate_ref, x1_ref, ys_hbm, o_ref, buf,
                    sem):
    i = pl.program_id(0)
    n = pl.num_programs(0)
    tm = x1_ref.shape[0]
    n_slots = TOP_K * tm
    cur = i % 2

    def issue(tile, buf_slot):
        def body(e, carry):
            k = tile * N_EXPERTS + e
            _copy_run(ys_hbm, buf, sem.at[buf_slot], ps_ref[e] + cb_ref[k],
                      buf_slot * n_slots + of_ref[k], nt_ref[k], tm)
            return carry
        lax.fori_loop(0, N_EXPERTS, body, 0)

    @pl.when(i == 0)
    def _():
        issue(i, 0)

    @pl.when(i + 1 < n)
    def _():
        issue(i + 1, 1 - cur)

    base = pl.multiple_of(cur * n_slots, n_slots)
    _wait_rows(n_slots, ys_hbm, buf, sem.at[cur], base)
    y = _load_rows(buf, base, n_slots).astype(BF16)
    jdx = lax.broadcasted_iota(jnp.int32, (n_slots, tm), 0)
    pick = jnp.where(jdx == slot_ref[0:1, :], gate_ref[0:1, :], jnp.where(
        jdx == slot_ref[1:2, :], gate_ref[1:2, :], jnp.where(
            jdx == slot_ref[2:3, :], gate_ref[2:3, :], jnp.where(
                jdx == slot_ref[3:4, :], gate_ref[3:4, :], 0.0)))).astype(BF16)
    o_ref[...] = x1_ref[...] + lax.dot_general(pick, y, (((0,), (0,)), ((), ())),
                                               preferred_element_type=F32)


def _combine(pstart, cbase_t, ntile_t, off_t, slots, gates, x1, ys_rows):
    t = x1.shape[0]
    tm = TM_POST
    n_slots = TOP_K * tm
    grid_spec = pltpu.PrefetchScalarGridSpec(
        num_scalar_prefetch=4,
        grid=(t // tm,),
        in_specs=[
            pl.BlockSpec((TOP_K, tm), lambda i, *_: (0, i)),
            pl.BlockSpec((TOP_K, tm), lambda i, *_: (0, i)),
            pl.BlockSpec((tm, D_MODEL), lambda i, *_: (i, 0)),
            pl.BlockSpec(memory_space=pl.ANY),
        ],
        out_specs=pl.BlockSpec((tm, D_MODEL), lambda i, *_: (i, 0)),
        scratch_shapes=[
            pltpu.VMEM((2 * n_slots * ROW_CHUNKS, LANES), F32),
            pltpu.SemaphoreType.DMA((2,)),
        ],
    )
    return pl.pallas_call(
        _combine_kernel,
        grid_spec=grid_spec,
        out_shape=jax.ShapeDtypeStruct((t, D_MODEL), F32),
        compiler_params=pltpu.CompilerParams(dimension_semantics=("arbitrary",),
                                             vmem_limit_bytes=VMEM_LIMIT),
        name="combine",
    )(pstart, cbase_t, ntile_t, off_t, slots, gates, x1, ys_rows)


def kernel(x, mem, g_mix, w_in, g_q_diff, g_k_diff, lambda_q1, lambda_k1, lambda_q2, lambda_k2,
           g_subln, conv_w, g_mem, w_mem_kv, g_q_mem, g_k_mem, w_branch, w_out, g_ffn, w_router,
           b_router, w_gate_up, b_gate_up, w_down, b_down):
    batch, seq, d = x.shape
    t = batch * seq
    depth = g_mix.shape[0]
    assert depth == 1 and d == D_MODEL and seq % TQ == 0 and t % TM_PROJ == 0

    inv_freq = 1.0 / (ROPE_THETA ** (jnp.arange(0, DIFF_HEAD_DIM, 2, dtype=F32) / DIFF_HEAD_DIM))
    freq = jnp.tile(inv_freq, 4)[None, :]
    half = jnp.ones((DIFF_HEAD_DIM // 2,), F32)
    sign = jnp.tile(jnp.concatenate([-half, half]), 2)[None, :]
    ang_r = jnp.arange(TM_PROJ, dtype=F32)[:, None] * freq
    cos_r, sin_r = jnp.cos(ang_r), jnp.sin(ang_r)
    grp = jnp.arange(SEC) // DIFF_HEAD_DIM
    bd = jnp.where(grp[:, None] == grp[None, :], 1.0 / DIFF_HEAD_DIM, 0.0).astype(BF16)
    tok = jnp.arange(TM_POST)
    tri = (tok[:, None] < tok[None, :]).astype(BF16)

    x2d = x.reshape(t, d)
    l = 0
    lam_init = 0.8 - 0.6 * math.exp(-0.3 * l)
    km, vm = _mem_kv(mem.reshape(batch * N_MEM, d), g_mem[l][None, :], w_mem_kv[l].astype(BF16),
                     g_k_mem[l][None, :])
    qn, kn, vt, cb, u, mqn, sig = _in_proj(
        x2d, g_mix[l][None, :], w_in[l].astype(BF16), freq, sign, cos_r, sin_r, bd,
        jnp.tile(g_q_diff[l], SEC // DIFF_HEAD_DIM)[None, :],
        jnp.tile(g_k_diff[l], SEC // DIFF_HEAD_DIM)[None, :],
        g_q_mem[l][None, :], seq)
    lam_rows = jnp.stack([lambda_q1[l], lambda_k1[l], lambda_q2[l], lambda_k2[l]]).astype(F32)
    score_bound = (DIFF_HEAD_DIM * (DIFF_HEAD_DIM ** -0.5 * LOG2E)
                   * jnp.max(jnp.abs(g_q_diff[l])) * jnp.max(jnp.abs(g_k_diff[l]))).astype(F32)
    y_diff = _diff_attn(qn, kn, vt, lam_rows, g_subln[l][None, :], score_bound, batch, seq, lam_init)
    x1, rows, slots, gates, cbase, ntile, off, counts = _post(
        x2d, y_diff, cb, u, mqn, sig, km, vm, conv_w[l], w_branch[l].astype(BF16),
        w_out[l].astype(BF16), g_ffn[l][None, :], w_router[l], b_router[l][:, None], tri, seq)

    n_tiles = t // TM_POST
    n_assign = t * TOP_K
    n_rows = -(-(n_assign + N_EXPERTS * (MOE_BLOCK - 1)) // MOE_BLOCK) * MOE_BLOCK
    n_blocks = n_rows // MOE_BLOCK
    cnt = counts[:, 0]
    padded = (cnt + MOE_BLOCK - 1) // MOE_BLOCK * MOE_BLOCK
    pend = jnp.cumsum(padded)
    pstart = pend - padded
    block_row0 = jnp.arange(n_blocks, dtype=jnp.int32) * MOE_BLOCK
    block_expert = jnp.minimum(
        jnp.sum((pend[None, :] <= block_row0[:, None]).astype(jnp.int32), axis=1), N_EXPERTS - 1)
    n_active = (pend[-1:] // MOE_BLOCK).astype(jnp.int32)
    cbase_t = cbase[:, 0].reshape(n_tiles, N_EXPERTS)
    ntile_t = ntile[:, 0].reshape(n_tiles, N_EXPERTS)
    off_t = off[:, 0].reshape(n_tiles, N_EXPERTS)
    run_end = (pstart[None, :] + cbase_t + ntile_t).reshape(1, -1)
    first_tile = (jnp.sum((run_end <= block_row0[:, None]).astype(jnp.int32), axis=1)
                  - block_expert * n_tiles)
    first_tile = jnp.clip(first_tile, 0, n_tiles - 1).astype(jnp.int32)

    zero_rows = jnp.zeros((MOE_BLOCK * ROW_CHUNKS, LANES), F32)
    ys_rows = _experts(block_expert, first_tile, n_active, pstart, cnt,
                       cbase_t.T.reshape(-1), ntile_t.T.reshape(-1), off_t.T.reshape(-1),
                       rows, zero_rows, w_gate_up[l], b_gate_up[l][:, None, :],
                       w_down[l], b_down[l][:, None, :], n_tiles)
    out = _combine(pstart, cbase_t.reshape(-1), ntile_t.reshape(-1), off_t.reshape(-1), slots, gates,
                   x1, ys_rows)
    return out.reshape(batch, seq, d)
```

```python
import functools
import math

import jax
import jax.numpy as jnp
from jax import lax
from jax.experimental import pallas as pl
from jax.experimental.pallas import tpu as pltpu

F32 = jnp.float32
BF16 = jnp.bfloat16

D_MODEL = 1024
CHUNK = 64
EPS = 1e-6
ROPE_THETA = 10000.0
DIFF_HEADS = 4
DIFF_HEAD_DIM = 64
DIFF_V_DIM = 2 * DIFF_HEAD_DIM
N_MEM = 256
MEM_HEADS = 4
MEM_HEAD_DIM = 128
SEC = 512
N_SEC = 13
N_EXPERTS = 32
TOP_K = 4
SWIGLU_LIMIT = 7.0
SWIGLU_ALPHA = 1.702
MOE_BLOCK = 512
LANES = 128
ROW_CHUNKS = D_MODEL // LANES
LOG2E = 1.4426950408889634
NEG_BIG = -1e30

TM_PROJ = 512
TQ = 2048
ATTN_STRIP = 512
MAX_FIXED_SHIFT = 40.0
TM_POST = 256
FF_CHUNK = 256
VMEM_LIMIT = 48 * 1024 * 1024
VMEM_LIMIT_EXPERTS = 56 * 1024 * 1024


def _rms(x, eps=EPS):
    return x * lax.rsqrt(jnp.mean(x * x, axis=-1, keepdims=True) + eps)


def _mem_kv_kernel(mem_ref, g_ref, w_ref, gk_ref, k_out, v_out):
    h = (_rms(mem_ref[...]) * g_ref[...]).astype(BF16)
    kv = jnp.dot(h, w_ref[...], preferred_element_type=F32)
    for hd in range(MEM_HEADS):
        sl = slice(hd * MEM_HEAD_DIM, (hd + 1) * MEM_HEAD_DIM)
        k_out[sl, :] = (_rms(kv[:, sl]) * gk_ref[...]).T.astype(BF16)
    v_out[...] = kv[:, SEC:].astype(BF16)


def _mem_kv(mem2d, g_mem, w_kv, g_k):
    rows = mem2d.shape[0]
    return pl.pallas_call(
        _mem_kv_kernel,
        grid=(rows // N_MEM,),
        in_specs=[
            pl.BlockSpec((N_MEM, D_MODEL), lambda i: (i, 0)),
            pl.BlockSpec((1, D_MODEL), lambda i: (0, 0)),
            pl.BlockSpec((D_MODEL, 2 * SEC), lambda i: (0, 0)),
            pl.BlockSpec((1, MEM_HEAD_DIM), lambda i: (0, 0)),
        ],
        out_specs=[pl.BlockSpec((SEC, N_MEM), lambda i: (i, 0)),
                   pl.BlockSpec((N_MEM, SEC), lambda i: (i, 0))],
        out_shape=[jax.ShapeDtypeStruct((rows // N_MEM * SEC, N_MEM), BF16),
                   jax.ShapeDtypeStruct((rows, SEC), BF16)],
        compiler_params=pltpu.CompilerParams(vmem_limit_bytes=VMEM_LIMIT),
        name="mem_kv",
    )(mem2d, g_mem, w_kv, g_k)


def _in_proj_kernel(x_ref, g_ref, w_ref, freq_ref, sign_ref, cosr_ref, sinr_ref, bd_ref, gq_ref, gk_ref,
                    gqm_ref, q_out, k_out, v_out, cb_out, u_out, mq_out, sig_out, *, tiles_per_seq):
    tm = x_ref.shape[0]
    h = (_rms(x_ref[...]) * g_ref[...]).astype(BF16)

    def proj(sec):
        return jnp.dot(h, w_ref[:, sec * SEC:(sec + 1) * SEC], preferred_element_type=F32)

    base = ((pl.program_id(0) % tiles_per_seq) * tm).astype(F32) * freq_ref[...]
    cos_b, sin_b = jnp.cos(base), jnp.sin(base)
    cos128 = cos_b * cosr_ref[...] - sin_b * sinr_ref[...]
    sin128 = (sin_b * cosr_ref[...] + cos_b * sinr_ref[...]) * sign_ref[...]
    cos = jnp.concatenate([cos128] * (SEC // LANES), axis=1)
    sin = jnp.concatenate([sin128] * (SEC // LANES), axis=1)
    lane = lax.broadcasted_iota(jnp.int32, (tm, SEC), 1)
    first_half = (lane & (DIFF_HEAD_DIM // 2)) == 0

    def norm_rope(a, g, scale):
        ms = jnp.dot((a * a).astype(BF16), bd_ref[...], preferred_element_type=F32)
        y = a * lax.rsqrt(ms + EPS) * g
        partner = jnp.where(first_half,
                            pltpu.roll(y, SEC - DIFF_HEAD_DIM // 2, 1),
                            pltpu.roll(y, DIFF_HEAD_DIM // 2, 1))
        return (y * cos + partner * sin) * scale

    q_out[...] = norm_rope(proj(0), gq_ref[...], DIFF_HEAD_DIM ** -0.5 * LOG2E).astype(BF16)
    k_out[...] = norm_rope(proj(1), gk_ref[...], 1.0).astype(BF16)
    v = proj(2)
    for hd in range(DIFF_HEADS):
        v_out[0, hd, 0] = v[:, hd * DIFF_V_DIM:(hd + 1) * DIFF_V_DIM].T.astype(BF16)
    cb_out[...] = proj(3).astype(BF16)
    u_out[...] = proj(4) * proj(5)
    mq = proj(6)
    for hd in range(MEM_HEADS):
        sl = slice(hd * MEM_HEAD_DIM, (hd + 1) * MEM_HEAD_DIM)
        mq_out[:, sl] = (_rms(mq[:, sl]) * gqm_ref[...]
                         * (MEM_HEAD_DIM ** -0.5 * LOG2E)).astype(BF16)
    for s in range(7, N_SEC):
        a = proj(s)
        sig_out[:, (s - 7) * SEC:(s - 6) * SEC] = (1.0 / (1.0 + jnp.exp(-a))).astype(BF16)


def _in_proj(x2d, g_mix, w_in, freq, sign, cos_r, sin_r, bd, gq, gk, gqm, seq):
    t = x2d.shape[0]
    tm = TM_PROJ
    tiles_per_seq = seq // tm
    tiles_per_key = TQ // tm
    row = lambda i: (i, 0)
    const = lambda i: (0, 0)
    outs = [
        jax.ShapeDtypeStruct((t, SEC), BF16),
        jax.ShapeDtypeStruct((t, SEC), BF16),
        jax.ShapeDtypeStruct((t // seq, DIFF_HEADS, seq // TQ, DIFF_V_DIM, TQ), BF16),
        jax.ShapeDtypeStruct((t, SEC), BF16),
        jax.ShapeDtypeStruct((t, SEC), F32),
        jax.ShapeDtypeStruct((t, SEC), BF16),
        jax.ShapeDtypeStruct((t, 3 * D_MODEL), BF16),
    ]
    return pl.pallas_call(
        functools.partial(_in_proj_kernel, tiles_per_seq=tiles_per_seq),
        grid=(t // tm,),
        in_specs=[
            pl.BlockSpec((tm, D_MODEL), row),
            pl.BlockSpec((1, D_MODEL), const),
            pl.BlockSpec((D_MODEL, N_SEC * SEC), const),
            pl.BlockSpec((1, LANES), const),
            pl.BlockSpec((1, LANES), const),
            pl.BlockSpec((tm, LANES), const),
            pl.BlockSpec((tm, LANES), const),
            pl.BlockSpec((SEC, SEC), const),
            pl.BlockSpec((1, SEC), const),
            pl.BlockSpec((1, SEC), const),
            pl.BlockSpec((1, MEM_HEAD_DIM), const),
        ],
        out_specs=[
            pl.BlockSpec((1, DIFF_HEADS, 1, DIFF_V_DIM, tm),
                         lambda i: (i // tiles_per_seq, 0, (i % tiles_per_seq) // tiles_per_key,
                                    0, i % tiles_per_key))
            if o.ndim == 5 else pl.BlockSpec((tm, o.shape[1]), row) for o in outs],
        out_shape=outs,
        compiler_params=pltpu.CompilerParams(vmem_limit_bytes=VMEM_LIMIT),
        name="in_proj",
    )(x2d, g_mix, w_in, freq, sign, cos_r, sin_r, bd, gq, gk, gqm)


def _diff_attn_kernel(fixed_ref, q_ref, qn_ref, k_ref, vt_ref, lam_ref, gs_ref, bound_ref, o_ref,
                      acc_sc, s0_sc, l_sc, *, lam_init):
    i = pl.program_id(2)
    tq = q_ref.shape[0]
    def stacked_queries(ref):
        qt = ref[...].astype(F32).T
        dim = lax.broadcasted_iota(jnp.int32, qt.shape, 0)
        return jnp.concatenate([jnp.where(dim < DIFF_HEAD_DIM, qt, 0.0),
                                jnp.where(dim >= DIFF_HEAD_DIM, qt, 0.0)], axis=1).astype(BF16)

    qq = stacked_queries(q_ref)
    acc_sc[...] = jnp.zeros(acc_sc.shape, F32)
    n_strips = 2 * tq // ATTN_STRIP

    @pl.when(i == 0)
    def _():
        s0_sc[...] = jnp.dot(k_ref[0:tq, :], qq[:, 0:ATTN_STRIP], preferred_element_type=F32)

    def scores(j, c, nk):
        off = pl.multiple_of(j * tq, tq)
        return jnp.dot(k_ref[pl.ds(off, nk), :], qq[:, c * ATTN_STRIP:(c + 1) * ATTN_STRIP],
                       preferred_element_type=F32)

    def step(j, ms, ls, masked, fixed):
        q_offs = [(c * ATTN_STRIP) % tq for c in range(n_strips)]
        nks = [min(tq, qo + ATTN_STRIP) if masked else tq for qo in q_offs]
        ms_new, ls_new = [], []
        s = s0_sc[0:nks[0], :]
        for c in range(n_strips):
            cols = slice(c * ATTN_STRIP, (c + 1) * ATTN_STRIP)
            if c + 1 < n_strips:
                s_next = scores(j, c + 1, nks[c + 1])
            elif not masked:
                s0_sc[...] = scores(j + 1, 0, tq)
            if masked:
                r = lax.broadcasted_iota(jnp.int32, s.shape, 0)
                col = lax.broadcasted_iota(jnp.int32, s.shape, 1)
                s = jnp.where((r // CHUNK) <= ((col + q_offs[c]) // CHUNK), s, NEG_BIG)
            if fixed:
                p = jnp.exp2(s - bound_ref[...])
                ls_new.append(ls[c] + jnp.sum(p, axis=0, keepdims=True))
                ms_new.append(ms[c])
                acc_sc[:, cols] = acc_sc[:, cols] + jnp.dot(
                    vt_ref[0, 0, j][:, :nks[c]], p.astype(BF16), preferred_element_type=F32)
            else:
                m_new = jnp.maximum(ms[c], jnp.max(s, axis=0, keepdims=True))
                p = jnp.exp2(s - m_new)
                alpha = jnp.exp2(ms[c] - m_new)
                ls_new.append(alpha * ls[c] + jnp.sum(p, axis=0, keepdims=True))
                ms_new.append(m_new)
                acc_sc[:, cols] = alpha * acc_sc[:, cols] + jnp.dot(
                    vt_ref[0, 0, j][:, :nks[c]], p.astype(BF16), preferred_element_type=F32)
            if c + 1 < n_strips:
                s = s_next
        return tuple(ms_new), tuple(ls_new)

    def run(fixed):
        m0 = tuple(jnp.full((1, ATTN_STRIP), NEG_BIG, F32) for _ in range(n_strips))
        l0 = tuple(jnp.zeros((1, ATTN_STRIP), F32) for _ in range(n_strips))
        m, l = lax.fori_loop(0, i, lambda j, c: step(j, c[0], c[1], False, fixed), (m0, l0))
        m, l = step(i, m, l, True, fixed)
        l_sc[...] = jnp.concatenate(l, axis=1)

    @pl.when(fixed_ref[0] == 1)
    def _():
        run(True)

    @pl.when(fixed_ref[0] != 1)
    def _():
        run(False)

    qt_next = qn_ref[0:ATTN_STRIP, :].astype(F32).T
    dim_next = lax.broadcasted_iota(jnp.int32, qt_next.shape, 0)
    s0_sc[...] = jnp.dot(k_ref[0:tq, :], jnp.where(dim_next < DIFF_HEAD_DIM, qt_next, 0.0).astype(BF16),
                         preferred_element_type=F32)

    ot = acc_sc[...] / l_sc[...]
    lam = (jnp.exp(jnp.sum(lam_ref[0:1, :] * lam_ref[1:2, :], axis=-1, keepdims=True))
           - jnp.exp(jnp.sum(lam_ref[2:3, :] * lam_ref[3:4, :], axis=-1, keepdims=True))
           + lam_init)
    d = (ot[:, :tq] - lam * ot[:, tq:]).T
    o_ref[...] = (_rms(d) * gs_ref[...] * (1.0 - lam_init)).astype(BF16)


def _diff_attn(qn, kn, vt, lam_rows, g_subln, score_bound, batch, seq, lam_init):
    t = qn.shape[0]
    nq = seq // TQ
    use_fixed = (score_bound <= MAX_FIXED_SHIFT).astype(jnp.int32).reshape(1)
    grid_spec = pltpu.PrefetchScalarGridSpec(
        num_scalar_prefetch=1,
        grid=(batch, DIFF_HEADS, nq),
        in_specs=[
            pl.BlockSpec((TQ, DIFF_V_DIM), lambda b, h, i, *_: (b * nq + i, h)),
            pl.BlockSpec((TQ, DIFF_V_DIM), lambda b, h, i, *_: (b * nq + jnp.minimum(i + 1, nq - 1), h)),
            pl.BlockSpec((seq, DIFF_V_DIM), lambda b, h, i, *_: (b, h)),
            pl.BlockSpec((1, 1, nq, DIFF_V_DIM, TQ), lambda b, h, i, *_: (b, h, 0, 0, 0)),
            pl.BlockSpec((4, DIFF_HEAD_DIM), lambda b, h, i, *_: (0, 0)),
            pl.BlockSpec((1, DIFF_V_DIM), lambda b, h, i, *_: (0, 0)),
            pl.BlockSpec((1, 1), lambda b, h, i, *_: (0, 0)),
        ],
        out_specs=pl.BlockSpec((TQ, DIFF_V_DIM), lambda b, h, i, *_: (b * nq + i, h)),
        scratch_shapes=[pltpu.VMEM((DIFF_V_DIM, 2 * TQ), F32), pltpu.VMEM((TQ, ATTN_STRIP), F32),
                        pltpu.VMEM((1, 2 * TQ), F32)],
    )
    return pl.pallas_call(
        functools.partial(_diff_attn_kernel, lam_init=lam_init),
        grid_spec=grid_spec,
        out_shape=jax.ShapeDtypeStruct((t, SEC), BF16),
        compiler_params=pltpu.CompilerParams(
            dimension_semantics=("arbitrary", "arbitrary", "arbitrary"),
            vmem_limit_bytes=VMEM_LIMIT),
        name="diff_attn",
    )(use_fixed, qn, qn, kn, vt, lam_rows, g_subln, score_bound.reshape(1, 1))


def _row_span(ref, row, n):
    return ref.at[pl.ds(pl.multiple_of(row * ROW_CHUNKS, ROW_CHUNKS), n * ROW_CHUNKS), :]


def _chunk_rows(row0, n, c):
    return pl.ds(row0 * ROW_CHUNKS + c, n, stride=ROW_CHUNKS)


def _load_rows(ref, row0, n):
    return jnp.concatenate([ref[_chunk_rows(row0, n, c), :] for c in range(ROW_CHUNKS)], axis=1)


def _store_rows(ref, n, val):
    for c in range(ROW_CHUNKS):
        ref[_chunk_rows(0, n, c), :] = val[:, c * LANES:(c + 1) * LANES]


def _copy_run(src_hbm, buf, sem, src_row, dst_row, n, max_n):
    piece = 1 << (max_n.bit_length() - 1)
    while piece >= 1:
        take = (n & piece) != 0

        @pl.when(take)
        def _(piece=piece, src_row=src_row, dst_row=dst_row):
            pltpu.make_async_copy(_row_span(src_hbm, src_row, piece), _row_span(buf, dst_row, piece),
                                  sem).start()

        step = jnp.where(take, piece, 0)
        src_row = src_row + step
        dst_row = dst_row + step
        piece //= 2


def _wait_rows(n, src_hbm, buf, sem, base):
    pltpu.make_async_copy(_row_span(src_hbm, 0, n), _row_span(buf, base, n), sem).wait()


def _post_kernel(x_ref, yd_ref, cb_ref, u_ref, up_ref, mq_ref, sig_ref, km_ref, vm_ref,
                 cw_ref, wb_ref, wo_ref, gf_ref, wr_ref, br_ref, tri_ref,
                 x1_out, rows_out, slot_out, gate_out, cbase_out, ntile_out, off_out, cnt_out, carry_sc,
                 *, seq):
    i = pl.program_id(0)
    tm = x_ref.shape[0]

    @pl.when(i == 0)
    def _():
        carry_sc[...] = jnp.zeros(carry_sc.shape, F32)

    u = u_ref[...]
    seq_start = (i * tm) % seq == 0
    up = jnp.where(seq_start, 0.0, up_ref[...])
    r = lax.broadcasted_iota(jnp.int32, u.shape, 0)
    u1 = jnp.where(r == 0, up[7:8, :], pltpu.roll(u, 1, 0))
    u2 = jnp.where(r == 0, up[6:7, :], jnp.where(r == 1, up[7:8, :], pltpu.roll(u, 2, 0)))
    y_conv = cb_ref[...].astype(F32) * (cw_ref[0:1, :] * u2 + cw_ref[1:2, :] * u1 + cw_ref[2:3, :] * u)

    def mem_scores(hd):
        sl = slice(hd * MEM_HEAD_DIM, (hd + 1) * MEM_HEAD_DIM)
        return jnp.dot(mq_ref[:, sl], km_ref[sl, :], preferred_element_type=F32)

    y_mem = []
    s_next = mem_scores(0)
    for hd in range(MEM_HEADS):
        sl = slice(hd * MEM_HEAD_DIM, (hd + 1) * MEM_HEAD_DIM)
        s = s_next
        if hd + 1 < MEM_HEADS:
            s_next = mem_scores(hd + 1)
        if hd == 0:
            merged = (sig_ref[:, 0:D_MODEL].astype(F32)
                      * jnp.dot(yd_ref[...], wb_ref[0], preferred_element_type=F32))
        elif hd == 1:
            merged += (sig_ref[:, D_MODEL:2 * D_MODEL].astype(F32)
                       * jnp.dot(y_conv.astype(BF16), wb_ref[1], preferred_element_type=F32))
        p = jnp.exp2(s - jnp.max(s, axis=-1, keepdims=True))
        o = jnp.dot(p.astype(BF16), vm_ref[:, sl], preferred_element_type=F32)
        y_mem.append(o / jnp.sum(p, axis=-1, keepdims=True))

    merged += (sig_ref[:, 2 * D_MODEL:3 * D_MODEL].astype(F32)
               * jnp.dot(jnp.concatenate(y_mem, axis=1).astype(BF16), wb_ref[2],
                         preferred_element_type=F32))
    x1 = x_ref[...] + jnp.dot(merged.astype(BF16), wo_ref[...], preferred_element_type=F32)
    x1_out[...] = x1

    h2 = _rms(x1) * gf_ref[...]

    w = wr_ref[...]
    w_hi = w.astype(BF16)
    w_lo = (w - w_hi.astype(F32)).astype(BF16)
    h_hi = h2.astype(BF16)
    h_lo = (h2 - h_hi.astype(F32)).astype(BF16)
    part = jnp.dot(h_hi, jnp.concatenate([w_hi, w_lo], axis=1), preferred_element_type=F32)
    by_token = (part[:, :N_EXPERTS] + part[:, N_EXPERTS:]
                + jnp.dot(h_lo, w_hi, preferred_element_type=F32))
    padded = jnp.concatenate([by_token, jnp.zeros((tm, LANES - N_EXPERTS), F32)], axis=1)
    logits = padded.T[:N_EXPERTS, :] + br_ref[...]
    eio = lax.broadcasted_iota(jnp.int32, logits.shape, 0)
    work = logits
    vals, hots = [], []
    for k in range(TOP_K):
        mk = jnp.max(work, axis=0, keepdims=True)
        ik = jnp.min(jnp.where(work == mk, eio, N_EXPERTS), axis=0, keepdims=True)
        hot = eio == ik
        work = jnp.where(hot, -jnp.inf, work)
        vals.append(mk)
        hots.append(hot)
    ex = [jnp.exp(v - vals[0]) for v in vals]
    den = ex[0] + ex[1] + ex[2] + ex[3]
    gates = [e / den for e in ex]

    assign = jnp.zeros(logits.shape, F32)
    for hot in hots:
        assign = jnp.where(hot, 1.0, assign)
    earlier = jnp.dot(assign.astype(BF16), tri_ref[...], preferred_element_type=F32)
    n_col = jnp.sum(assign, axis=1, keepdims=True)
    e_row = lax.broadcasted_iota(jnp.int32, (N_EXPERTS, LANES), 0)
    run = jnp.broadcast_to(n_col, (N_EXPERTS, LANES))
    shift = 1
    while shift < N_EXPERTS:
        run = run + jnp.where(e_row >= shift, pltpu.roll(run, shift, 0), 0.0)
        shift *= 2
    off_col = run[:, 0:1] - n_col
    slots = [jnp.sum(jnp.where(hot, earlier + off_col, 0.0), axis=0, keepdims=True).astype(jnp.int32)
             for hot in hots]
    for k in range(TOP_K):
        slot_out[k:k + 1, :] = slots[k]
        gate_out[k:k + 1, :] = gates[k]

    n_slots = TOP_K * tm
    jdx = lax.broadcasted_iota(jnp.int32, (n_slots, tm), 0)
    pick = jnp.where(jdx == slots[0], 1.0, jnp.where(jdx == slots[1], 1.0, jnp.where(
        jdx == slots[2], 1.0, jnp.where(jdx == slots[3], 1.0, 0.0)))).astype(BF16)
    rows = jnp.dot(pick, h_hi, preferred_element_type=F32)
    _store_rows(rows_out, n_slots, rows)

    cbase_out[...] = jnp.broadcast_to(carry_sc[...], cbase_out.shape).astype(jnp.int32)
    ntile_out[...] = jnp.broadcast_to(n_col, ntile_out.shape).astype(jnp.int32)
    off_out[...] = jnp.broadcast_to(off_col, off_out.shape).astype(jnp.int32)
    carry_sc[...] = carry_sc[...] + n_col
    cnt_out[...] = jnp.broadcast_to(carry_sc[...], cnt_out.shape).astype(jnp.int32)


def _post(x2d, y_diff, cb, u, mqn, sig, km, vm, conv_w, w_branch, w_out, g_ffn, w_rt, b_r, tri, seq):
    t = x2d.shape[0]
    tm = TM_POST
    n_tiles = t // tm
    row = lambda i: (i, 0)
    const = lambda i: (0, 0)
    table = jax.ShapeDtypeStruct((n_tiles * N_EXPERTS, LANES), jnp.int32)
    outs = [
        jax.ShapeDtypeStruct((t, D_MODEL), F32),
        jax.ShapeDtypeStruct((TOP_K * t * ROW_CHUNKS, LANES), F32),
        jax.ShapeDtypeStruct((TOP_K, t), jnp.int32),
        jax.ShapeDtypeStruct((TOP_K, t), F32),
        table, table, table,
        jax.ShapeDtypeStruct((N_EXPERTS, LANES), jnp.int32),
    ]
    return pl.pallas_call(
        functools.partial(_post_kernel, seq=seq),
        grid=(t // tm,),
        in_specs=[
            pl.BlockSpec((tm, D_MODEL), row),
            pl.BlockSpec((tm, SEC), row),
            pl.BlockSpec((tm, SEC), row),
            pl.BlockSpec((tm, SEC), row),
            pl.BlockSpec((8, SEC), lambda i: (jnp.maximum(i * (tm // 8) - 1, 0), 0)),
            pl.BlockSpec((tm, SEC), row),
            pl.BlockSpec((tm, 3 * D_MODEL), row),
            pl.BlockSpec((SEC, N_MEM), lambda i: ((i * tm) // seq, 0)),
            pl.BlockSpec((N_MEM, SEC), lambda i: ((i * tm) // seq, 0)),
            pl.BlockSpec((3, SEC), const),
            pl.BlockSpec((3, SEC, D_MODEL), lambda i: (0, 0, 0)),
            pl.BlockSpec((D_MODEL, D_MODEL), const),
            pl.BlockSpec((1, D_MODEL), const),
            pl.BlockSpec((D_MODEL, N_EXPERTS), const),
            pl.BlockSpec((N_EXPERTS, 1), const),
            pl.BlockSpec((tm, tm), const),
        ],
        out_specs=[
            pl.BlockSpec((tm, D_MODEL), row),
            pl.BlockSpec((TOP_K * tm * ROW_CHUNKS, LANES), row),
            pl.BlockSpec((TOP_K, tm), lambda i: (0, i)),
            pl.BlockSpec((TOP_K, tm), lambda i: (0, i)),
            pl.BlockSpec((N_EXPERTS, LANES), row),
            pl.BlockSpec((N_EXPERTS, LANES), row),
            pl.BlockSpec((N_EXPERTS, LANES), row),
            pl.BlockSpec((N_EXPERTS, LANES), const),
        ],
        out_shape=outs,
        scratch_shapes=[pltpu.VMEM((N_EXPERTS, 1), F32)],
        compiler_params=pltpu.CompilerParams(dimension_semantics=("arbitrary",),
                                             vmem_limit_bytes=VMEM_LIMIT),
        name="post",
    )(x2d, y_diff, cb, u, u, mqn, sig, km, vm, conv_w, w_branch, w_out, g_ffn, w_rt, b_r, tri)


def _experts_kernel(be_ref, t0_ref, na_ref, ps_ref, cnt_ref, cb_ref, nt_ref, of_ref,
                    rows_hbm, zeros_hbm, wgu_f32, bgu_ref, wd_f32, bd_ref,
                    ys_out, xbuf, sem, wgu_ref, wd_ref, *, n_tiles, slots_per_tile):
    b = pl.program_id(0)
    n_active = na_ref[0]
    slot = b % 2

    @pl.when(jnp.logical_and(b < n_active,
                             jnp.logical_or(b == 0, be_ref[b] != be_ref[jnp.maximum(b - 1, 0)])))
    def _():
        wgu_ref[...] = wgu_f32[0].astype(BF16)
        wd_ref[...] = wd_f32[0].astype(BF16)

    def issue(blk, buf_slot):
        e = be_ref[blk]
        r0 = blk * MOE_BLOCK - ps_ref[e]
        r1 = r0 + MOE_BLOCK
        base = buf_slot * MOE_BLOCK
        sm = sem.at[buf_slot]

        def cond(i):
            return jnp.logical_and(i < n_tiles, cb_ref[e * n_tiles + jnp.minimum(i, n_tiles - 1)] < r1)

        def body(i):
            c = cb_ref[e * n_tiles + i]
            lo = jnp.maximum(c, r0)
            hi = jnp.minimum(c + nt_ref[e * n_tiles + i], r1)
            src = i * slots_per_tile + of_ref[e * n_tiles + i] + (lo - c)
            _copy_run(rows_hbm, xbuf, sm, src, base + (lo - r0), jnp.maximum(hi - lo, 0),
                      min(TM_POST, MOE_BLOCK))
            return i + 1

        lax.while_loop(cond, body, t0_ref[blk])
        valid = jnp.clip(cnt_ref[e] - r0, 0, MOE_BLOCK)
        _copy_run(zeros_hbm, xbuf, sm, 0, base + valid, MOE_BLOCK - valid, MOE_BLOCK)

    @pl.when(b == 0)
    def _():
        issue(b, 0)

    @pl.when(b + 1 < n_active)
    def _():
        issue(b + 1, 1 - slot)

    def compute(n_rows):
        base = pl.multiple_of(slot * MOE_BLOCK, MOE_BLOCK)
        _wait_rows(MOE_BLOCK, rows_hbm, xbuf, sem.at[slot], base)
        x = _load_rows(xbuf, base, n_rows).astype(BF16)
        d_ff = wd_ref.shape[0]

        def gate_up(j):
            gs = slice(j * FF_CHUNK, (j + 1) * FF_CHUNK)
            us = slice(d_ff + j * FF_CHUNK, d_ff + (j + 1) * FF_CHUNK)
            return (jnp.dot(x, wgu_ref[:, gs], preferred_element_type=F32) + bgu_ref[0, :, gs],
                    jnp.dot(x, wgu_ref[:, us], preferred_element_type=F32) + bgu_ref[0, :, us])

        acts = []
        nxt = gate_up(0)
        for j in range(d_ff // FF_CHUNK):
            g, u = nxt
            if (j + 1) * FF_CHUNK < d_ff:
                nxt = gate_up(j + 1)
            g = jnp.minimum(g, SWIGLU_LIMIT)
            u = jnp.clip(u, -SWIGLU_LIMIT, SWIGLU_LIMIT)
            acts.append(((u + 1.0) * (g * (1.0 / (1.0 + jnp.exp(-SWIGLU_ALPHA * g))))).astype(BF16))
        act = jnp.concatenate(acts, axis=1)
        for n in range(D_MODEL // FF_CHUNK):
            cols = slice(n * FF_CHUNK, (n + 1) * FF_CHUNK)
            yn = jnp.dot(act, wd_ref[:, cols], preferred_element_type=F32) + bd_ref[0, :, cols]
            for c in range(FF_CHUNK // LANES):
                ys_out[_chunk_rows(0, n_rows, n * (FF_CHUNK // LANES) + c), :] = (
                    yn[:, c * LANES:(c + 1) * LANES])
        if n_rows < MOE_BLOCK:
            ys_out[n_rows * ROW_CHUNKS:, :] = jnp.zeros(((MOE_BLOCK - n_rows) * ROW_CHUNKS, LANES), F32)

    @pl.when(b < n_active)
    def _():
        compute(MOE_BLOCK)

    @pl.when(b >= n_active)
    def _():
        ys_out[...] = jnp.zeros(ys_out.shape, F32)


def _experts(block_expert, first_tile, n_active, pstart, counts, cbase_e, ntile_e, off_e,
             rows, zero_rows, wgu, bgu, wd, bd, n_tiles):
    n_blocks = block_expert.shape[0]
    ff2 = wgu.shape[2]
    by_expert = lambda b, be, *_: (be[b], 0, 0)
    grid_spec = pltpu.PrefetchScalarGridSpec(
        num_scalar_prefetch=8,
        grid=(n_blocks,),
        in_specs=[
            pl.BlockSpec(memory_space=pl.ANY),
            pl.BlockSpec(memory_space=pl.ANY),
            pl.BlockSpec((1, D_MODEL, ff2), by_expert),
            pl.BlockSpec((1, 1, ff2), by_expert),
            pl.BlockSpec((1, ff2 // 2, D_MODEL), by_expert),
            pl.BlockSpec((1, 1, D_MODEL), by_expert),
        ],
        out_specs=pl.BlockSpec((MOE_BLOCK * ROW_CHUNKS, LANES), lambda b, *_: (b, 0)),
        scratch_shapes=[
            pltpu.VMEM((2 * MOE_BLOCK * ROW_CHUNKS, LANES), F32),
            pltpu.SemaphoreType.DMA((2,)),
            pltpu.VMEM((D_MODEL, ff2), BF16),
            pltpu.VMEM((ff2 // 2, D_MODEL), BF16),
        ],
    )
    return pl.pallas_call(
        functools.partial(_experts_kernel, n_tiles=n_tiles, slots_per_tile=TOP_K * TM_POST),
        grid_spec=grid_spec,
        out_shape=jax.ShapeDtypeStruct((n_blocks * MOE_BLOCK * ROW_CHUNKS, LANES), F32),
        compiler_params=pltpu.CompilerParams(dimension_semantics=("arbitrary",),
                                             vmem_limit_bytes=VMEM_LIMIT_EXPERTS),
        name="experts",
    )(block_expert, first_tile, n_active, pstart, counts, cbase_e, ntile_e, off_e,
      rows, zero_rows, wgu, bgu, wd, bd)


def _combine_kernel(ps_ref, cb_ref, nt_ref, of_ref, slot_ref, gate_ref, x1_ref, ys_hbm, o_ref, buf,
                    sem):
    i = pl.program_id(0)
    n = pl.num_programs(0)
    tm = x1_ref.shape[0]
    n_slots = TOP_K * tm
    cur = i % 2

    def issue(tile, buf_slot):
        def body(e, carry):
            k = tile * N_EXPERTS + e
            _copy_run(ys_hbm, buf, sem.at[buf_slot], ps_ref[e] + cb_ref[k],
                      buf_slot * n_slots + of_ref[k], nt_ref[k], tm)
            return carry
        lax.fori_loop(0, N_EXPERTS, body, 0)

    @pl.when(i == 0)
    def _():
        issue(i, 0)

    @pl.when(i + 1 < n)
    def _():
        issue(i + 1, 1 - cur)

    base = pl.multiple_of(cur * n_slots, n_slots)
    _wait_rows(n_slots, ys_hbm, buf, sem.at[cur], base)
    y = _load_rows(buf, base, n_slots).astype(BF16)
    jdx = lax.broadcasted_iota(jnp.int32, (n_slots, tm), 0)
    pick = jnp.where(jdx == slot_ref[0:1, :], gate_ref[0:1, :], jnp.where(
        jdx == slot_ref[1:2, :], gate_ref[1:2, :], jnp.where(
            jdx == slot_ref[2:3, :], gate_ref[2:3, :], jnp.where(
                jdx == slot_ref[3:4, :], gate_ref[3:4, :], 0.0)))).astype(BF16)
    o_ref[...] = x1_ref[...] + lax.dot_general(pick, y, (((0,), (0,)), ((), ())),
                                               preferred_element_type=F32)


def _combine(pstart, cbase_t, ntile_t, off_t, slots, gates, x1, ys_rows):
    t = x1.shape[0]
    tm = TM_POST
    n_slots = TOP_K * tm
    grid_spec = pltpu.PrefetchScalarGridSpec(
        num_scalar_prefetch=4,
        grid=(t // tm,),
        in_specs=[
            pl.BlockSpec((TOP_K, tm), lambda i, *_: (0, i)),
            pl.BlockSpec((TOP_K, tm), lambda i, *_: (0, i)),
            pl.BlockSpec((tm, D_MODEL), lambda i, *_: (i, 0)),
            pl.BlockSpec(memory_space=pl.ANY),
        ],
        out_specs=pl.BlockSpec((tm, D_MODEL), lambda i, *_: (i, 0)),
        scratch_shapes=[
            pltpu.VMEM((2 * n_slots * ROW_CHUNKS, LANES), F32),
            pltpu.SemaphoreType.DMA((2,)),
        ],
    )
    return pl.pallas_call(
        _combine_kernel,
        grid_spec=grid_spec,
        out_shape=jax.ShapeDtypeStruct((t, D_MODEL), F32),
        compiler_params=pltpu.CompilerParams(dimension_semantics=("arbitrary",),
                                             vmem_limit_bytes=VMEM_LIMIT),
        name="combine",
    )(pstart, cbase_t, ntile_t, off_t, slots, gates, x1, ys_rows)


def kernel(x, mem, g_mix, w_in, g_q_diff, g_k_diff, lambda_q1, lambda_k1, lambda_q2, lambda_k2,
           g_subln, conv_w, g_mem, w_mem_kv, g_q_mem, g_k_mem, w_branch, w_out, g_ffn, w_router,
           b_router, w_gate_up, b_gate_up, w_down, b_down):
    batch, seq, d = x.shape
    t = batch * seq
    depth = g_mix.shape[0]
    assert depth == 1 and d == D_MODEL and seq % TQ == 0 and t % TM_PROJ == 0

    inv_freq = 1.0 / (ROPE_THETA ** (jnp.arange(0, DIFF_HEAD_DIM, 2, dtype=F32) / DIFF_HEAD_DIM))
    freq = jnp.tile(inv_freq, 4)[None, :]
    half = jnp.ones((DIFF_HEAD_DIM // 2,), F32)
    sign = jnp.tile(jnp.concatenate([-half, half]), 2)[None, :]
    ang_r = jnp.arange(TM_PROJ, dtype=F32)[:, None] * freq
    cos_r, sin_r = jnp.cos(ang_r), jnp.sin(ang_r)
    grp = jnp.arange(SEC) // DIFF_HEAD_DIM
    bd = jnp.where(grp[:, None] == grp[None, :], 1.0 / DIFF_HEAD_DIM, 0.0).astype(BF16)
    tok = jnp.arange(TM_POST)
    tri = (tok[:, None] < tok[None, :]).astype(BF16)

    x2d = x.reshape(t, d)
    l = 0
    lam_init = 0.8 - 0.6 * math.exp(-0.3 * l)
    km, vm = _mem_kv(mem.reshape(batch * N_MEM, d), g_mem[l][None, :], w_mem_kv[l].astype(BF16),
                     g_k_mem[l][None, :])
    qn, kn, vt, cb, u, mqn, sig = _in_proj(
        x2d, g_mix[l][None, :], w_in[l].astype(BF16), freq, sign, cos_r, sin_r, bd,
        jnp.tile(g_q_diff[l], SEC // DIFF_HEAD_DIM)[None, :],
        jnp.tile(g_k_diff[l], SEC // DIFF_HEAD_DIM)[None, :],
        g_q_mem[l][None, :], seq)
    lam_rows = jnp.stack([lambda_q1[l], lambda_k1[l], lambda_q2[l], lambda_k2[l]]).astype(F32)
    score_bound = (DIFF_HEAD_DIM * (DIFF_HEAD_DIM ** -0.5 * LOG2E)
                   * jnp.max(jnp.abs(g_q_diff[l])) * jnp.max(jnp.abs(g_k_diff[l]))).astype(F32)
    y_diff = _diff_attn(qn, kn, vt, lam_rows, g_subln[l][None, :], score_bound, batch, seq, lam_init)
    x1, rows, slots, gates, cbase, ntile, off, counts = _post(
        x2d, y_diff, cb, u, mqn, sig, km, vm, conv_w[l], w_branch[l].astype(BF16),
        w_out[l].astype(BF16), g_ffn[l][None, :], w_router[l], b_router[l][:, None], tri, seq)

    n_tiles = t // TM_POST
    n_assign = t * TOP_K
    n_rows = -(-(n_assign + N_EXPERTS * (MOE_BLOCK - 1)) // MOE_BLOCK) * MOE_BLOCK
    n_blocks = n_rows // MOE_BLOCK
    cnt = counts[:, 0]
    padded = (cnt + MOE_BLOCK - 1) // MOE_BLOCK * MOE_BLOCK
    pend = jnp.cumsum(padded)
    pstart = pend - padded
    block_row0 = jnp.arange(n_blocks, dtype=jnp.int32) * MOE_BLOCK
    block_expert = jnp.minimum(
        jnp.sum((pend[None, :] <= block_row0[:, None]).astype(jnp.int32), axis=1), N_EXPERTS - 1)
    n_active = (pend[-1:] // MOE_BLOCK).astype(jnp.int32)
    cbase_t = cbase[:, 0].reshape(n_tiles, N_EXPERTS)
    ntile_t = ntile[:, 0].reshape(n_tiles, N_EXPERTS)
    off_t = off[:, 0].reshape(n_tiles, N_EXPERTS)
    run_end = (pstart[None, :] + cbase_t + ntile_t).reshape(1, -1)
    first_tile = (jnp.sum((run_end <= block_row0[:, None]).astype(jnp.int32), axis=1)
                  - block_expert * n_tiles)
    first_tile = jnp.clip(first_tile, 0, n_tiles - 1).astype(jnp.int32)

    zero_rows = jnp.zeros((MOE_BLOCK * ROW_CHUNKS, LANES), F32)
    ys_rows = _experts(block_expert, first_tile, n_active, pstart, cnt,
                       cbase_t.T.reshape(-1), ntile_t.T.reshape(-1), off_t.T.reshape(-1),
                       rows, zero_rows, w_gate_up[l], b_gate_up[l][:, None, :],
                       w_down[l], b_down[l][:, None, :], n_tiles)
    out = _combine(pstart, cbase_t.reshape(-1), ntile_t.reshape(-1), off_t.reshape(-1), slots, gates,
                   x1, ys_rows)
    return out.reshape(batch, seq, d)
```
